```python
import math
import jax, jax.numpy as jnp
from jax import lax
import numpy as np

D_MODEL = 1024
BATCH = 4
SEQ = 8192
DEPTH = 2

GRID_W = 64
CTX_LEN = 256
EPS = 1e-6

SSD_INNER = 1024
SSD_HEAD_DIM = 64
SSD_HEADS = SSD_INNER // SSD_HEAD_DIM
SSD_GROUPS = 4
SSD_HPG = SSD_HEADS // SSD_GROUPS
SSD_STATE = 128
SSD_CONV = 5
SSD_CHUNK = 128
SSD_CONV_DIM = SSD_INNER + 2 * SSD_GROUPS * SSD_STATE
ROPE_FREQS = SSD_STATE // 4
ROPE_BASE = 10000.0

NA_HEAD_DIM = 64
NA_WIDTH = 512
NA_HEADS = NA_WIDTH // NA_HEAD_DIM
NA_ROWS = 8
NA_COLS = 16

GM_WIDTH = 512
GM_GROUPS = 8
GM_GROUP_DIM = GM_WIDTH // GM_GROUPS
GM_CHUNK = 128

N_BRANCH = 3
FFN_HIDDEN = ((8 * D_MODEL + 3 * 256 - 1) // (3 * 256)) * 256

IN_SIZES = (SSD_INNER, SSD_CONV_DIM, 2 * SSD_HEADS, 3 * NA_WIDTH, 2 * GM_WIDTH, N_BRANCH * D_MODEL)
IN_WIDTH = sum(IN_SIZES)

kernel_name = 'hybrid_ssd_natten_gmlp_dit_block'


def split_cols(t, sizes):
    out, start = [], 0
    for s in sizes:
        out.append(t[..., start:start + s])
        start += s
    return out


def rmsnorm(t, w):
    tf = t.astype(jnp.float32)
    tf = tf * lax.rsqrt(jnp.mean(tf * tf, axis=-1, keepdims=True) + EPS)
    return tf.astype(t.dtype) * w


def modulate(t, shift, scale):
    return t * (1 + scale) + shift


def dwconv_centred(t, w, b):
    ch = t.shape[-1]
    y = lax.conv_general_dilated(t, w[:, None, :], window_strides=(1,),
                                 padding=[(SSD_CONV // 2, SSD_CONV // 2)],
                                 dimension_numbers=('NWC', 'WIO', 'NWC'),
                                 feature_group_count=ch)
    return y + b


def rope_2d(t, ang_row, ang_col):
    def rot(u, ang):
        cos = jnp.cos(ang)[:, None, :].astype(u.dtype)
        sin = jnp.sin(ang)[:, None, :].astype(u.dtype)
        u1, u2 = u[..., :ROPE_FREQS], u[..., ROPE_FREQS:]
        return jnp.concatenate([u1 * cos - u2 * sin, u1 * sin + u2 * cos], axis=-1)
    half = SSD_STATE // 2
    return jnp.concatenate([rot(t[..., :half], ang_row), rot(t[..., half:], ang_col)], axis=-1)


def ssd_scan(x, dt, A, Bm, Cm, h0, return_y):
    Bt, L, G, R, P = x.shape
    N = Bm.shape[-1]
    Q = SSD_CHUNK
    nc = L // Q
    f32 = jnp.float32
    xdt = (x.astype(f32) * dt[..., None]).reshape(Bt, nc, Q, G, R, P)
    a_cum = jnp.cumsum((dt * A).reshape(Bt, nc, Q, G, R), axis=2)
    Bc = Bm.astype(f32).reshape(Bt, nc, Q, G, N)
    Cc = Cm.astype(f32).reshape(Bt, nc, Q, G, N)
    states = jnp.einsum('bcsgn,bcsgr,bcsgrp->bcgrpn', Bc, jnp.exp(a_cum[:, :, -1:] - a_cum), xdt)
    chunk_decay = jnp.exp(a_cum[:, :, -1])

    def step(h, inp):
        s_c, d_c = inp
        return d_c[..., None, None] * h + s_c, h

    h_final, h_prev = lax.scan(step, h0, (jnp.moveaxis(states, 1, 0), jnp.moveaxis(chunk_decay, 1, 0)))
    if not return_y:
        return None, h_final
    h_prev = jnp.moveaxis(h_prev, 0, 1)
    tri = jnp.tril(jnp.ones((Q, Q), bool))[None, None, :, :, None, None]
    seg = a_cum[:, :, :, None] - a_cum[:, :, None, :]
    decay_ls = jnp.exp(jnp.where(tri, seg, -jnp.inf))
    cb = jnp.einsum('bclgn,bcsgn->bclsg', Cc, Bc)
    y = (jnp.einsum('bclsg,bclsgr,bcsgrp->bclgrp', cb, decay_ls, xdt)
         + jnp.einsum('bclgn,bcgrpn,bclgr->bclgrp', Cc, h_prev, jnp.exp(a_cum)))
    return y.reshape(Bt, L, G, R, P).astype(x.dtype), h_final


def ssd_stream(z, xbc, dt_raw, angles, h0_f, h0_b, conv_w, conv_b, a_log, dt_bias, d_skip, norm_w, return_y):
    Bt, L, _ = xbc.shape
    xbc = jax.nn.silu(dwconv_centred(xbc, conv_w, conv_b))
    xs, Bm, Cm = split_cols(xbc, (SSD_INNER, SSD_GROUPS * SSD_STATE, SSD_GROUPS * SSD_STATE))
    Bm = Bm.reshape(Bt, L, SSD_GROUPS, SSD_STATE)
    Cm = Cm.reshape(Bt, L, SSD_GROUPS, SSD_STATE)
    if angles is not None:
        Bm = rope_2d(Bm, *angles)
        Cm = rope_2d(Cm, *angles)
    xs = xs.reshape(Bt, L, SSD_GROUPS, SSD_HPG, SSD_HEAD_DIM)
    dt = jax.nn.softplus((dt_raw + dt_bias.reshape(-1)).astype(jnp.float32))
    dt = dt.reshape(Bt, L, 2, SSD_GROUPS, SSD_HPG)
    A = -jnp.exp(a_log.astype(jnp.float32)).reshape(2, SSD_GROUPS, SSD_HPG)
    flip = lambda t: jnp.flip(t, axis=1)
    y_f, s_f = ssd_scan(xs, dt[:, :, 0], A[0], Bm, Cm, h0_f, return_y)
    y_b, s_b = ssd_scan(flip(xs), flip(dt[:, :, 1]), A[1], flip(Bm), flip(Cm), h0_b, return_y)
    if not return_y:
        return None, s_f, s_b
    y = y_f + flip(y_b) + d_skip.reshape(SSD_GROUPS, SSD_HPG)[:, :, None] * xs
    y = rmsnorm(y.reshape(Bt, L, SSD_INNER) * jax.nn.silu(z), norm_w)
    return y, s_f, s_b


def qkv_heads(t, q_norm, k_norm):
    Bt, L, _ = t.shape
    q, k, v = [u.reshape(Bt, L, NA_HEADS, NA_HEAD_DIM) for u in split_cols(t, (NA_WIDTH,) * 3)]
    return rmsnorm(q, q_norm), rmsnorm(k, k_norm), v


def na_latent(q, k, v, kc, vc, rpb):
    Bt, L, H, Dh = q.shape
    rows_n = L // GRID_W
    wr = min(NA_ROWS, rows_n)
    scale = Dh ** -0.5
    qg = q.reshape(Bt, rows_n, GRID_W, H, Dh)
    kg = k.reshape(Bt, rows_n, GRID_W, H, Dh)
    vg = v.reshape(Bt, rows_n, GRID_W, H, Dh)
    r = jnp.arange(rows_n)
    rs = jnp.clip(r - wr // 2, 0, rows_n - wr)
    row_idx = rs[:, None] + jnp.arange(wr)
    k_win = kg[:, row_idx]
    v_win = vg[:, row_idx]
    cidx = jnp.arange(GRID_W)
    cs = jnp.clip(cidx - NA_COLS // 2, 0, GRID_W - NA_COLS)
    col_ok = (cidx[None, :] >= cs[:, None]) & (cidx[None, :] < cs[:, None] + NA_COLS)
    ri = row_idx - r[:, None] + (NA_ROWS - 1)
    ci = jnp.clip(cidx[None, :] - cidx[:, None] + (NA_COLS - 1), 0, 2 * NA_COLS - 2)
    bias = jnp.transpose(rpb, (1, 2, 0))[ri[:, None, :, None], ci[None, :, None, :]]
    s_lat = jnp.einsum('brqhd,brjkhd->brqjkh', qg, k_win) * scale + bias
    s_lat = jnp.where(col_ok[None, None, :, None, :, None], s_lat, -jnp.inf)
    s_lat = s_lat.reshape(Bt, rows_n, GRID_W, wr * GRID_W, H)
    s_ctx = jnp.einsum('brqhd,bchd->brqch', qg, kc) * scale
    p = jax.nn.softmax(jnp.concatenate([s_lat, s_ctx], axis=3).astype(jnp.float32), axis=3).astype(v.dtype)
    p_lat = p[:, :, :, :wr * GRID_W].reshape(Bt, rows_n, GRID_W, wr, GRID_W, H)
    p_ctx = p[:, :, :, wr * GRID_W:]
    out = (jnp.einsum('brqjkh,brjkhd->brqhd', p_lat, v_win)
           + jnp.einsum('brqch,bchd->brqhd', p_ctx, vc))
    return out.reshape(Bt, L, H * Dh)


def attn_ctx(q, k, v):
    Bt, Lc, H, Dh = q.shape
    s = jnp.einsum('bqhd,bkhd->bhqk', q, k) * (Dh ** -0.5)
    p = jax.nn.softmax(s.astype(jnp.float32), axis=-1).astype(v.dtype)
    return jnp.einsum('bhqk,bkhd->bqhd', p, v).reshape(Bt, Lc, H * Dh)


def gmlp_mix(uv, norm_w, w_s, b_s):
    u, v = jnp.split(jax.nn.gelu(uv), 2, axis=-1)
    v = rmsnorm(v, norm_w)
    Bt, L, _ = v.shape
    vg = v.reshape(Bt, L // GM_CHUNK, GM_CHUNK, GM_GROUPS, GM_GROUP_DIM)
    mixed = jnp.einsum('gts,bcsgd->bctgd', w_s, vg) + b_s.T[:, :, None]
    return u * mixed.reshape(Bt, L, GM_WIDTH)


def merge_branches(gate_raw, b_gate, y_ssd, y_na, y_gm, w_a, w_b, w_c):
    g_a, g_b, g_c = jnp.split(jax.nn.sigmoid(gate_raw + b_gate), N_BRANCH, axis=-1)
    return g_a * (y_ssd @ w_a) + g_b * (y_na @ w_b) + g_c * (y_gm @ w_c)


def swiglu(h, w_in, w_out):
    a, b = jnp.split(h @ w_in, 2, axis=-1)
    return (jax.nn.silu(a) * b) @ w_out


def hybrid_layer(x, xc, mod, mod_c, angles, norm1, w_in, b_gate, conv_w, conv_b, a_log, dt_bias,
                 d_skip, ssd_norm, q_norm, k_norm, rpb, gm_norm, w_spatial, b_spatial,
                 w_branch_ssd, w_branch_na, w_branch_gm, w_out, norm2, w_ffn_in, w_ffn_out,
                 with_ctx_out):
    sh1, sc1, g1, sh2, sc2, g2 = [m[:, None, :] for m in jnp.split(mod, 6, axis=-1)]
    csh1, csc1, cg1, csh2, csc2, cg2 = jnp.split(mod_c, 6, axis=-1)
    h = modulate(rmsnorm(x, norm1), sh1, sc1)
    hc = modulate(rmsnorm(xc, norm1), csh1, csc1)
    z, xbc, dt_raw, qkv, uv, gate_raw = split_cols(h @ w_in, IN_SIZES)
    zc, xbcc, dtc, qkvc, uvc, gate_rawc = split_cols(hc @ w_in, IN_SIZES)

    ssd_p = (conv_w, conv_b, a_log, dt_bias, d_skip, ssd_norm)
    h0 = jnp.zeros((x.shape[0], SSD_GROUPS, SSD_HPG, SSD_HEAD_DIM, SSD_STATE), jnp.float32)
    y_ssd_c, s_f, s_b = ssd_stream(zc, xbcc, dtc, None, h0, h0, *ssd_p, return_y=with_ctx_out)
    y_ssd, _, _ = ssd_stream(z, xbc, dt_raw, angles, s_f, s_b, *ssd_p, return_y=True)

    q, k, v = qkv_heads(qkv, q_norm, k_norm)
    qc, kc, vc = qkv_heads(qkvc, q_norm, k_norm)
    y_na = na_latent(q, k, v, kc, vc, rpb)

    y_gm = gmlp_mix(uv, gm_norm, w_spatial, b_spatial)

    mixed = merge_branches(gate_raw, b_gate, y_ssd, y_na, y_gm, w_branch_ssd, w_branch_na, w_branch_gm)
    x = x + g1 * (mixed @ w_out)
    x = x + g2 * swiglu(modulate(rmsnorm(x, norm2), sh2, sc2), w_ffn_in, w_ffn_out)

    if with_ctx_out:
        y_na_c = attn_ctx(qc, kc, vc)
        y_gm_c = gmlp_mix(uvc, gm_norm, w_spatial, b_spatial)
        mixed_c = merge_branches(gate_rawc, b_gate, y_ssd_c, y_na_c, y_gm_c,
                                 w_branch_ssd, w_branch_na, w_branch_gm)
        xc = xc + cg1 * (mixed_c @ w_out)
        xc = xc + cg2 * swiglu(modulate(rmsnorm(xc, norm2), csh2, csc2), w_ffn_in, w_ffn_out)
    return x, xc


def setup_inputs(seed: int = 0) -> dict:
    key = jax.random.key(seed)
    ks = jax.random.split(key, 28)
    f32 = jnp.float32
    nrm = lambda k, shape, s: jax.random.normal(k, shape, f32) * s
    D = D_MODEL
    dt0 = jnp.exp(jax.random.uniform(ks[12], (DEPTH, 2, SSD_HEADS), f32,
                                     minval=math.log(1e-3), maxval=math.log(1e-1)))
    return {
        'x': nrm(ks[0], (BATCH, SEQ, D), 1.0),
        'c': nrm(ks[1], (BATCH, D), 1.0),
        'ctx': nrm(ks[2], (BATCH, CTX_LEN, D), 1.0),
        'c_ctx': nrm(ks[3], (D,), 1.0),
        'w_mod': nrm(ks[4], (DEPTH, D, 6 * D), D ** -0.5),
        'b_mod': nrm(ks[5], (DEPTH, 6 * D), 0.02),
        'norm1': 1.0 + nrm(ks[6], (DEPTH, D), 0.02),
        'w_in': nrm(ks[7], (DEPTH, D, IN_WIDTH), D ** -0.5),
        'b_gate': nrm(ks[8], (DEPTH, N_BRANCH * D), 0.02),
        'conv_w': nrm(ks[9], (DEPTH, SSD_CONV, SSD_CONV_DIM), SSD_CONV ** -0.5),
        'conv_b': nrm(ks[10], (DEPTH, SSD_CONV_DIM), 0.02),
        'a_log': jnp.log(jax.random.uniform(ks[11], (DEPTH, 2, SSD_HEADS), f32, minval=1.0, maxval=16.0)),
        'dt_bias': dt0 + jnp.log(-jnp.expm1(-dt0)),
        'd_skip': 1.0 + nrm(ks[13], (DEPTH, SSD_HEADS), 0.02),
        'ssd_norm': 1.0 + nrm(ks[14], (DEPTH, SSD_INNER), 0.02),
        'q_norm': 1.0 + nrm(ks[15], (DEPTH, NA_HEAD_DIM), 0.02),
        'k_norm': 1.0 + nrm(ks[16], (DEPTH, NA_HEAD_DIM), 0.02),
        'rpb': nrm(ks[17], (DEPTH, NA_HEADS, 2 * NA_ROWS - 1, 2 * NA_COLS - 1), 0.1),
        'gm_norm': 1.0 + nrm(ks[18], (DEPTH, GM_WIDTH), 0.02),
        'w_spatial': nrm(ks[19], (DEPTH, GM_GROUPS, GM_CHUNK, GM_CHUNK), GM_CHUNK ** -0.5),
        'b_spatial': 1.0 + nrm(ks[20], (DEPTH, GM_GROUPS, GM_CHUNK), 0.02),
        'w_branch_ssd': nrm(ks[21], (DEPTH, SSD_INNER, D), SSD_INNER ** -0.5),
        'w_branch_na': nrm(ks[22], (DEPTH, NA_WIDTH, D), NA_WIDTH ** -0.5),
        'w_branch_gm': nrm(ks[23], (DEPTH, GM_WIDTH, D), GM_WIDTH ** -0.5),
        'w_out': nrm(ks[24], (DEPTH, D, D), D ** -0.5),
        'norm2': 1.0 + nrm(ks[25], (DEPTH, D), 0.02),
        'w_ffn_in': nrm(ks[26], (DEPTH, D, 2 * FFN_HIDDEN), D ** -0.5),
        'w_ffn_out': nrm(ks[27], (DEPTH, FFN_HIDDEN, D), FFN_HIDDEN ** -0.5),
    }


def reference(x, c, ctx, c_ctx, w_mod, b_mod, norm1, w_in, b_gate, conv_w, conv_b, a_log, dt_bias,
              d_skip, ssd_norm, q_norm, k_norm, rpb, gm_norm, w_spatial, b_spatial,
              w_branch_ssd, w_branch_na, w_branch_gm, w_out, norm2, w_ffn_in, w_ffn_out):
    L = x.shape[1]
    pos = jnp.arange(L)
    freqs = ROPE_BASE ** (-jnp.arange(ROPE_FREQS, dtype=jnp.float32) / ROPE_FREQS)
    ang_row = (pos // GRID_W).astype(jnp.float32)[:, None] * freqs
    ang_col = (pos % GRID_W).astype(jnp.float32)[:, None] * freqs
    angles = (ang_row, ang_col)
    sc = jax.nn.silu(c)
    scc = jax.nn.silu(c_ctx)
    xc = ctx
    for l in range(DEPTH):
        mod = sc @ w_mod[l] + b_mod[l]
        mod_c = scc @ w_mod[l] + b_mod[l]
        x, xc = hybrid_layer(x, xc, mod, mod_c, angles, norm1[l], w_in[l], b_gate[l], conv_w[l], conv_b[l],
                             a_log[l], dt_bias[l], d_skip[l], ssd_norm[l], q_norm[l], k_norm[l], rpb[l],
                             gm_norm[l], w_spatial[l], b_spatial[l], w_branch_ssd[l], w_branch_na[l],
                             w_branch_gm[l], w_out[l], norm2[l], w_ffn_in[l], w_ffn_out[l],
                             with_ctx_out=(l < DEPTH - 1))
    return x
```

```python
import functools
import math

import jax
import jax.numpy as jnp
from jax import lax
from jax.experimental import pallas as pl
from jax.experimental.pallas import tpu as pltpu

F32 = jnp.float32
BF16 = jnp.bfloat16

EPS = 1e-6
GRID_W = 64

SSD_INNER = 1024
SSD_HEAD_DIM = 64
SSD_HEADS = 16
SSD_GROUPS = 4
SSD_STATE = 128
SSD_CONV = 5
SSD_CHUNK = 128
SSD_BC = SSD_GROUPS * SSD_STATE
SSD_CONV_DIM = SSD_INNER + 2 * SSD_BC
ROPE_FREQS = 32
ROPE_BASE = 10000.0

NA_HEAD_DIM = 64
NA_WIDTH = 512
NA_HEADS = 8
NA_ROWS = 8
NA_COLS = 16

GM_WIDTH = 512
GM_GROUPS = 8
GM_CHUNK = 128

LANES = 128
HALO = 16
NEG = -1e30
VMEM_LIMIT = 52 * 1024 * 1024


def _dot(a, b):
    return jnp.dot(a, b, preferred_element_type=F32)


def _dot_nt(a, b):
    return lax.dot_general(a, b, (((1,), (1,)), ((), ())), preferred_element_type=F32)


def _silu(x):
    return x / (1.0 + jnp.exp(-x))


def _sigmoid(x):
    return 1.0 / (1.0 + jnp.exp(-x))


def _gelu_tanh(x):
    return 0.5 * x * (1.0 + jnp.tanh(math.sqrt(2.0 / math.pi) * (x + 0.044715 * (x * x * x))))


def _softplus(x):
    return jnp.maximum(x, 0.0) + jnp.log(1.0 + jnp.exp(-jnp.abs(x)))


def _split_hi_lo(v):
    hi = v.astype(BF16)
    lo = (v - hi.astype(F32)).astype(BF16)
    return hi, lo


def _const_spec(shape):
    nd = len(shape)
    return pl.BlockSpec(shape, lambda *_: (0,) * nd, pipeline_mode=pl.Buffered(1))


def _mod_kernel(c_ref, w_ref, b_ref, o_ref):
    o_ref[0] = _dot(_silu(c_ref[...]), w_ref[0]) + b_ref[0]


def _modulation(c_all, w_mod, b_mod):
    depth, d, n = w_mod.shape
    tn = 1536
    return pl.pallas_call(
        _mod_kernel,
        out_shape=jax.ShapeDtypeStruct((depth, 8, n), F32),
        grid=(depth, n // tn),
        in_specs=[pl.BlockSpec((8, d), lambda l, j: (0, 0)),
                  pl.BlockSpec((1, d, tn), lambda l, j: (l, 0, j)),
                  pl.BlockSpec((1, 1, tn), lambda l, j: (l, 0, j))],
        out_specs=pl.BlockSpec((1, 8, tn), lambda l, j: (l, 0, j)),
        compiler_params=pltpu.CompilerParams(dimension_semantics=("arbitrary", "arbitrary"),
                                             vmem_limit_bytes=VMEM_LIMIT),
        name="modulation",
    )(c_all, w_mod, b_mod.reshape(depth, 1, n))


def _inproj_kernel(x_ref, mod_ref, n1_ref, wz, wxbc, wdt, wq, wk, wv, wuv, wg, dtb, bg, qn, kn, blk,
                   gmn, ws, bsp, z_o, xbc_o, dt_o, q_o, k_o, v_o, ygm_o, gate_o):
    tm = x_ref.shape[1]
    x = x_ref[0]
    xn = x * lax.rsqrt(jnp.mean(x * x, axis=-1, keepdims=True) + EPS) * n1_ref[...]
    hb = (xn * (1.0 + mod_ref[0, 1:2, :]) + mod_ref[0, 0:1, :]).astype(BF16)

    cw = 512
    for n0 in range(0, z_o.shape[2], cw):
        z_o[0, :, n0:n0 + cw] = _dot(hb, wz[:, n0:n0 + cw]).astype(BF16)
    for n0 in range(0, xbc_o.shape[2], cw):
        xbc_o[0, :, n0:n0 + cw] = _dot(hb, wxbc[:, n0:n0 + cw]).astype(BF16)
    for n0 in range(0, gate_o.shape[2], cw):
        gate_o[0, :, n0:n0 + cw] = _sigmoid(_dot(hb, wg[:, n0:n0 + cw]) + bg[:, n0:n0 + cw]).astype(BF16)

    dt_o[0] = _softplus(_dot(hb, wdt[...]) + dtb[...])

    def head_norm(t, w_row):
        hi, lo = _split_hi_lo(t * t)
        ms = _dot(hi, blk[...]) + _dot(lo, blk[...])
        return t * lax.rsqrt(ms + EPS) * w_row

    q_o[0] = (head_norm(_dot(hb, wq[...]), qn[...]) * (NA_HEAD_DIM ** -0.5)).astype(BF16)
    k_o[0] = head_norm(_dot(hb, wk[...]), kn[...]).astype(BF16)
    v_o[0] = _dot(hb, wv[...]).astype(BF16)

    g = _gelu_tanh(_dot(hb, wuv[...]))
    u = g[:, :GM_WIDTH]
    v = g[:, GM_WIDTH:]
    vb = (v * lax.rsqrt(jnp.mean(v * v, axis=-1, keepdims=True) + EPS) * gmn[...]).astype(BF16)
    lane = lax.broadcasted_iota(jnp.int32, (GM_CHUNK, LANES), 1)
    for c0 in range(0, tm, GM_CHUNK):
        for p in range(GM_WIDTH // LANES):
            cols = slice(p * LANES, (p + 1) * LANES)
            vp = vb[c0:c0 + GM_CHUNK, cols]
            mixed = jnp.where(lane < LANES // 2, _dot(ws[2 * p], vp), _dot(ws[2 * p + 1], vp)) + bsp[:, cols]
            ygm_o[0, c0:c0 + GM_CHUNK, cols] = (u[c0:c0 + GM_CHUNK, cols] * mixed).astype(BF16)


def _input_projection(x_all, mod_tok, lw, n_ctx, tm=256):
    b, s, d = x_all.shape
    nct = n_ctx // tm
    tok = lambda c: pl.BlockSpec((1, tm, c), lambda i, j: (i, j, 0))
    consts = [lw["norm1"], lw["w_z"], lw["w_xbc"], lw["w_dt"], lw["w_q"], lw["w_k"], lw["w_v"], lw["w_uv"],
              lw["w_gate"], lw["dt_bias"], lw["b_gate"], lw["q_norm"], lw["k_norm"], lw["head_blk"],
              lw["gm_norm"], lw["w_spatial"], lw["b_spatial"]]
    widths = [(SSD_INNER, BF16), (SSD_CONV_DIM, BF16), (LANES, F32), (NA_WIDTH, BF16), (NA_WIDTH, BF16),
              (NA_WIDTH, BF16), (GM_WIDTH, BF16), (3 * d, BF16)]
    return pl.pallas_call(
        _inproj_kernel,
        out_shape=[jax.ShapeDtypeStruct((b, s, c), t) for c, t in widths],
        grid=(b, s // tm),
        in_specs=[tok(d),
                  pl.BlockSpec((1, 6, d), lambda i, j: (2 * i + jnp.where(j >= nct, 1, 0), 0, 0))]
                 + [_const_spec(a.shape) for a in consts],
        out_specs=[tok(c) for c, _ in widths],
        compiler_params=pltpu.CompilerParams(dimension_semantics=("parallel", "parallel"),
                                             vmem_limit_bytes=VMEM_LIMIT),
        name="input_projection",
    )(x_all, mod_tok, *consts)


def _ssd_chunk(xs, bm, cm, dt, a_row, e_ref, tri_ref, h_ref, reverse, head_off):
    q = xs.shape[0]
    last = 0 if reverse else q - 1
    a = dt * a_row
    a1 = a.astype(BF16)
    r1 = a - a1.astype(F32)
    a2 = r1.astype(BF16)
    a3 = (r1 - a2.astype(F32)).astype(BF16)
    tri = tri_ref[...]
    acum = _dot(tri, a1) + _dot(tri, a2) + _dot(tri, a3)
    acum_t = acum.T
    a_last = acum[last:last + 1, :]

    def expand(v):
        hi, lo = _split_hi_lo(v)
        return _dot(hi, e_ref[...]) + _dot(lo, e_ref[...])

    dec_out = expand(jnp.exp(acum))
    xdt = xs * expand(dt)
    xdt_b = xdt.astype(BF16)
    xw_b = (xdt * expand(jnp.exp(a_last - acum))).astype(BF16)
    h_dec = dec_out[last:last + 1, :]

    row = lax.broadcasted_iota(jnp.int32, (q, q), 0)
    col = lax.broadcasted_iota(jnp.int32, (q, q), 1)
    in_scan = (col >= row) if reverse else (col <= row)
    lane = lax.broadcasted_iota(jnp.int32, (q, LANES), 1)

    gw = SSD_INNER // SSD_GROUPS
    ys = []
    for g in range(SSD_GROUPS):
        bg = bm[:, g * SSD_STATE:(g + 1) * SSD_STATE]
        cgb = cm[:, g * SSD_STATE:(g + 1) * SSD_STATE].astype(BF16)
        cb = _dot_nt(cgb, bg.astype(BF16))
        h_prev = h_ref[:, g * gw:(g + 1) * gw]
        y_g = _dot(cgb, h_prev.astype(BF16)) * dec_out[:, g * gw:(g + 1) * gw]
        pairs = []
        for pr in range(gw // LANES):
            c0 = g * gw + pr * LANES
            xp = xdt_b[:, c0:c0 + LANES]
            outs = []
            for hh in range(2):
                hc = head_off + (c0 // SSD_HEAD_DIM) + hh
                seg = acum[:, hc:hc + 1] - acum_t[hc:hc + 1, :]
                decay = jnp.exp(jnp.where(in_scan, seg, NEG))
                outs.append(_dot((cb * decay).astype(BF16), xp))
            pairs.append(jnp.where(lane < SSD_HEAD_DIM, outs[0], outs[1]))
        ys.append(y_g + jnp.concatenate(pairs, axis=1))
        h_ref[:, g * gw:(g + 1) * gw] = (h_prev * h_dec[:, g * gw:(g + 1) * gw]
                                         + _dot(bg.T.astype(BF16), xw_b[:, g * gw:(g + 1) * gw]))
    return jnp.concatenate(ys, axis=1)


def _ssd_fwd_kernel(cur_ref, prev_ref, next_ref, dt_ref, cos_ref, sin_ref, cw_ref, cb_ref, a_ref, e_ref,
                    tri_ref, dsk_ref, xbc_o, y_o, ext_ref, h_ref, *, nc_ctx):
    i = pl.program_id(1)
    nc = pl.num_programs(1)
    q = cur_ref.shape[1]

    @pl.when(i == 0)
    def _():
        h_ref[...] = jnp.zeros_like(h_ref)

    pv = jnp.where((i == 0) | (i == nc_ctx), 0.0, 1.0)
    nv = jnp.where((i == nc_ctx - 1) | (i == nc - 1), 0.0, 1.0)
    ext_ref[0:HALO, :] = prev_ref[0].astype(F32) * pv
    ext_ref[HALO:HALO + q, :] = cur_ref[0].astype(F32)
    ext_ref[HALO + q:2 * HALO + q, :] = next_ref[0].astype(F32) * nv
    parts = []
    cwid = 256
    for c0 in range(0, SSD_CONV_DIM, cwid):
        acc = cb_ref[:, c0:c0 + cwid]
        for k in range(SSD_CONV):
            r0 = HALO - SSD_CONV // 2 + k
            acc = acc + cw_ref[k:k + 1, c0:c0 + cwid] * ext_ref[r0:r0 + q, c0:c0 + cwid]
        parts.append(_silu(acc))
    xs = jnp.concatenate(parts[:SSD_INNER // cwid], axis=1)

    lane = lax.broadcasted_iota(jnp.int32, (q, LANES), 1)
    cos = cos_ref[...]
    sin = sin_ref[...]

    def rope(t):
        sw = jnp.where((lane & ROPE_FREQS) == 0, pltpu.roll(t, LANES - ROPE_FREQS, 1), pltpu.roll(t, ROPE_FREQS, 1))
        return t * cos + sw * sin

    bc = jnp.concatenate(parts[SSD_INNER // cwid:], axis=1)
    bc = jnp.concatenate([rope(bc[:, g * LANES:(g + 1) * LANES]) for g in range(2 * SSD_GROUPS)], axis=1)
    bm = bc[:, :SSD_BC]
    cm = bc[:, SSD_BC:]

    xbc_o[0, :, :SSD_INNER] = xs.astype(BF16)
    xbc_o[0, :, SSD_INNER:] = bc.astype(BF16)

    y = _ssd_chunk(xs, bm, cm, dt_ref[0], a_ref[...], e_ref, tri_ref, h_ref, False, 0)
    y_o[0] = (y + dsk_ref[...] * xs).astype(BF16)


def _ssd_bwd_kernel(xbc_ref, dt_ref, yf_ref, a_ref, e_ref, tri_ref, y_o, h_ref):
    @pl.when(pl.program_id(1) == 0)
    def _():
        h_ref[...] = jnp.zeros_like(h_ref)

    xbc = xbc_ref[0].astype(F32)
    y = _ssd_chunk(xbc[:, :SSD_INNER], xbc[:, SSD_INNER:SSD_INNER + SSD_BC], xbc[:, SSD_INNER + SSD_BC:],
                   dt_ref[0], a_ref[...], e_ref, tri_ref, h_ref, True, SSD_HEADS)
    y_o[0] = (y + yf_ref[0].astype(F32)).astype(BF16)


def _ssd(xbc, dt, lw, consts, n_ctx):
    b, s, _ = xbc.shape
    q = SSD_CHUNK
    nc = s // q
    nc_ctx = n_ctx // q
    hb = q // HALO
    n_hb = s // HALO
    chunk = lambda c: pl.BlockSpec((1, q, c), lambda i, j: (i, j, 0))
    scratch_h = pltpu.VMEM((SSD_STATE, SSD_INNER), F32)
    params = pltpu.CompilerParams(dimension_semantics=("parallel", "arbitrary"), vmem_limit_bytes=VMEM_LIMIT)

    xbc_c, y_f = pl.pallas_call(
        functools.partial(_ssd_fwd_kernel, nc_ctx=nc_ctx),
        out_shape=[jax.ShapeDtypeStruct((b, s, SSD_CONV_DIM), BF16), jax.ShapeDtypeStruct((b, s, SSD_INNER), BF16)],
        grid=(b, nc),
        in_specs=[chunk(SSD_CONV_DIM),
                  pl.BlockSpec((1, HALO, SSD_CONV_DIM), lambda i, j: (i, jnp.maximum(j * hb - 1, 0), 0)),
                  pl.BlockSpec((1, HALO, SSD_CONV_DIM), lambda i, j: (i, jnp.minimum((j + 1) * hb, n_hb - 1), 0)),
                  chunk(LANES),
                  pl.BlockSpec((q, LANES), lambda i, j: (j, 0)),
                  pl.BlockSpec((q, LANES), lambda i, j: (j, 0)),
                  _const_spec(lw["conv_w"].shape), _const_spec(lw["conv_b"].shape),
                  _const_spec(lw["a_fwd"].shape), _const_spec(consts["e_fwd"].shape),
                  _const_spec(consts["tri_fwd"].shape), _const_spec(lw["d_skip"].shape)],
        out_specs=[chunk(SSD_CONV_DIM), chunk(SSD_INNER)],
        scratch_shapes=[pltpu.VMEM((q + 2 * HALO, SSD_CONV_DIM), F32), scratch_h],
        compiler_params=params,
        name="ssd_forward",
    )(xbc, xbc, xbc, dt, consts["rope_cos"], consts["rope_sin"], lw["conv_w"], lw["conv_b"], lw["a_fwd"],
      consts["e_fwd"], consts["tri_fwd"], lw["d_skip"])

    def rev(j):
        return jnp.where(j < nc_ctx, nc_ctx - 1 - j, nc + nc_ctx - 1 - j)

    rchunk = lambda c: pl.BlockSpec((1, q, c), lambda i, j: (i, rev(j), 0))
    return pl.pallas_call(
        _ssd_bwd_kernel,
        out_shape=jax.ShapeDtypeStruct((b, s, SSD_INNER), BF16),
        grid=(b, nc),
        in_specs=[rchunk(SSD_CONV_DIM), rchunk(LANES), rchunk(SSD_INNER),
                  _const_spec(lw["a_bwd"].shape), _const_spec(consts["e_bwd"].shape),
                  _const_spec(consts["tri_bwd"].shape)],
        out_specs=rchunk(SSD_INNER),
        scratch_shapes=[scratch_h],
        compiler_params=params,
        name="ssd_backward",
    )(xbc_c, dt, y_f, lw["a_bwd"], consts["e_bwd"], consts["tri_bwd"])


def _bias_kernel(rpb_ref, o_ref):
    lh = pl.program_id(0)
    e = pl.program_id(1)
    n_ri = 2 * NA_ROWS - 1
    n_ci = 2 * NA_COLS - 1
    lane = lax.broadcasted_iota(jnp.int32, (GRID_W, LANES), 1)
    qc = lax.broadcasted_iota(jnp.int32, (GRID_W, LANES), 0)
    kc = lane % GRID_W
    half = lane // GRID_W
    cs = jnp.clip(qc - NA_COLS // 2, 0, GRID_W - NA_COLS)
    col_ok = (kc >= cs) & (kc < cs + NA_COLS)
    ci = jnp.clip(kc - qc + (NA_COLS - 1), 0, n_ci - 1)
    acc = jnp.full((GRID_W, LANES), NEG, F32)
    for hf in range(2):
        ri = e - 1 + hf
        ok = col_ok & (half == hf) & (ri >= 0) & (ri < n_ri)
        base = (lh * n_ri + jnp.clip(ri, 0, n_ri - 1)) * n_ci
        for c in range(n_ci):
            acc = jnp.where(ok & (ci == c), rpb_ref[base + c], acc)
    o_ref[0, 0] = acc


def _bias_table(rpb):
    depth, heads, n_ri, n_ci = rpb.shape
    return pl.pallas_call(
        _bias_kernel,
        out_shape=jax.ShapeDtypeStruct((depth * heads, n_ri + 1, GRID_W, LANES), F32),
        grid=(depth * heads, n_ri + 1),
        in_specs=[pl.BlockSpec(memory_space=pltpu.SMEM)],
        out_specs=pl.BlockSpec((1, 1, GRID_W, LANES), lambda i, j: (i, j, 0, 0)),
        compiler_params=pltpu.CompilerParams(dimension_semantics=("arbitrary", "arbitrary")),
        name="na_bias_table",
    )(rpb.reshape(-1))


def _na_kernel(q_ref, k_ref, v_ref, bias_ref, o_ref, *, n_ctx, rt, nw, rows_n):
    t = pl.program_id(1)
    tq = rt * GRID_W
    nct = n_ctx // tq
    is_lat = t >= nct
    r0 = (t - nct) * rt
    base = jnp.clip(r0 - NA_ROWS // 2, 0, rows_n - nw)
    start = pl.multiple_of(n_ctx + base * GRID_W, GRID_W)
    lane_q = lax.broadcasted_iota(jnp.int32, (tq, LANES), 1)
    lane_r = lax.broadcasted_iota(jnp.int32, (GRID_W, LANES), 1)
    nblk = nw // 2

    ents, valid = [], []
    for qi in range(rt):
        r = r0 + qi
        rs = jnp.clip(r - NA_ROWS // 2, 0, rows_n - NA_ROWS)
        e_row, v_row = [], []
        for m in range(nblk):
            j0 = base + 2 * m
            e_row.append(jnp.clip(j0 - r + NA_ROWS, 0, 2 * NA_ROWS - 1))
            ok0 = is_lat & (j0 >= rs) & (j0 < rs + NA_ROWS)
            ok1 = is_lat & (j0 + 1 >= rs) & (j0 + 1 < rs + NA_ROWS)
            v_row.append(jnp.where(lane_r < GRID_W, ok0.astype(jnp.int32), ok1.astype(jnp.int32)) > 0)
        ents.append(e_row)
        valid.append(v_row)

    for p in range(NA_WIDTH // LANES):
        cols = slice(p * LANES, (p + 1) * LANES)
        qp = q_ref[0, :, cols]
        kc = k_ref[0, 0:n_ctx, cols]
        vc = v_ref[0, 0:n_ctx, cols]
        kw = k_ref[0, pl.ds(start, nw * GRID_W), cols]
        vw = v_ref[0, pl.ds(start, nw * GRID_W), cols]
        outs = []
        for hh in range(2):
            h = 2 * p + hh
            qm = jnp.where((lane_q < NA_HEAD_DIM) == (hh == 0), qp, jnp.zeros_like(qp))
            s_ctx = _dot_nt(qm, kc)
            s_win = _dot_nt(qm, kw)
            p_ctx_rows, p_win_rows, inv_rows = [], [], []
            for qi in range(rt):
                rows = slice(qi * GRID_W, (qi + 1) * GRID_W)
                sc = s_ctx[rows, :]
                blocks = []
                for m in range(nblk):
                    sb = s_win[rows, m * LANES:(m + 1) * LANES] + bias_ref[h, ents[qi][m]]
                    blocks.append(jnp.where(valid[qi][m], sb, NEG))
                mx = jnp.max(sc, axis=-1, keepdims=True)
                for sb in blocks:
                    mx = jnp.maximum(mx, jnp.max(sb, axis=-1, keepdims=True))
                pc = jnp.exp(sc - mx)
                pw = [jnp.exp(sb - mx) for sb in blocks]
                den = jnp.sum(pc, axis=-1, keepdims=True)
                for pb in pw:
                    den = den + jnp.sum(pb, axis=-1, keepdims=True)
                p_ctx_rows.append(pc.astype(BF16))
                p_win_rows.append(jnp.concatenate(pw, axis=1).astype(BF16))
                inv_rows.append(1.0 / den)
            p_ctx = jnp.concatenate(p_ctx_rows, axis=0)
            p_win = jnp.concatenate(p_win_rows, axis=0)
            inv = jnp.concatenate(inv_rows, axis=0)
            outs.append((_dot(p_ctx, vc) + _dot(p_win, vw)) * inv)
        o_ref[0, :, cols] = jnp.where(lane_q < NA_HEAD_DIM, outs[0], outs[1]).astype(BF16)


def _neighbourhood_attention(q, k, v, bias, n_ctx, rt=4):
    b, s, w = q.shape
    rows_n = (s - n_ctx) // GRID_W
    nw = rt + NA_ROWS
    nw += nw % 2
    tq = rt * GRID_W
    whole = pl.BlockSpec((1, s, w), lambda i, j: (i, 0, 0))
    return pl.pallas_call(
        functools.partial(_na_kernel, n_ctx=n_ctx, rt=rt, nw=nw, rows_n=rows_n),
        out_shape=jax.ShapeDtypeStruct((b, s, w), BF16),
        grid=(b, s // tq),
        in_specs=[pl.BlockSpec((1, tq, w), lambda i, j: (i, j, 0)), whole, whole, _const_spec(bias.shape)],
        out_specs=pl.BlockSpec((1, tq, w), lambda i, j: (i, j, 0)),
        compiler_params=pltpu.CompilerParams(dimension_semantics=("parallel", "arbitrary"),
                                             vmem_limit_bytes=VMEM_LIMIT),
        name="neighbourhood_attention",
    )(q, k, v, bias)


def _merge_ffn_kernel(x_ref, mod_ref, gate_ref, yssd_ref, z_ref, yna_ref, ygm_ref, sn_ref, wa, wb, wc, wo,
                      n2_ref, wfi, wfo, o_ref, *, ffn_chunk):
    d = x_ref.shape[2]
    y = yssd_ref[0].astype(F32) * _silu(z_ref[0].astype(F32))
    y = (y * lax.rsqrt(jnp.mean(y * y, axis=-1, keepdims=True) + EPS) * sn_ref[...]).astype(BF16)
    gate = gate_ref[0]
    mixed = (gate[:, 0:d].astype(F32) * _dot(y, wa[...])
             + gate[:, d:2 * d].astype(F32) * _dot(yna_ref[0], wb[...])
             + gate[:, 2 * d:3 * d].astype(F32) * _dot(ygm_ref[0], wc[...]))
    x1 = x_ref[0] + mod_ref[0, 2:3, :] * _dot(mixed.astype(BF16), wo[...])
    xn = x1 * lax.rsqrt(jnp.mean(x1 * x1, axis=-1, keepdims=True) + EPS) * n2_ref[...]
    hb = (xn * (1.0 + mod_ref[0, 4:5, :]) + mod_ref[0, 3:4, :]).astype(BF16)
    hid = wfo.shape[0]
    acc = jnp.zeros_like(x1)
    for c0 in range(0, hid, ffn_chunk):
        a = _dot(hb, wfi[:, c0:c0 + ffn_chunk])
        g = _dot(hb, wfi[:, hid + c0:hid + c0 + ffn_chunk])
        acc = acc + _dot((_silu(a) * g).astype(BF16), wfo[c0:c0 + ffn_chunk, :])
    o_ref[0] = x1 + mod_ref[0, 5:6, :] * acc


def _merge_ffn(x_all, mod_tok, gate, y_ssd, z, y_na, y_gm, lw, n_ctx, tm=256):
    b, s, d = x_all.shape
    nct = n_ctx // tm
    tok = lambda c: pl.BlockSpec((1, tm, c), lambda i, j: (i, j, 0))
    consts = [lw["ssd_norm"], lw["w_branch_ssd"], lw["w_branch_na"], lw["w_branch_gm"], lw["w_out"], lw["norm2"],
              lw["w_ffn_in"], lw["w_ffn_out"]]
    return pl.pallas_call(
        functools.partial(_merge_ffn_kernel, ffn_chunk=256),
        out_shape=jax.ShapeDtypeStruct((b, s, d), F32),
        grid=(b, s // tm),
        in_specs=[tok(d),
                  pl.BlockSpec((1, 6, d), lambda i, j: (2 * i + jnp.where(j >= nct, 1, 0), 0, 0)),
                  tok(3 * d), tok(SSD_INNER), tok(SSD_INNER), tok(NA_WIDTH), tok(GM_WIDTH)]
                 + [_const_spec(a.shape) for a in consts],
        out_specs=tok(d),
        compiler_params=pltpu.CompilerParams(dimension_semantics=("parallel", "parallel"),
                                             vmem_limit_bytes=VMEM_LIMIT),
        name="merge_out_ffn",
    )(x_all, mod_tok, gate, y_ssd, z, y_na, y_gm, *consts)


def _shared_constants(n_ctx, seq):
    pos = jnp.arange(seq)
    freqs = ROPE_BASE ** (-jnp.arange(ROPE_FREQS, dtype=F32) / ROPE_FREQS)
    ang_row = (pos // GRID_W).astype(F32)[:, None] * freqs
    ang_col = (pos % GRID_W).astype(F32)[:, None] * freqs

    def table(fr, fc, ctx_val):
        lat = jnp.concatenate([fr(ang_row)[0], fr(ang_row)[1], fc(ang_col)[0], fc(ang_col)[1]], axis=1)
        return jnp.concatenate([jnp.full((n_ctx, LANES), ctx_val, F32), lat], axis=0)

    cos = table(lambda a: (jnp.cos(a), jnp.cos(a)), lambda a: (jnp.cos(a), jnp.cos(a)), 1.0)
    sin = table(lambda a: (-jnp.sin(a), jnp.sin(a)), lambda a: (-jnp.sin(a), jnp.sin(a)), 0.0)

    r = jnp.arange(SSD_CHUNK)
    tri_fwd = (r[None, :] <= r[:, None]).astype(BF16)
    tri_bwd = (r[None, :] >= r[:, None]).astype(BF16)
    lane_head = jnp.arange(SSD_INNER) // SSD_HEAD_DIM
    rows = jnp.arange(LANES)
    e_fwd = (rows[:, None] == lane_head[None, :]).astype(BF16)
    e_bwd = (rows[:, None] == lane_head[None, :] + SSD_HEADS).astype(BF16)
    return dict(rope_cos=cos, rope_sin=sin, tri_fwd=tri_fwd, tri_bwd=tri_bwd, e_fwd=e_fwd, e_bwd=e_bwd)


def _layer_weights(l, p):
    d = p["w_in"].shape[1]
    sizes = (SSD_INNER, SSD_CONV_DIM, 2 * SSD_HEADS, NA_WIDTH, NA_WIDTH, NA_WIDTH, 2 * GM_WIDTH, 3 * d)
    names = ("w_z", "w_xbc", "w_dt", "w_q", "w_k", "w_v", "w_uv", "w_gate")
    lw, start = {}, 0
    w_in = p["w_in"][l]
    for name, size in zip(names, sizes):
        lw[name] = w_in[:, start:start + size].astype(BF16)
        start += size
    pad_lanes = lambda v: jnp.pad(v, (0, LANES - v.shape[0])).reshape(1, LANES)
    lw["w_dt"] = jnp.pad(lw["w_dt"], ((0, 0), (0, LANES - 2 * SSD_HEADS)))
    lw["dt_bias"] = pad_lanes(p["dt_bias"][l].reshape(-1))
    a = -jnp.exp(p["a_log"][l].astype(F32))
    lw["a_fwd"] = pad_lanes(a[0])
    lw["a_bwd"] = pad_lanes(jnp.concatenate([jnp.zeros((SSD_HEADS,), F32), a[1]]))
    row = lambda v: v.reshape(1, -1)
    lw["norm1"] = row(p["norm1"][l])
    lw["norm2"] = row(p["norm2"][l])
    lw["b_gate"] = row(p["b_gate"][l])
    lw["q_norm"] = row(jnp.tile(p["q_norm"][l], NA_HEADS))
    lw["k_norm"] = row(jnp.tile(p["k_norm"][l], NA_HEADS))
    head = jnp.arange(NA_WIDTH) // NA_HEAD_DIM
    lw["head_blk"] = ((head[:, None] == head[None, :]).astype(F32) / NA_HEAD_DIM).astype(BF16)
    lw["gm_norm"] = row(p["gm_norm"][l])
    lw["w_spatial"] = p["w_spatial"][l].astype(BF16)
    lw["b_spatial"] = jnp.repeat(p["b_spatial"][l].T, GM_WIDTH // GM_GROUPS, axis=1)
    lw["conv_w"] = jnp.pad(p["conv_w"][l], ((0, 8 - SSD_CONV), (0, 0)))
    lw["conv_b"] = row(p["conv_b"][l])
    lw["d_skip"] = row(jnp.repeat(p["d_skip"][l], SSD_HEAD_DIM))
    lw["ssd_norm"] = row(p["ssd_norm"][l])
    for name in ("w_branch_ssd", "w_branch_na", "w_branch_gm", "w_out", "w_ffn_in", "w_ffn_out"):
        lw[name] = p[name][l].astype(BF16)
    return lw


def kernel(x, c, ctx, c_ctx, w_mod, b_mod, norm1, w_in, b_gate, conv_w, conv_b, a_log, dt_bias, d_skip, ssd_norm,
           q_norm, k_norm, rpb, gm_norm, w_spatial, b_spatial, w_branch_ssd, w_branch_na, w_branch_gm, w_out,
           norm2, w_ffn_in, w_ffn_out):
    p = dict(norm1=norm1, w_in=w_in, b_gate=b_gate, conv_w=conv_w, conv_b=conv_b, a_log=a_log, dt_bias=dt_bias,
             d_skip=d_skip, ssd_norm=ssd_norm, q_norm=q_norm, k_norm=k_norm, gm_norm=gm_norm, w_spatial=w_spatial,
             b_spatial=b_spatial, w_branch_ssd=w_branch_ssd, w_branch_na=w_branch_na, w_branch_gm=w_branch_gm,
             w_out=w_out, norm2=norm2, w_ffn_in=w_ffn_in, w_ffn_out=w_ffn_out)
    b, seq, d = x.shape
    n_ctx = ctx.shape[1]
    depth = w_mod.shape[0]

    c_all = jnp.zeros((8, d), F32).at[:b].set(c).at[b].set(c_ctx)
    mod = _modulation(c_all, w_mod, b_mod)
    bias = _bias_table(rpb)
    consts = _shared_constants(n_ctx, seq)

    x_all = jnp.concatenate([ctx, x], axis=1)
    for l in range(depth):
        lw = _layer_weights(l, p)
        mod_tok = jnp.stack([jnp.broadcast_to(mod[l, b], (b, 6 * d)), mod[l, :b]], axis=1).reshape(2 * b, 6, d)
        z, xbc, dt, q, k, v, y_gm, gate = _input_projection(x_all, mod_tok, lw, n_ctx)
        y_ssd = _ssd(xbc, dt, lw, consts, n_ctx)
        y_na = _neighbourhood_attention(q, k, v, bias[l * NA_HEADS:(l + 1) * NA_HEADS], n_ctx)
        x_all = _merge_ffn(x_all, mod_tok, gate, y_ssd, z, y_na, y_gm, lw, n_ctx)
    return x_all[:, n_ctx:]
```

```python
import functools
import math

import jax
import jax.numpy as jnp
from jax import lax
from jax.experimental import pallas as pl
from jax.experimental.pallas import tpu as pltpu

F32 = jnp.float32
BF16 = jnp.bfloat16

EPS = 1e-6
GRID_W = 64

SSD_INNER = 1024
SSD_HEAD_DIM = 64
SSD_HEADS = 16
SSD_GROUPS = 4
SSD_STATE = 128
SSD_CONV = 5
SSD_CHUNK = 128
SSD_BC = SSD_GROUPS * SSD_STATE
SSD_CONV_DIM = SSD_INNER + 2 * SSD_BC
ROPE_FREQS = 32
ROPE_BASE = 10000.0

NA_HEAD_DIM = 64
NA_WIDTH = 512
NA_HEADS = 8
NA_ROWS = 8
NA_COLS = 16

GM_WIDTH = 512
GM_GROUPS = 8
GM_CHUNK = 128

LANES = 128
HALO = 16
NEG = -1e30
VMEM_LIMIT = 52 * 1024 * 1024


def _dot(a, b):
    return jnp.dot(a, b, preferred_element_type=F32)


def _dot_nt(a, b):
    return lax.dot_general(a, b, (((1,), (1,)), ((), ())), preferred_element_type=F32)


def _silu(x):
    return x / (1.0 + jnp.exp(-x))


def _sigmoid(x):
    return 1.0 / (1.0 + jnp.exp(-x))


def _gelu_tanh(x):
    return 0.5 * x * (1.0 + jnp.tanh(math.sqrt(2.0 / math.pi) * (x + 0.044715 * (x * x * x))))


def _softplus(x):
    return jnp.maximum(x, 0.0) + jnp.log(1.0 + jnp.exp(-jnp.abs(x)))


def _split_hi_lo(v):
    hi = v.astype(BF16)
    lo = (v - hi.astype(F32)).astype(BF16)
    return hi, lo


def _const_spec(shape):
    nd = len(shape)
    return pl.BlockSpec(shape, lambda *_: (0,) * nd, pipeline_mode=pl.Buffered(1))


def _mod_kernel(c_ref, w_ref, b_ref, o_ref):
    o_ref[0] = _dot(_silu(c_ref[...]), w_ref[0]) + b_ref[0]


def _modulation(c_all, w_mod, b_mod):
    depth, d, n = w_mod.shape
    tn = 1536
    return pl.pallas_call(
        _mod_kernel,
        out_shape=jax.ShapeDtypeStruct((depth, 8, n), F32),
        grid=(depth, n // tn),
        in_specs=[pl.BlockSpec((8, d), lambda l, j: (0, 0)),
                  pl.BlockSpec((1, d, tn), lambda l, j: (l, 0, j)),
                  pl.BlockSpec((1, 1, tn), lambda l, j: (l, 0, j))],
        out_specs=pl.BlockSpec((1, 8, tn), lambda l, j: (l, 0, j)),
        compiler_params=pltpu.CompilerParams(dimension_semantics=("arbitrary", "arbitrary"),
                                             vmem_limit_bytes=VMEM_LIMIT),
        name="modulation",
    )(c_all, w_mod, b_mod.reshape(depth, 1, n))


def _inproj_kernel(x_ref, mod_ref, n1_ref, wz, wxbc, wdt, wq, wk, wv, wuv, wg, dtb, bg, qn, kn, blk,
                   gmn, ws, bsp, z_o, xbc_o, dt_o, q_o, k_o, v_o, ygm_o, gate_o):
    tm = x_ref.shape[1]
    x = x_ref[0]
    xn = x * lax.rsqrt(jnp.mean(x * x, axis=-1, keepdims=True) + EPS) * n1_ref[...]
    hb = (xn * (1.0 + mod_ref[0, 1:2, :]) + mod_ref[0, 0:1, :]).astype(BF16)

    cw = 512
    for n0 in range(0, z_o.shape[2], cw):
        z_o[0, :, n0:n0 + cw] = _dot(hb, wz[:, n0:n0 + cw]).astype(BF16)
    for n0 in range(0, xbc_o.shape[2], cw):
        xbc_o[0, :, n0:n0 + cw] = _dot(hb, wxbc[:, n0:n0 + cw]).astype(BF16)
    for n0 in range(0, gate_o.shape[2], cw):
        gate_o[0, :, n0:n0 + cw] = _sigmoid(_dot(hb, wg[:, n0:n0 + cw]) + bg[:, n0:n0 + cw]).astype(BF16)

    dt_o[0] = _softplus(_dot(hb, wdt[...]) + dtb[...])

    def head_norm(t, w_row):
        hi, lo = _split_hi_lo(t * t)
        ms = _dot(hi, blk[...]) + _dot(lo, blk[...])
        return t * lax.rsqrt(ms + EPS) * w_row

    q_o[0] = (head_norm(_dot(hb, wq[...]), qn[...]) * (NA_HEAD_DIM ** -0.5)).astype(BF16)
    k_o[0] = head_norm(_dot(hb, wk[...]), kn[...]).astype(BF16)
    v_o[0] = _dot(hb, wv[...]).astype(BF16)

    g = _gelu_tanh(_dot(hb, wuv[...]))
    u = g[:, :GM_WIDTH]
    v = g[:, GM_WIDTH:]
    vb = (v * lax.rsqrt(jnp.mean(v * v, axis=-1, keepdims=True) + EPS) * gmn[...]).astype(BF16)
    lane = lax.broadcasted_iota(jnp.int32, (GM_CHUNK, LANES), 1)
    for c0 in range(0, tm, GM_CHUNK):
        for p in range(GM_WIDTH // LANES):
            cols = slice(p * LANES, (p + 1) * LANES)
            vp = vb[c0:c0 + GM_CHUNK, cols]
            mixed = jnp.where(lane < LANES // 2, _dot(ws[2 * p], vp), _dot(ws[2 * p + 1], vp)) + bsp[:, cols]
            ygm_o[0, c0:c0 + GM_CHUNK, cols] = (u[c0:c0 + GM_CHUNK, cols] * mixed).astype(BF16)


def _input_projection(x_all, mod_tok, lw, n_ctx, tm=256):
    b, s, d = x_all.shape
    nct = n_ctx // tm
    tok = lambda c: pl.BlockSpec((1, tm, c), lambda i, j: (i, j, 0))
    consts = [lw["norm1"], lw["w_z"], lw["w_xbc"], lw["w_dt"], lw["w_q"], lw["w_k"], lw["w_v"], lw["w_uv"],
              lw["w_gate"], lw["dt_bias"], lw["b_gate"], lw["q_norm"], lw["k_norm"], lw["head_blk"],
              lw["gm_norm"], lw["w_spatial"], lw["b_spatial"]]
    widths = [(SSD_INNER, BF16), (SSD_CONV_DIM, BF16), (LANES, F32), (NA_WIDTH, BF16), (NA_WIDTH, BF16),
              (NA_WIDTH, BF16), (GM_WIDTH, BF16), (3 * d, BF16)]
    return pl.pallas_call(
        _inproj_kernel,
        out_shape=[jax.ShapeDtypeStruct((b, s, c), t) for c, t in widths],
        grid=(b, s // tm),
        in_specs=[tok(d),
                  pl.BlockSpec((1, 6, d), lambda i, j: (2 * i + jnp.where(j >= nct, 1, 0), 0, 0))]
                 + [_const_spec(a.shape) for a in consts],
        out_specs=[tok(c) for c, _ in widths],
        compiler_params=pltpu.CompilerParams(dimension_semantics=("parallel", "parallel"),
                                             vmem_limit_bytes=VMEM_LIMIT),
        name="input_projection",
    )(x_all, mod_tok, *consts)


def _ssd_chunk(xs, bm, cm, dt, a_row, e2_ref, tri2_ref, h_ref, reverse, head_off):
    q = xs.shape[0]
    last = 0 if reverse else q - 1
    a = dt * a_row
    a1 = a.astype(BF16)
    r1 = a - a1.astype(F32)
    a2 = r1.astype(BF16)
    a3 = (r1 - a2.astype(F32)).astype(BF16)
    tri2 = tri2_ref[...]
    acum = _dot(tri2, jnp.concatenate([a1, a2], axis=0)) + _dot(tri2[:, :q], a3)
    acum_t = acum.T
    a_last = acum[last:last + 1, :]

    per_head = jnp.concatenate([dt, dt * jnp.exp(a_last - acum), jnp.broadcast_to(jnp.exp(a_last), (8, LANES))],
                               axis=0)
    hi, lo = _split_hi_lo(per_head)
    wide = _dot(jnp.concatenate([hi, lo], axis=1), e2_ref[...])
    xdt_b = (xs * wide[0:q]).astype(BF16)
    xw_b = (xs * wide[q:2 * q]).astype(BF16)
    h_dec = wide[2 * q:2 * q + 1]

    row = lax.broadcasted_iota(jnp.int32, (q, q), 0)
    col = lax.broadcasted_iota(jnp.int32, (q, q), 1)
    in_scan = (col >= row) if reverse else (col <= row)
    lane = lax.broadcasted_iota(jnp.int32, (q, LANES), 1)
    gw = SSD_INNER // SSD_GROUPS
    hpg = gw // SSD_HEAD_DIM
    lane_head = lax.broadcasted_iota(jnp.int32, (q, gw), 1) // SSD_HEAD_DIM

    ys = []
    for g in range(SSD_GROUPS):
        bg = bm[:, g * SSD_STATE:(g + 1) * SSD_STATE]
        cgb = cm[:, g * SSD_STATE:(g + 1) * SSD_STATE].astype(BF16)
        cb = _dot_nt(cgb, bg.astype(BF16))
        h_prev = h_ref[:, g * gw:(g + 1) * gw]
        xg = xdt_b[:, g * gw:(g + 1) * gw]
        ms, decs, rhs = [], [], []
        for j in range(hpg):
            hc = head_off + g * hpg + j
            colx = jnp.broadcast_to(acum[:, hc:hc + 1], (q, q))
            decay = jnp.exp(jnp.where(in_scan, colx - acum_t[hc:hc + 1, :], NEG))
            ms.append((cb * decay).astype(BF16))
            decs.append(jnp.exp(colx))
            rhs.append(jnp.where(lane_head == j, xg, jnp.zeros_like(xg)))
        y_intra = _dot(jnp.concatenate(ms, axis=1), jnp.concatenate(rhs, axis=0))
        dec_out = jnp.concatenate([jnp.where(lane < SSD_HEAD_DIM, decs[2 * i], decs[2 * i + 1])
                                   for i in range(hpg // 2)], axis=1)
        ys.append(y_intra + _dot(cgb, h_prev.astype(BF16)) * dec_out)
        h_ref[:, g * gw:(g + 1) * gw] = (h_prev * h_dec[:, g * gw:(g + 1) * gw]
                                         + _dot(bg.astype(F32).T.astype(BF16), xw_b[:, g * gw:(g + 1) * gw]))
    return jnp.concatenate(ys, axis=1)


def _ssd_fwd_kernel(cur_ref, prev_ref, next_ref, dt_ref, cos_ref, sin_ref, cw_ref, cb_ref, shift_ref, a_ref, e_ref,
                    tri_ref, dsk_ref, xbc_o, y_o, h_ref, *, nc_ctx):
    i = pl.program_id(1)
    nc = pl.num_programs(1)
    q = cur_ref.shape[1]

    @pl.when(i == 0)
    def _():
        h_ref[...] = jnp.zeros_like(h_ref)

    pv = jnp.where((i == 0) | (i == nc_ctx), 0.0, 1.0).astype(BF16)
    nv = jnp.where((i == nc_ctx - 1) | (i == nc - 1), 0.0, 1.0).astype(BF16)
    ext = jnp.concatenate([prev_ref[0] * pv, cur_ref[0], next_ref[0] * nv], axis=0)
    parts = []
    cwid = 256
    side_taps = [k for k in range(SSD_CONV) if k != SSD_CONV // 2]
    for c0 in range(0, SSD_CONV_DIM, cwid):
        shifted = _dot(shift_ref[...], ext[:, c0:c0 + cwid])
        acc = cb_ref[:, c0:c0 + cwid] + (cw_ref[SSD_CONV // 2:SSD_CONV // 2 + 1, c0:c0 + cwid]
                                         * cur_ref[0, :, c0:c0 + cwid].astype(F32))
        for n, k in enumerate(side_taps):
            acc = acc + cw_ref[k:k + 1, c0:c0 + cwid] * shifted[n * q:(n + 1) * q]
        parts.append(_silu(acc))
    xs = jnp.concatenate(parts[:SSD_INNER // cwid], axis=1)

    lane = lax.broadcasted_iota(jnp.int32, (q, LANES), 1)
    cos = cos_ref[...]
    sin = sin_ref[...]

    def rope(t):
        sw = jnp.where((lane & ROPE_FREQS) == 0, pltpu.roll(t, LANES - ROPE_FREQS, 1), pltpu.roll(t, ROPE_FREQS, 1))
        return t * cos + sw * sin

    bc = jnp.concatenate(parts[SSD_INNER // cwid:], axis=1)
    bc = jnp.concatenate([rope(bc[:, g * LANES:(g + 1) * LANES]) for g in range(2 * SSD_GROUPS)], axis=1)
    bm = bc[:, :SSD_BC]
    cm = bc[:, SSD_BC:]

    xbc_o[0, :, :SSD_INNER] = xs.astype(BF16)
    xbc_o[0, :, SSD_INNER:] = bc.astype(BF16)

    y = _ssd_chunk(xs, bm, cm, dt_ref[0], a_ref[...], e_ref, tri_ref, h_ref, False, 0)
    y_o[0] = (y + dsk_ref[...] * xs).astype(BF16)


def _ssd_bwd_kernel(xbc_ref, dt_ref, yf_ref, a_ref, e_ref, tri_ref, y_o, h_ref):
    @pl.when(pl.program_id(1) == 0)
    def _():
        h_ref[...] = jnp.zeros_like(h_ref)

    y = _ssd_chunk(xbc_ref[0, :, :SSD_INNER].astype(F32), xbc_ref[0, :, SSD_INNER:SSD_INNER + SSD_BC],
                   xbc_ref[0, :, SSD_INNER + SSD_BC:], dt_ref[0], a_ref[...], e_ref, tri_ref, h_ref, True, SSD_HEADS)
    y_o[0] = (y + yf_ref[0].astype(F32)).astype(BF16)


def _ssd(xbc, dt, lw, consts, n_ctx):
    b, s, _ = xbc.shape
    q = SSD_CHUNK
    nc = s // q
    nc_ctx = n_ctx // q
    hb = q // HALO
    n_hb = s // HALO
    chunk = lambda c: pl.BlockSpec((1, q, c), lambda i, j: (i, j, 0))
    scratch_h = pltpu.VMEM((SSD_STATE, SSD_INNER), F32)
    params = pltpu.CompilerParams(dimension_semantics=("parallel", "arbitrary"), vmem_limit_bytes=VMEM_LIMIT)

    xbc_c, y_f = pl.pallas_call(
        functools.partial(_ssd_fwd_kernel, nc_ctx=nc_ctx),
        out_shape=[jax.ShapeDtypeStruct((b, s, SSD_CONV_DIM), BF16), jax.ShapeDtypeStruct((b, s, SSD_INNER), BF16)],
        grid=(b, nc),
        in_specs=[chunk(SSD_CONV_DIM),
                  pl.BlockSpec((1, HALO, SSD_CONV_DIM), lambda i, j: (i, jnp.maximum(j * hb - 1, 0), 0)),
                  pl.BlockSpec((1, HALO, SSD_CONV_DIM), lambda i, j: (i, jnp.minimum((j + 1) * hb, n_hb - 1), 0)),
                  chunk(LANES),
                  pl.BlockSpec((q, LANES), lambda i, j: (j, 0)),
                  pl.BlockSpec((q, LANES), lambda i, j: (j, 0)),
                  _const_spec(lw["conv_w"].shape), _const_spec(lw["conv_b"].shape),
                  _const_spec(consts["conv_shift"].shape),
                  _const_spec(lw["a_fwd"].shape), _const_spec(consts["e_fwd"].shape),
                  _const_spec(consts["tri_fwd"].shape), _const_spec(lw["d_skip"].shape)],
        out_specs=[chunk(SSD_CONV_DIM), chunk(SSD_INNER)],
        scratch_shapes=[scratch_h],
        compiler_params=params,
        name="ssd_forward",
    )(xbc, xbc, xbc, dt, consts["rope_cos"], consts["rope_sin"], lw["conv_w"], lw["conv_b"], consts["conv_shift"],
      lw["a_fwd"], consts["e_fwd"], consts["tri_fwd"], lw["d_skip"])

    def rev(j):
        return jnp.where(j < nc_ctx, nc_ctx - 1 - j, nc + nc_ctx - 1 - j)

    rchunk = lambda c: pl.BlockSpec((1, q, c), lambda i, j: (i, rev(j), 0))
    return pl.pallas_call(
        _ssd_bwd_kernel,
        out_shape=jax.ShapeDtypeStruct((b, s, SSD_INNER), BF16),
        grid=(b, nc),
        in_specs=[rchunk(SSD_CONV_DIM), rchunk(LANES), rchunk(SSD_INNER),
                  _const_spec(lw["a_bwd"].shape), _const_spec(consts["e_bwd"].shape),
                  _const_spec(consts["tri_bwd"].shape)],
        out_specs=rchunk(SSD_INNER),
        scratch_shapes=[scratch_h],
        compiler_params=params,
        name="ssd_backward",
    )(xbc_c, dt, y_f, lw["a_bwd"], consts["e_bwd"], consts["tri_bwd"])


def _bias_kernel(rpb_ref, o_ref):
    lh = pl.program_id(0)
    e = pl.program_id(1)
    n_ri = 2 * NA_ROWS - 1
    n_ci = 2 * NA_COLS - 1
    lane = lax.broadcasted_iota(jnp.int32, (GRID_W, LANES), 1)
    qc = lax.broadcasted_iota(jnp.int32, (GRID_W, LANES), 0)
    kc = lane % GRID_W
    half = lane // GRID_W
    cs = jnp.clip(qc - NA_COLS // 2, 0, GRID_W - NA_COLS)
    col_ok = (kc >= cs) & (kc < cs + NA_COLS)
    ci = jnp.clip(kc - qc + (NA_COLS - 1), 0, n_ci - 1)
    acc = jnp.full((GRID_W, LANES), NEG, F32)
    for hf in range(2):
        ri = e - 1 + hf
        ok = col_ok & (half == hf) & (ri >= 0) & (ri < n_ri)
        base = (lh * n_ri + jnp.clip(ri, 0, n_ri - 1)) * n_ci
        for c in range(n_ci):
            acc = jnp.where(ok & (ci == c), rpb_ref[base + c], acc)
    o_ref[0, 0] = acc


def _bias_table(rpb):
    depth, heads, n_ri, n_ci = rpb.shape
    return pl.pallas_call(
        _bias_kernel,
        out_shape=jax.ShapeDtypeStruct((depth * heads, n_ri + 1, GRID_W, LANES), F32),
        grid=(depth * heads, n_ri + 1),
        in_specs=[pl.BlockSpec(memory_space=pltpu.SMEM)],
        out_specs=pl.BlockSpec((1, 1, GRID_W, LANES), lambda i, j: (i, j, 0, 0)),
        compiler_params=pltpu.CompilerParams(dimension_semantics=("arbitrary", "arbitrary")),
        name="na_bias_table",
    )(rpb.reshape(-1))


def _na_kernel(q_ref, k_ref, v_ref, bias_ref, o_ref, *, n_ctx, rt, nw, rows_n):
    t = pl.program_id(1)
    tq = rt * GRID_W
    nct = n_ctx // tq
    is_lat = t >= nct
    r0 = (t - nct) * rt
    base = jnp.clip(r0 - NA_ROWS // 2, 0, rows_n - nw)
    start = pl.multiple_of(n_ctx + base * GRID_W, GRID_W)
    lane_q = lax.broadcasted_iota(jnp.int32, (tq, LANES), 1)
    lane_r = lax.broadcasted_iota(jnp.int32, (GRID_W, LANES), 1)
    nblk = nw // 2

    ents, valid = [], []
    for qi in range(rt):
        r = r0 + qi
        rs = jnp.clip(r - NA_ROWS // 2, 0, rows_n - NA_ROWS)
        e_row, v_row = [], []
        for m in range(nblk):
            j0 = base + 2 * m
            e_row.append(jnp.clip(j0 - r + NA_ROWS, 0, 2 * NA_ROWS - 1))
            ok0 = is_lat & (j0 >= rs) & (j0 < rs + NA_ROWS)
            ok1 = is_lat & (j0 + 1 >= rs) & (j0 + 1 < rs + NA_ROWS)
            v_row.append(jnp.where(lane_r < GRID_W, ok0.astype(jnp.int32), ok1.astype(jnp.int32)) > 0)
        ents.append(e_row)
        valid.append(v_row)

    n_cb = n_ctx // LANES
    lane_c = lax.broadcasted_iota(jnp.int32, (n_ctx, LANES), 1)
    lane_w = lax.broadcasted_iota(jnp.int32, (nw * GRID_W, LANES), 1)
    for p in range(NA_WIDTH // LANES):
        cols = slice(p * LANES, (p + 1) * LANES)
        qp = q_ref[0, :, cols]
        kc = k_ref[0, 0:n_ctx, cols]
        vc = v_ref[0, 0:n_ctx, cols]
        kw = k_ref[0, pl.ds(start, nw * GRID_W), cols]
        vw = v_ref[0, pl.ds(start, nw * GRID_W), cols]
        nums = []
        for hh in range(2):
            h = 2 * p + hh
            own = lambda lane: (lane < NA_HEAD_DIM) == (hh == 0)
            qm = jnp.where(own(lane_q), qp, jnp.zeros_like(qp))
            vce = jnp.where(own(lane_c), vc, jnp.ones_like(vc))
            vwe = jnp.where(own(lane_w), vw, jnp.ones_like(vw))
            s_ctx = _dot_nt(qm, kc)
            s_win = _dot_nt(qm, kw)
            p_rows = []
            for qi in range(rt):
                rows = slice(qi * GRID_W, (qi + 1) * GRID_W)
                blocks = [s_ctx[rows, m * LANES:(m + 1) * LANES] for m in range(n_cb)]
                for m in range(nblk):
                    sb = s_win[rows, m * LANES:(m + 1) * LANES] + bias_ref[h, ents[qi][m]]
                    blocks.append(jnp.where(valid[qi][m], sb, NEG))
                mx = blocks[0]
                for sb in blocks[1:]:
                    mx = jnp.maximum(mx, sb)
                mx = jnp.max(mx, axis=-1, keepdims=True)
                p_rows.append(jnp.concatenate([jnp.exp(sb - mx).astype(BF16) for sb in blocks], axis=1))
            pm = jnp.concatenate(p_rows, axis=0)
            nums.append(_dot(pm[:, :n_ctx], vce) + _dot(pm[:, n_ctx:], vwe))
        num = jnp.where(lane_q < NA_HEAD_DIM, nums[0], nums[1])
        den = pltpu.roll(jnp.where(lane_q < NA_HEAD_DIM, nums[1], nums[0]), NA_HEAD_DIM, 1)
        o_ref[0, :, cols] = (num / den).astype(BF16)


def _neighbourhood_attention(q, k, v, bias, n_ctx, rt=4):
    b, s, w = q.shape
    rows_n = (s - n_ctx) // GRID_W
    nw = rt + NA_ROWS
    nw += nw % 2
    tq = rt * GRID_W
    whole = pl.BlockSpec((1, s, w), lambda i, j: (i, 0, 0))
    return pl.pallas_call(
        functools.partial(_na_kernel, n_ctx=n_ctx, rt=rt, nw=nw, rows_n=rows_n),
        out_shape=jax.ShapeDtypeStruct((b, s, w), BF16),
        grid=(b, s // tq),
        in_specs=[pl.BlockSpec((1, tq, w), lambda i, j: (i, j, 0)), whole, whole, _const_spec(bias.shape)],
        out_specs=pl.BlockSpec((1, tq, w), lambda i, j: (i, j, 0)),
        compiler_params=pltpu.CompilerParams(dimension_semantics=("parallel", "arbitrary"),
                                             vmem_limit_bytes=VMEM_LIMIT),
        name="neighbourhood_attention",
    )(q, k, v, bias)


def _merge_ffn_kernel(x_ref, mod_ref, gate_ref, yssd_ref, z_ref, yna_ref, ygm_ref, sn_ref, wa, wb, wc, wo,
                      n2_ref, wfi, wfo, o_ref, *, ffn_chunk):
    d = x_ref.shape[2]
    y = yssd_ref[0].astype(F32) * _silu(z_ref[0].astype(F32))
    y = (y * lax.rsqrt(jnp.mean(y * y, axis=-1, keepdims=True) + EPS) * sn_ref[...]).astype(BF16)
    gate = gate_ref[0]
    mixed = (gate[:, 0:d].astype(F32) * _dot(y, wa[...])
             + gate[:, d:2 * d].astype(F32) * _dot(yna_ref[0], wb[...])
             + gate[:, 2 * d:3 * d].astype(F32) * _dot(ygm_ref[0], wc[...]))
    x1 = x_ref[0] + mod_ref[0, 2:3, :] * _dot(mixed.astype(BF16), wo[...])
    xn = x1 * lax.rsqrt(jnp.mean(x1 * x1, axis=-1, keepdims=True) + EPS) * n2_ref[...]
    hb = (xn * (1.0 + mod_ref[0, 4:5, :]) + mod_ref[0, 3:4, :]).astype(BF16)
    hid = wfo.shape[0]
    acc = jnp.zeros_like(x1)
    for c0 in range(0, hid, ffn_chunk):
        a = _dot(hb, wfi[:, c0:c0 + ffn_chunk])
        g = _dot(hb, wfi[:, hid + c0:hid + c0 + ffn_chunk])
        acc = acc + _dot((_silu(a) * g).astype(BF16), wfo[c0:c0 + ffn_chunk, :])
    o_ref[0] = x1 + mod_ref[0, 5:6, :] * acc


def _merge_ffn(x_all, mod_tok, gate, y_ssd, z, y_na, y_gm, lw, n_ctx, tm=256):
    b, s, d = x_all.shape
    nct = n_ctx // tm
    tok = lambda c: pl.BlockSpec((1, tm, c), lambda i, j: (i, j, 0))
    consts = [lw["ssd_norm"], lw["w_branch_ssd"], lw["w_branch_na"], lw["w_branch_gm"], lw["w_out"], lw["norm2"],
              lw["w_ffn_in"], lw["w_ffn_out"]]
    return pl.pallas_call(
        functools.partial(_merge_ffn_kernel, ffn_chunk=256),
        out_shape=jax.ShapeDtypeStruct((b, s, d), F32),
        grid=(b, s // tm),
        in_specs=[tok(d),
                  pl.BlockSpec((1, 6, d), lambda i, j: (2 * i + jnp.where(j >= nct, 1, 0), 0, 0)),
                  tok(3 * d), tok(SSD_INNER), tok(SSD_INNER), tok(NA_WIDTH), tok(GM_WIDTH)]
                 + [_const_spec(a.shape) for a in consts],
        out_specs=tok(d),
        compiler_params=pltpu.CompilerParams(dimension_semantics=("parallel", "parallel"),
                                             vmem_limit_bytes=VMEM_LIMIT),
        name="merge_out_ffn",
    )(x_all, mod_tok, gate, y_ssd, z, y_na, y_gm, *consts)


def _shared_constants(n_ctx, seq):
    pos = jnp.arange(seq)
    freqs = ROPE_BASE ** (-jnp.arange(ROPE_FREQS, dtype=F32) / ROPE_FREQS)
    ang_row = (pos // GRID_W).astype(F32)[:, None] * freqs
    ang_col = (pos % GRID_W).astype(F32)[:, None] * freqs

    def table(fr, fc, ctx_val):
        lat = jnp.concatenate([fr(ang_row)[0], fr(ang_row)[1], fc(ang_col)[0], fc(ang_col)[1]], axis=1)
        return jnp.concatenate([jnp.full((n_ctx, LANES), ctx_val, F32), lat], axis=0)

    cos = table(lambda a: (jnp.cos(a), jnp.cos(a)), lambda a: (jnp.cos(a), jnp.cos(a)), 1.0)
    sin = table(lambda a: (-jnp.sin(a), jnp.sin(a)), lambda a: (-jnp.sin(a), jnp.sin(a)), 0.0)

    r = jnp.arange(SSD_CHUNK)
    twice = lambda m, axis: jnp.concatenate([m, m], axis=axis).astype(BF16)
    tri_fwd = twice(r[None, :] <= r[:, None], 1)
    tri_bwd = twice(r[None, :] >= r[:, None], 1)
    lane_head = jnp.arange(SSD_INNER) // SSD_HEAD_DIM
    rows = jnp.arange(LANES)
    e_fwd = twice(rows[:, None] == lane_head[None, :], 0)
    e_bwd = twice(rows[:, None] == lane_head[None, :] + SSD_HEADS, 0)
    src = jnp.arange(SSD_CHUNK + 2 * HALO)
    offs = [k - SSD_CONV // 2 for k in range(SSD_CONV) if k != SSD_CONV // 2]
    conv_shift = jnp.concatenate([(src[None, :] == r[:, None] + HALO + o) for o in offs], axis=0).astype(BF16)
    return dict(rope_cos=cos, rope_sin=sin, tri_fwd=tri_fwd, tri_bwd=tri_bwd, e_fwd=e_fwd, e_bwd=e_bwd,
                conv_shift=conv_shift)


def _layer_weights(l, p):
    d = p["w_in"].shape[1]
    sizes = (SSD_INNER, SSD_CONV_DIM, 2 * SSD_HEADS, NA_WIDTH, NA_WIDTH, NA_WIDTH, 2 * GM_WIDTH, 3 * d)
    names = ("w_z", "w_xbc", "w_dt", "w_q", "w_k", "w_v", "w_uv", "w_gate")
    lw, start = {}, 0
    w_in = p["w_in"][l]
    for name, size in zip(names, sizes):
        lw[name] = w_in[:, start:start + size].astype(BF16)
        start += size
    pad_lanes = lambda v: jnp.pad(v, (0, LANES - v.shape[0])).reshape(1, LANES)
    lw["w_dt"] = jnp.pad(lw["w_dt"], ((0, 0), (0, LANES - 2 * SSD_HEADS)))
    lw["dt_bias"] = pad_lanes(p["dt_bias"][l].reshape(-1))
    a = -jnp.exp(p["a_log"][l].astype(F32))
    lw["a_fwd"] = pad_lanes(a[0])
    lw["a_bwd"] = pad_lanes(jnp.concatenate([jnp.zeros((SSD_HEADS,), F32), a[1]]))
    row = lambda v: v.reshape(1, -1)
    lw["norm1"] = row(p["norm1"][l])
    lw["norm2"] = row(p["norm2"][l])
    lw["b_gate"] = row(p["b_gate"][l])
    lw["q_norm"] = row(jnp.tile(p["q_norm"][l], NA_HEADS))
    lw["k_norm"] = row(jnp.tile(p["k_norm"][l], NA_HEADS))
    head = jnp.arange(NA_WIDTH) // NA_HEAD_DIM
    lw["head_blk"] = ((head[:, None] == head[None, :]).astype(F32) / NA_HEAD_DIM).astype(BF16)
    lw["gm_norm"] = row(p["gm_norm"][l])
    lw["w_spatial"] = p["w_spatial"][l].astype(BF16)
    lw["b_spatial"] = jnp.repeat(p["b_spatial"][l].T, GM_WIDTH // GM_GROUPS, axis=1)
    lw["conv_w"] = jnp.pad(p["conv_w"][l], ((0, 8 - SSD_CONV), (0, 0)))
    lw["conv_b"] = row(p["conv_b"][l])
    lw["d_skip"] = row(jnp.repeat(p["d_skip"][l], SSD_HEAD_DIM))
    lw["ssd_norm"] = row(p["ssd_norm"][l])
    for name in ("w_branch_ssd", "w_branch_na", "w_branch_gm", "w_out", "w_ffn_in", "w_ffn_out"):
        lw[name] = p[name][l].astype(BF16)
    return lw


def kernel(x, c, ctx, c_ctx, w_mod, b_mod, norm1, w_in, b_gate, conv_w, conv_b, a_log, dt_bias, d_skip, ssd_norm,
           q_norm, k_norm, rpb, gm_norm, w_spatial, b_spatial, w_branch_ssd, w_branch_na, w_branch_gm, w_out,
           norm2, w_ffn_in, w_ffn_out):
    p = dict(norm1=norm1, w_in=w_in, b_gate=b_gate, conv_w=conv_w, conv_b=conv_b, a_log=a_log, dt_bias=dt_bias,
             d_skip=d_skip, ssd_norm=ssd_norm, q_norm=q_norm, k_norm=k_norm, gm_norm=gm_norm, w_spatial=w_spatial,
             b_spatial=b_spatial, w_branch_ssd=w_branch_ssd, w_branch_na=w_branch_na, w_branch_gm=w_branch_gm,
             w_out=w_out, norm2=norm2, w_ffn_in=w_ffn_in, w_ffn_out=w_ffn_out)
    b, seq, d = x.shape
    n_ctx = ctx.shape[1]
    depth = w_mod.shape[0]

    c_all = jnp.zeros((8, d), F32).at[:b].set(c).at[b].set(c_ctx)
    mod = _modulation(c_all, w_mod, b_mod)
    bias = _bias_table(rpb)
    consts = _shared_constants(n_ctx, seq)

    x_all = jnp.concatenate([ctx, x], axis=1)
    for l in range(depth):
        lw = _layer_weights(l, p)
        mod_tok = jnp.stack([jnp.broadcast_to(mod[l, b], (b, 6 * d)), mod[l, :b]], axis=1).reshape(2 * b, 6, d)
        z, xbc, dt, q, k, v, y_gm, gate = _input_projection(x_all, mod_tok, lw, n_ctx)
        y_ssd = _ssd(xbc, dt, lw, consts, n_ctx)
        y_na = _neighbourhood_attention(q, k, v, bias[l * NA_HEADS:(l + 1) * NA_HEADS], n_ctx)
        x_all = _merge_ffn(x_all, mod_tok, gate, y_ssd, z, y_na, y_gm, lw, n_ctx)
    return x_all[:, n_ctx:]
```

```python
import functools
import math

import jax
import jax.numpy as jnp
from jax import lax
from jax.experimental import pallas as pl
from jax.experimental.pallas import tpu as pltpu

F32 = jnp.float32
BF16 = jnp.bfloat16

EPS = 1e-6
GRID_W = 64

SSD_INNER = 1024
SSD_HEAD_DIM = 64
SSD_HEADS = 16
SSD_GROUPS = 4
SSD_STATE = 128
SSD_CONV = 5
SSD_CHUNK = 128
SSD_BC = SSD_GROUPS * SSD_STATE
SSD_CONV_DIM = SSD_INNER + 2 * SSD_BC
ROPE_FREQS = 32
ROPE_BASE = 10000.0

NA_HEAD_DIM = 64
NA_WIDTH = 512
NA_HEADS = 8
NA_ROWS = 8
NA_COLS = 16
NA_QROWS = 4

GM_WIDTH = 512
GM_GROUPS = 8
GM_CHUNK = 128

LANES = 128
HALO = 16
NEG = -1e30
LOG2E = math.log2(math.e)
VMEM_LIMIT = 56 * 1024 * 1024
TOKEN_TILE = 512
SSD_SEQS_PER_STEP = 2


def _dot(a, b):
    return jnp.dot(a, b, preferred_element_type=F32)


def _dot_nt(a, b):
    return lax.dot_general(a, b, (((1,), (1,)), ((), ())), preferred_element_type=F32)


def _silu(x):
    return x / (1.0 + jnp.exp(-x))


def _sigmoid(x):
    return 1.0 / (1.0 + jnp.exp(-x))


def _gelu_tanh(x):
    return 0.5 * x * (1.0 + jnp.tanh(math.sqrt(2.0 / math.pi) * (x + 0.044715 * (x * x * x))))


def _softplus(x):
    return jnp.maximum(x, 0.0) + jnp.log(1.0 + jnp.exp(-jnp.abs(x)))


def _split_hi_lo(v):
    hi = v.astype(BF16)
    lo = (v - hi.astype(F32)).astype(BF16)
    return hi, lo


def _const_spec(shape):
    nd = len(shape)
    return pl.BlockSpec(shape, lambda *_: (0,) * nd, pipeline_mode=pl.Buffered(1))


def _params(*semantics):
    return pltpu.CompilerParams(dimension_semantics=semantics, vmem_limit_bytes=VMEM_LIMIT)


def _mod_kernel(c_ref, w_ref, b_ref, o_ref):
    o_ref[0] = _dot(_silu(c_ref[...]), w_ref[0]) + b_ref[0]


def _modulation(c_all, w_mod, b_mod):
    depth, d, n = w_mod.shape
    tn = 1536
    return pl.pallas_call(
        _mod_kernel,
        out_shape=jax.ShapeDtypeStruct((depth, 8, n), F32),
        grid=(depth, n // tn),
        in_specs=[pl.BlockSpec((8, d), lambda l, j: (0, 0)),
                  pl.BlockSpec((1, d, tn), lambda l, j: (l, 0, j)),
                  pl.BlockSpec((1, 1, tn), lambda l, j: (l, 0, j))],
        out_specs=pl.BlockSpec((1, 8, tn), lambda l, j: (l, 0, j)),
        compiler_params=_params("arbitrary", "arbitrary"),
        name="modulation",
    )(c_all, w_mod, b_mod.reshape(depth, 1, n))


def _inproj_kernel(x_ref, mod_ref, n1_ref, wz, wxbc, wdt, wq, wk, wv, wuv, wg, dtb, bg, qn, kn, blk,
                   gmn, ws, bsp, z_o, xbc_o, dt_o, q_o, k_o, v_o, ygm_o, gate_o):
    tm = x_ref.shape[1]
    x = x_ref[0]
    xn = x * lax.rsqrt(jnp.mean(x * x, axis=-1, keepdims=True) + EPS) * n1_ref[...]
    hb = (xn * (1.0 + mod_ref[0, 1:2, :]) + mod_ref[0, 0:1, :]).astype(BF16)

    cw = 512
    for n0 in range(0, z_o.shape[2], cw):
        z_o[0, :, n0:n0 + cw] = _dot(hb, wz[:, n0:n0 + cw]).astype(BF16)
    for n0 in range(0, xbc_o.shape[2], cw):
        xbc_o[0, :, n0:n0 + cw] = _dot(hb, wxbc[:, n0:n0 + cw]).astype(BF16)
    for n0 in range(0, gate_o.shape[2], cw):
        gate_o[0, :, n0:n0 + cw] = _sigmoid(_dot(hb, wg[:, n0:n0 + cw]) + bg[:, n0:n0 + cw]).astype(BF16)

    dt_o[0] = _softplus(_dot(hb, wdt[...]) + dtb[...])

    def head_norm(t, w_row):
        ms = _dot((t * t).astype(BF16), blk[...])
        return t * lax.rsqrt(ms + EPS) * w_row

    q_o[0] = (head_norm(_dot(hb, wq[...]), qn[...]) * (NA_HEAD_DIM ** -0.5 * LOG2E)).astype(BF16)
    k_o[0] = head_norm(_dot(hb, wk[...]), kn[...]).astype(BF16)
    v_o[0] = _dot(hb, wv[...]).astype(BF16)

    g = _gelu_tanh(_dot(hb, wuv[...]))
    u = g[:, :GM_WIDTH]
    v = g[:, GM_WIDTH:]
    vb = (v * lax.rsqrt(jnp.mean(v * v, axis=-1, keepdims=True) + EPS) * gmn[...]).astype(BF16)
    first = lax.broadcasted_iota(jnp.int32, (GM_CHUNK, LANES), 1) < LANES // 2
    for c0 in range(0, tm, GM_CHUNK):
        for p in range(GM_WIDTH // LANES):
            cols = slice(p * LANES, (p + 1) * LANES)
            vp = vb[c0:c0 + GM_CHUNK, cols]
            zero = jnp.zeros_like(vp)
            stacked = jnp.concatenate([jnp.where(first, vp, zero), jnp.where(first, zero, vp)], axis=0)
            mixed = _dot(ws[p], stacked) + bsp[:, cols]
            ygm_o[0, c0:c0 + GM_CHUNK, cols] = (u[c0:c0 + GM_CHUNK, cols] * mixed).astype(BF16)


def _input_projection(x, mod6, lw):
    b, t, d = x.shape
    tm = min(TOKEN_TILE, t)
    tok = lambda c: pl.BlockSpec((1, tm, c), lambda i, j: (i, j, 0))
    consts = [lw["norm1"], lw["w_z"], lw["w_xbc"], lw["w_dt"], lw["w_q"], lw["w_k"], lw["w_v"], lw["w_uv"],
              lw["w_gate"], lw["dt_bias"], lw["b_gate"], lw["q_norm"], lw["k_norm"], lw["head_blk"],
              lw["gm_norm"], lw["w_spatial"], lw["b_spatial"]]
    widths = [(SSD_INNER, BF16), (SSD_CONV_DIM, BF16), (LANES, F32), (NA_WIDTH, BF16), (NA_WIDTH, BF16),
              (NA_WIDTH, BF16), (GM_WIDTH, BF16), (3 * d, BF16)]
    return pl.pallas_call(
        _inproj_kernel,
        out_shape=[jax.ShapeDtypeStruct((b, t, c), ty) for c, ty in widths],
        grid=(b, t // tm),
        in_specs=[tok(d), pl.BlockSpec((1, 6, d), lambda i, j: (i, 0, 0))] + [_const_spec(a.shape) for a in consts],
        out_specs=[tok(c) for c, _ in widths],
        compiler_params=_params("parallel", "parallel"),
        name="input_projection",
    )(x, mod6, *consts)


def _ssd_chunk(xs, bm, cm, dt, a_row, e2_ref, tri2_ref, h_ref, reverse, head_off):
    q = xs.shape[0]
    last = 0 if reverse else q - 1
    a = dt * a_row
    a1 = a.astype(BF16)
    r1 = a - a1.astype(F32)
    a2 = r1.astype(BF16)
    a3 = (r1 - a2.astype(F32)).astype(BF16)
    tri2 = tri2_ref[...]
    acum = _dot(tri2, jnp.concatenate([a1, a2], axis=0)) + _dot(tri2[:, :q], a3)
    acum_t = acum.T
    a_last = acum[last:last + 1, :]

    per_head = jnp.concatenate([dt, dt * jnp.exp(a_last - acum), jnp.broadcast_to(jnp.exp(a_last), (8, LANES))],
                               axis=0)
    hi, lo = _split_hi_lo(per_head)
    wide = _dot(jnp.concatenate([hi, lo], axis=1), e2_ref[...])
    xdt_b = (xs * wide[0:q]).astype(BF16)
    xw_b = (xs * wide[q:2 * q]).astype(BF16)
    h_dec = wide[2 * q:2 * q + 1]

    row = lax.broadcasted_iota(jnp.int32, (q, q), 0)
    col = lax.broadcasted_iota(jnp.int32, (q, q), 1)
    in_scan = (col >= row) if reverse else (col <= row)
    lane = lax.broadcasted_iota(jnp.int32, (q, LANES), 1)
    gw = SSD_INNER // SSD_GROUPS
    hpg = gw // SSD_HEAD_DIM
    lane_head = lax.broadcasted_iota(jnp.int32, (q, gw), 1) // SSD_HEAD_DIM

    ys = []
    for g in range(SSD_GROUPS):
        bg = bm[:, g * SSD_STATE:(g + 1) * SSD_STATE]
        cgb = cm[:, g * SSD_STATE:(g + 1) * SSD_STATE].astype(BF16)
        cb = _dot_nt(cgb, bg.astype(BF16))
        h_prev = h_ref[:, g * gw:(g + 1) * gw]
        xg = xdt_b[:, g * gw:(g + 1) * gw]
        ms, decs, rhs = [], [], []
        for j in range(hpg):
            hc = head_off + g * hpg + j
            colx = jnp.broadcast_to(acum[:, hc:hc + 1], (q, q))
            decay = jnp.exp(jnp.where(in_scan, colx - acum_t[hc:hc + 1, :], NEG))
            ms.append((cb * decay).astype(BF16))
            decs.append(jnp.exp(colx))
            rhs.append(jnp.where(lane_head == j, xg, jnp.zeros_like(xg)))
        y_intra = _dot(jnp.concatenate(ms, axis=1), jnp.concatenate(rhs, axis=0))
        dec_out = jnp.concatenate([jnp.where(lane < SSD_HEAD_DIM, decs[2 * i], decs[2 * i + 1])
                                   for i in range(hpg // 2)], axis=1)
        ys.append(y_intra + _dot(cgb, h_prev.astype(BF16)) * dec_out)
        h_ref[:, g * gw:(g + 1) * gw] = (h_prev * h_dec[:, g * gw:(g + 1) * gw]
                                         + _dot(bg.astype(F32).T.astype(BF16), xw_b[:, g * gw:(g + 1) * gw]))
    return jnp.concatenate(ys, axis=1)


def _ssd_fwd_kernel(cur_ref, prev_ref, next_ref, dt_ref, cos_ref, sin_ref, cw_ref, cb_ref, shift_ref, a_ref, e_ref,
                    tri_ref, dsk_ref, h0_ref, xbc_o, y_o, hn_o, h_ref):
    i = pl.program_id(1)
    nc = pl.num_programs(1)
    nb, q, _ = cur_ref.shape

    @pl.when(i == 0)
    def _():
        h_ref[...] = h0_ref[...]

    pv = jnp.where(i == 0, 0.0, 1.0).astype(BF16)
    nv = jnp.where(i == nc - 1, 0.0, 1.0).astype(BF16)
    lane = lax.broadcasted_iota(jnp.int32, (q, LANES), 1)
    cos = cos_ref[...]
    sin = sin_ref[...]

    def rope(t):
        sw = jnp.where((lane & ROPE_FREQS) == 0, pltpu.roll(t, LANES - ROPE_FREQS, 1), pltpu.roll(t, ROPE_FREQS, 1))
        return t * cos + sw * sin

    cwid = 256
    side_taps = [k for k in range(SSD_CONV) if k != SSD_CONV // 2]
    for s in range(nb):
        ext = jnp.concatenate([prev_ref[s] * pv, cur_ref[s], next_ref[s] * nv], axis=0)
        parts = []
        for c0 in range(0, SSD_CONV_DIM, cwid):
            shifted = _dot(shift_ref[...], ext[:, c0:c0 + cwid])
            acc = cb_ref[:, c0:c0 + cwid] + (cw_ref[SSD_CONV // 2:SSD_CONV // 2 + 1, c0:c0 + cwid]
                                             * cur_ref[s, :, c0:c0 + cwid].astype(F32))
            for n, k in enumerate(side_taps):
                acc = acc + cw_ref[k:k + 1, c0:c0 + cwid] * shifted[n * q:(n + 1) * q]
            parts.append(_silu(acc))
        xs = jnp.concatenate(parts[:SSD_INNER // cwid], axis=1)
        bc = jnp.concatenate(parts[SSD_INNER // cwid:], axis=1)
        bc = jnp.concatenate([rope(bc[:, g * LANES:(g + 1) * LANES]) for g in range(2 * SSD_GROUPS)], axis=1)
        xbc_o[s, :, :SSD_INNER] = xs.astype(BF16)
        xbc_o[s, :, SSD_INNER:] = bc.astype(BF16)
        y = _ssd_chunk(xs, bc[:, :SSD_BC], bc[:, SSD_BC:], dt_ref[s], a_ref[...], e_ref, tri_ref, h_ref.at[s], False, 0)
        y_o[s] = (y + dsk_ref[...] * xs).astype(BF16)

    @pl.when(i == nc - 1)
    def _():
        hn_o[...] = h_ref[...]


def _ssd_bwd_kernel(xbc_ref, dt_ref, yf_ref, a_ref, e_ref, tri_ref, h0_ref, y_o, hn_o, h_ref):
    i = pl.program_id(1)

    @pl.when(i == 0)
    def _():
        h_ref[...] = h0_ref[...]

    for s in range(xbc_ref.shape[0]):
        y = _ssd_chunk(xbc_ref[s, :, :SSD_INNER].astype(F32), xbc_ref[s, :, SSD_INNER:SSD_INNER + SSD_BC],
                       xbc_ref[s, :, SSD_INNER + SSD_BC:], dt_ref[s], a_ref[...], e_ref, tri_ref, h_ref.at[s], True,
                       SSD_HEADS)
        y_o[s] = (y + yf_ref[s].astype(F32)).astype(BF16)

    @pl.when(i == pl.num_programs(1) - 1)
    def _():
        hn_o[...] = h_ref[...]


def _ssd(xbc, dt, h0_f, h0_b, rope_cos, rope_sin, lw, consts):
    b, t, _ = xbc.shape
    q = SSD_CHUNK
    nc = t // q
    nb = SSD_SEQS_PER_STEP if b % SSD_SEQS_PER_STEP == 0 else 1
    hb = q // HALO
    n_hb = t // HALO
    chunk = lambda c: pl.BlockSpec((nb, q, c), lambda i, j: (i, j, 0))
    state = pl.BlockSpec((nb, SSD_STATE, SSD_INNER), lambda i, j: (i, 0, 0))
    state_shape = jax.ShapeDtypeStruct((b, SSD_STATE, SSD_INNER), F32)
    scratch_h = pltpu.VMEM((nb, SSD_STATE, SSD_INNER), F32)

    xbc_c, y_f, hn_f = pl.pallas_call(
        _ssd_fwd_kernel,
        out_shape=[jax.ShapeDtypeStruct((b, t, SSD_CONV_DIM), BF16), jax.ShapeDtypeStruct((b, t, SSD_INNER), BF16),
                   state_shape],
        grid=(b // nb, nc),
        in_specs=[chunk(SSD_CONV_DIM),
                  pl.BlockSpec((nb, HALO, SSD_CONV_DIM), lambda i, j: (i, jnp.maximum(j * hb - 1, 0), 0)),
                  pl.BlockSpec((nb, HALO, SSD_CONV_DIM), lambda i, j: (i, jnp.minimum((j + 1) * hb, n_hb - 1), 0)),
                  chunk(LANES),
                  pl.BlockSpec((q, LANES), lambda i, j: (j, 0)),
                  pl.BlockSpec((q, LANES), lambda i, j: (j, 0)),
                  _const_spec(lw["conv_w"].shape), _const_spec(lw["conv_b"].shape),
                  _const_spec(consts["conv_shift"].shape),
                  _const_spec(lw["a_fwd"].shape), _const_spec(consts["e_fwd"].shape),
                  _const_spec(consts["tri_fwd"].shape), _const_spec(lw["d_skip"].shape), state],
        out_specs=[chunk(SSD_CONV_DIM), chunk(SSD_INNER), state],
        scratch_shapes=[scratch_h],
        compiler_params=_params("parallel", "arbitrary"),
        name="ssd_forward",
    )(xbc, xbc, xbc, dt, rope_cos, rope_sin, lw["conv_w"], lw["conv_b"], consts["conv_shift"],
      lw["a_fwd"], consts["e_fwd"], consts["tri_fwd"], lw["d_skip"], h0_f)

    rchunk = lambda c: pl.BlockSpec((nb, q, c), lambda i, j: (i, nc - 1 - j, 0))
    y, hn_b = pl.pallas_call(
        _ssd_bwd_kernel,
        out_shape=[jax.ShapeDtypeStruct((b, t, SSD_INNER), BF16), state_shape],
        grid=(b // nb, nc),
        in_specs=[rchunk(SSD_CONV_DIM), rchunk(LANES), rchunk(SSD_INNER),
                  _const_spec(lw["a_bwd"].shape), _const_spec(consts["e_bwd"].shape),
                  _const_spec(consts["tri_bwd"].shape), state],
        out_specs=[rchunk(SSD_INNER), state],
        scratch_shapes=[scratch_h],
        compiler_params=_params("parallel", "arbitrary"),
        name="ssd_backward",
    )(xbc_c, dt, y_f, lw["a_bwd"], consts["e_bwd"], consts["tri_bwd"], h0_b)
    return y, hn_f, hn_b


def _bias_kernel(rpb_ref, o_ref):
    lh = pl.program_id(0)
    n_ri = 2 * NA_ROWS - 1
    n_ci = 2 * NA_COLS - 1
    lane = lax.broadcasted_iota(jnp.int32, (GRID_W, LANES), 1)
    qc = lax.broadcasted_iota(jnp.int32, (GRID_W, LANES), 0)
    kc = lane % GRID_W
    cs = jnp.clip(qc - NA_COLS // 2, 0, GRID_W - NA_COLS)
    col_ok = (kc >= cs) & (kc < cs + NA_COLS)
    ci = jnp.clip(kc - qc + (NA_COLS - 1), 0, n_ci - 1)
    tiles = []
    for ri in range(n_ri):
        base = (lh * n_ri + ri) * n_ci
        acc = jnp.full((GRID_W, LANES), NEG, F32)
        for c in range(n_ci):
            acc = jnp.where(col_ok & (ci == c), rpb_ref[base + c] * LOG2E, acc)
        tiles.append(acc)
    masked = jnp.full((GRID_W, LANES), NEG, F32)
    tiles = [masked] + tiles + [masked]
    for e in range(n_ri + 1):
        o_ref[0, e] = jnp.where(lane < GRID_W, tiles[e], tiles[e + 1])


def _bias_table(rpb):
    depth, heads, n_ri, n_ci = rpb.shape
    return pl.pallas_call(
        _bias_kernel,
        out_shape=jax.ShapeDtypeStruct((depth * heads, n_ri + 1, GRID_W, LANES), F32),
        grid=(depth * heads,),
        in_specs=[pl.BlockSpec(memory_space=pltpu.SMEM)],
        out_specs=pl.BlockSpec((1, n_ri + 1, GRID_W, LANES), lambda i: (i, 0, 0, 0)),
        compiler_params=pltpu.CompilerParams(dimension_semantics=("arbitrary",)),
        name="na_bias_table",
    )(rpb.reshape(-1))


def _na_kernel(q_ref, k_ref, v_ref, kc_ref, vc_ref, bias_ref, o_ref, *, nw, rows_n):
    rt = NA_QROWS
    tq = rt * GRID_W
    n_ctx = kc_ref.shape[1]
    r0 = pl.program_id(1) * rt
    base = jnp.clip(r0 - NA_ROWS // 2, 0, rows_n - nw)
    start = pl.multiple_of(base * GRID_W, GRID_W)
    lane_q = lax.broadcasted_iota(jnp.int32, (tq, LANES), 1)
    lane_r = lax.broadcasted_iota(jnp.int32, (GRID_W, LANES), 1)
    lane_c = lax.broadcasted_iota(jnp.int32, (n_ctx, LANES), 1)
    lane_w = lax.broadcasted_iota(jnp.int32, (nw * GRID_W, LANES), 1)
    nblk = nw // 2
    n_cb = n_ctx // LANES

    ents, valid = [], []
    for qi in range(rt):
        r = r0 + qi
        rs = jnp.clip(r - NA_ROWS // 2, 0, rows_n - NA_ROWS)
        e_row, v_row = [], []
        for m in range(nblk):
            j0 = base + 2 * m
            e_row.append(jnp.clip(j0 - r + NA_ROWS, 0, 2 * NA_ROWS - 1))
            ok0 = (j0 >= rs) & (j0 < rs + NA_ROWS)
            ok1 = (j0 + 1 >= rs) & (j0 + 1 < rs + NA_ROWS)
            v_row.append(jnp.where(lane_r < GRID_W, ok0.astype(jnp.int32), ok1.astype(jnp.int32)) > 0)
        ents.append(e_row)
        valid.append(v_row)

    for p in range(NA_WIDTH // LANES):
        cols = slice(p * LANES, (p + 1) * LANES)
        qp = q_ref[0, :, cols]
        kc = kc_ref[0, :, cols]
        vc = vc_ref[0, :, cols]
        kw = k_ref[0, pl.ds(start, nw * GRID_W), cols]
        vw = v_ref[0, pl.ds(start, nw * GRID_W), cols]
        nums = []
        for hh in range(2):
            h = 2 * p + hh
            own = lambda lane: (lane < NA_HEAD_DIM) == (hh == 0)
            qm = jnp.where(own(lane_q), qp, jnp.zeros_like(qp))
            vce = jnp.where(own(lane_c), vc, jnp.ones_like(vc))
            vwe = jnp.where(own(lane_w), vw, jnp.ones_like(vw))
            s_ctx = _dot_nt(qm, kc)
            s_win = _dot_nt(qm, kw)
            p_rows = []
            for qi in range(rt):
                rows = slice(qi * GRID_W, (qi + 1) * GRID_W)
                blocks = [s_ctx[rows, m * LANES:(m + 1) * LANES] for m in range(n_cb)]
                for m in range(nblk):
                    sb = s_win[rows, m * LANES:(m + 1) * LANES] + bias_ref[h, ents[qi][m]]
                    blocks.append(jnp.where(valid[qi][m], sb, NEG))
                mx = blocks[0]
                for sb in blocks[1:]:
                    mx = jnp.maximum(mx, sb)
                mx = jnp.max(mx, axis=-1, keepdims=True)
                p_rows.append(jnp.concatenate([jnp.exp2(sb - mx).astype(BF16) for sb in blocks], axis=1))
            pm = jnp.concatenate(p_rows, axis=0)
            nums.append(_dot(pm[:, :n_ctx], vce) + _dot(pm[:, n_ctx:], vwe))
        num = jnp.where(lane_q < NA_HEAD_DIM, nums[0], nums[1])
        den = pltpu.roll(jnp.where(lane_q < NA_HEAD_DIM, nums[1], nums[0]), NA_HEAD_DIM, 1)
        o_ref[0, :, cols] = (num / den).astype(BF16)


def _neighbourhood_attention(q, k, v, kc, vc, bias):
    b, t, w = q.shape
    n_ctx = kc.shape[1]
    rows_n = t // GRID_W
    nw = NA_QROWS + NA_ROWS
    nw += nw % 2
    tq = NA_QROWS * GRID_W
    whole = lambda n: pl.BlockSpec((1, n, w), lambda i, j: (i, 0, 0))
    return pl.pallas_call(
        functools.partial(_na_kernel, nw=nw, rows_n=rows_n),
        out_shape=jax.ShapeDtypeStruct((b, t, w), BF16),
        grid=(b, t // tq),
        in_specs=[pl.BlockSpec((1, tq, w), lambda i, j: (i, j, 0)), whole(t), whole(t), whole(n_ctx), whole(n_ctx),
                  _const_spec(bias.shape)],
        out_specs=pl.BlockSpec((1, tq, w), lambda i, j: (i, j, 0)),
        compiler_params=_params("parallel", "arbitrary"),
        name="neighbourhood_attention",
    )(q, k, v, kc, vc, bias)


def _ctx_attn_kernel(q_ref, k_ref, v_ref, o_ref):
    n = q_ref.shape[1]
    lane = lax.broadcasted_iota(jnp.int32, (n, LANES), 1)
    for p in range(NA_WIDTH // LANES):
        cols = slice(p * LANES, (p + 1) * LANES)
        qp = q_ref[0, :, cols]
        kp = k_ref[0, :, cols]
        vp = v_ref[0, :, cols]
        nums = []
        for hh in range(2):
            own = (lane < NA_HEAD_DIM) == (hh == 0)
            s = _dot_nt(jnp.where(own, qp, jnp.zeros_like(qp)), kp)
            pm = jnp.exp2(s - jnp.max(s, axis=-1, keepdims=True)).astype(BF16)
            nums.append(_dot(pm, jnp.where(own, vp, jnp.ones_like(vp))))
        num = jnp.where(lane < NA_HEAD_DIM, nums[0], nums[1])
        den = pltpu.roll(jnp.where(lane < NA_HEAD_DIM, nums[1], nums[0]), NA_HEAD_DIM, 1)
        o_ref[0, :, cols] = (num / den).astype(BF16)


def _context_attention(q, k, v):
    b, n, w = q.shape
    spec = pl.BlockSpec((1, n, w), lambda i: (i, 0, 0))
    return pl.pallas_call(
        _ctx_attn_kernel,
        out_shape=jax.ShapeDtypeStruct((b, n, w), BF16),
        grid=(b,),
        in_specs=[spec, spec, spec],
        out_specs=spec,
        compiler_params=_params("parallel"),
        name="context_attention",
    )(q, k, v)


def _merge_ffn_kernel(x_ref, mod_ref, gate_ref, yssd_ref, z_ref, yna_ref, ygm_ref, sn_ref, wa, wb, wc, wo,
                      n2_ref, wfi, wfo, o_ref, *, ffn_chunk):
    d = x_ref.shape[2]
    y = yssd_ref[0].astype(F32) * _silu(z_ref[0].astype(F32))
    y = (y * lax.rsqrt(jnp.mean(y * y, axis=-1, keepdims=True) + EPS) * sn_ref[...]).astype(BF16)
    mixed = (gate_ref[0, :, 0:d].astype(F32) * _dot(y, wa[...])
             + gate_ref[0, :, d:2 * d].astype(F32) * _dot(yna_ref[0], wb[...])
             + gate_ref[0, :, 2 * d:3 * d].astype(F32) * _dot(ygm_ref[0], wc[...]))
    x1 = x_ref[0] + mod_ref[0, 2:3, :] * _dot(mixed.astype(BF16), wo[...])
    xn = x1 * lax.rsqrt(jnp.mean(x1 * x1, axis=-1, keepdims=True) + EPS) * n2_ref[...]
    hb = (xn * (1.0 + mod_ref[0, 4:5, :]) + mod_ref[0, 3:4, :]).astype(BF16)
    hid = wfo.shape[0]
    acc = jnp.zeros_like(x1)
    for c0 in range(0, hid, ffn_chunk):
        a = _dot(hb, wfi[:, c0:c0 + ffn_chunk])
        g = _dot(hb, wfi[:, hid + c0:hid + c0 + ffn_chunk])
        acc = acc + _dot((_silu(a) * g).astype(BF16), wfo[c0:c0 + ffn_chunk, :])
    o_ref[0] = x1 + mod_ref[0, 5:6, :] * acc


def _merge_ffn(x, mod6, gate, y_ssd, z, y_na, y_gm, lw):
    b, t, d = x.shape
    tm = min(TOKEN_TILE, t)
    tok = lambda c: pl.BlockSpec((1, tm, c), lambda i, j: (i, j, 0))
    consts = [lw["ssd_norm"], lw["w_branch_ssd"], lw["w_branch_na"], lw["w_branch_gm"], lw["w_out"], lw["norm2"],
              lw["w_ffn_in"], lw["w_ffn_out"]]
    return pl.pallas_call(
        functools.partial(_merge_ffn_kernel, ffn_chunk=256),
        out_shape=jax.ShapeDtypeStruct((b, t, d), F32),
        grid=(b, t // tm),
        in_specs=[tok(d), pl.BlockSpec((1, 6, d), lambda i, j: (i, 0, 0)),
                  tok(3 * d), tok(SSD_INNER), tok(SSD_INNER), tok(NA_WIDTH), tok(GM_WIDTH)]
                 + [_const_spec(a.shape) for a in consts],
        out_specs=tok(d),
        compiler_params=_params("parallel", "parallel"),
        name="merge_out_ffn",
    )(x, mod6, gate, y_ssd, z, y_na, y_gm, *consts)


def _shared_constants(n_ctx, seq):
    pos = jnp.arange(seq)
    freqs = ROPE_BASE ** (-jnp.arange(ROPE_FREQS, dtype=F32) / ROPE_FREQS)
    ang_row = (pos // GRID_W).astype(F32)[:, None] * freqs
    ang_col = (pos % GRID_W).astype(F32)[:, None] * freqs
    cos = jnp.concatenate([jnp.cos(ang_row), jnp.cos(ang_row), jnp.cos(ang_col), jnp.cos(ang_col)], axis=1)
    sin = jnp.concatenate([-jnp.sin(ang_row), jnp.sin(ang_row), -jnp.sin(ang_col), jnp.sin(ang_col)], axis=1)

    r = jnp.arange(SSD_CHUNK)
    twice = lambda m, axis: jnp.concatenate([m, m], axis=axis).astype(BF16)
    lane_head = jnp.arange(SSD_INNER) // SSD_HEAD_DIM
    rows = jnp.arange(LANES)
    src = jnp.arange(SSD_CHUNK + 2 * HALO)
    offs = [k - SSD_CONV // 2 for k in range(SSD_CONV) if k != SSD_CONV // 2]
    conv_shift = jnp.concatenate([(src[None, :] == r[:, None] + HALO + o) for o in offs], axis=0).astype(BF16)
    return dict(rope_cos=cos, rope_sin=sin,
                ctx_cos=jnp.ones((n_ctx, LANES), F32), ctx_sin=jnp.zeros((n_ctx, LANES), F32),
                tri_fwd=twice(r[None, :] <= r[:, None], 1), tri_bwd=twice(r[None, :] >= r[:, None], 1),
                e_fwd=twice(rows[:, None] == lane_head[None, :], 0),
                e_bwd=twice(rows[:, None] == lane_head[None, :] + SSD_HEADS, 0),
                conv_shift=conv_shift)


def _layer_weights(l, p):
    d = p["w_in"].shape[1]
    sizes = (SSD_INNER, SSD_CONV_DIM, 2 * SSD_HEADS, NA_WIDTH, NA_WIDTH, NA_WIDTH, 2 * GM_WIDTH, 3 * d)
    names = ("w_z", "w_xbc", "w_dt", "w_q", "w_k", "w_v", "w_uv", "w_gate")
    lw, start = {}, 0
    w_in = p["w_in"][l]
    for name, size in zip(names, sizes):
        lw[name] = w_in[:, start:start + size].astype(BF16)
        start += size
    pad_lanes = lambda v: jnp.pad(v, (0, LANES - v.shape[0])).reshape(1, LANES)
    lw["w_dt"] = jnp.pad(lw["w_dt"], ((0, 0), (0, LANES - 2 * SSD_HEADS)))
    lw["dt_bias"] = pad_lanes(p["dt_bias"][l].reshape(-1))
    a = -jnp.exp(p["a_log"][l].astype(F32))
    lw["a_fwd"] = pad_lanes(a[0])
    lw["a_bwd"] = pad_lanes(jnp.concatenate([jnp.zeros((SSD_HEADS,), F32), a[1]]))
    row = lambda v: v.reshape(1, -1)
    lw["norm1"] = row(p["norm1"][l])
    lw["norm2"] = row(p["norm2"][l])
    lw["b_gate"] = row(p["b_gate"][l])
    lw["q_norm"] = row(jnp.tile(p["q_norm"][l], NA_HEADS))
    lw["k_norm"] = row(jnp.tile(p["k_norm"][l], NA_HEADS))
    head = jnp.arange(NA_WIDTH) // NA_HEAD_DIM
    lw["head_blk"] = ((head[:, None] == head[None, :]).astype(F32) / NA_HEAD_DIM).astype(BF16)
    lw["gm_norm"] = row(p["gm_norm"][l])
    w_s = p["w_spatial"][l].astype(BF16)
    lw["w_spatial"] = jnp.concatenate([w_s[0::2], w_s[1::2]], axis=2)
    lw["b_spatial"] = jnp.repeat(p["b_spatial"][l].T, GM_WIDTH // GM_GROUPS, axis=1)
    lw["conv_w"] = jnp.pad(p["conv_w"][l], ((0, 8 - SSD_CONV), (0, 0)))
    lw["conv_b"] = row(p["conv_b"][l])
    lw["d_skip"] = row(jnp.repeat(p["d_skip"][l], SSD_HEAD_DIM))
    lw["ssd_norm"] = row(p["ssd_norm"][l])
    for name in ("w_branch_ssd", "w_branch_na", "w_branch_gm", "w_out", "w_ffn_in", "w_ffn_out"):
        lw[name] = p[name][l].astype(BF16)
    return lw


def kernel(x, c, ctx, c_ctx, w_mod, b_mod, norm1, w_in, b_gate, conv_w, conv_b, a_log, dt_bias, d_skip, ssd_norm,
           q_norm, k_norm, rpb, gm_norm, w_spatial, b_spatial, w_branch_ssd, w_branch_na, w_branch_gm, w_out,
           norm2, w_ffn_in, w_ffn_out):
    p = dict(norm1=norm1, w_in=w_in, b_gate=b_gate, conv_w=conv_w, conv_b=conv_b, a_log=a_log, dt_bias=dt_bias,
             d_skip=d_skip, ssd_norm=ssd_norm, q_norm=q_norm, k_norm=k_norm, gm_norm=gm_norm, w_spatial=w_spatial,
             b_spatial=b_spatial, w_branch_ssd=w_branch_ssd, w_branch_na=w_branch_na, w_branch_gm=w_branch_gm,
             w_out=w_out, norm2=norm2, w_ffn_in=w_ffn_in, w_ffn_out=w_ffn_out)
    b, seq, d = x.shape
    n_ctx = ctx.shape[1]
    depth = w_mod.shape[0]

    c_all = jnp.zeros((8, d), F32).at[:b].set(c).at[b].set(c_ctx)
    mod = _modulation(c_all, w_mod, b_mod)
    bias = _bias_table(rpb)
    consts = _shared_constants(n_ctx, seq)
    zero_state = jnp.zeros((b, SSD_STATE, SSD_INNER), F32)

    xc = ctx
    for l in range(depth):
        lw = _layer_weights(l, p)
        mod_x = mod[l, :b].reshape(b, 6, d)
        mod_c = jnp.broadcast_to(mod[l, b].reshape(1, 6, d), (b, 6, d))
        last = l == depth - 1

        zc, xbcc, dtc, qc, kc, vc, ygm_c, gate_c = _input_projection(xc, mod_c, lw)
        z, xbc, dt, q, k, v, y_gm, gate = _input_projection(x, mod_x, lw)

        yssd_c, s_f, s_b = _ssd(xbcc, dtc, zero_state, zero_state, consts["ctx_cos"], consts["ctx_sin"], lw, consts)
        y_ssd, _, _ = _ssd(xbc, dt, s_f, s_b, consts["rope_cos"], consts["rope_sin"], lw, consts)

        y_na = _neighbourhood_attention(q, k, v, kc, vc, bias[l * NA_HEADS:(l + 1) * NA_HEADS])
        x = _merge_ffn(x, mod_x, gate, y_ssd, z, y_na, y_gm, lw)
        if not last:
            xc = _merge_ffn(xc, mod_c, gate_c, yssd_c, zc, _context_attention(qc, kc, vc), ygm_c, lw)
    return x
```

```python
import functools
import math

import jax
import jax.numpy as jnp
from jax import lax
from jax.experimental import pallas as pl
from jax.experimental.pallas import tpu as pltpu

F32 = jnp.float32
BF16 = jnp.bfloat16

EPS = 1e-6
GRID_W = 64

SSD_INNER = 1024
SSD_HEAD_DIM = 64
SSD_HEADS = 16
SSD_GROUPS = 4
SSD_STATE = 128
SSD_CONV = 5
SSD_CHUNK = 128
SSD_BC = SSD_GROUPS * SSD_STATE
SSD_CONV_DIM = SSD_INNER + 2 * SSD_BC
ROPE_FREQS = 32
ROPE_BASE = 10000.0

NA_HEAD_DIM = 64
NA_WIDTH = 512
NA_HEADS = 8
NA_ROWS = 8
NA_COLS = 16
NA_QROWS = 4

GM_WIDTH = 512
GM_GROUPS = 8
GM_CHUNK = 128

LANES = 128
HALO = 16
NEG = -1e30
LOG2E = math.log2(math.e)
VMEM_LIMIT = 56 * 1024 * 1024
TOKEN_TILE = 512
SSD_SEQS_PER_STEP = 2


def _dot(a, b):
    return jnp.dot(a, b, preferred_element_type=F32)


def _dot_nt(a, b):
    return lax.dot_general(a, b, (((1,), (1,)), ((), ())), preferred_element_type=F32)


def _silu(x):
    return x / (1.0 + jnp.exp(-x))


def _sigmoid(x):
    return 1.0 / (1.0 + jnp.exp(-x))


def _gelu_tanh(x):
    return 0.5 * x * (1.0 + jnp.tanh(math.sqrt(2.0 / math.pi) * (x + 0.044715 * (x * x * x))))


def _softplus(x):
    return jnp.maximum(x, 0.0) + jnp.log(1.0 + jnp.exp(-jnp.abs(x)))


def _split_hi_lo(v):
    hi = v.astype(BF16)
    lo = (v - hi.astype(F32)).astype(BF16)
    return hi, lo


def _const_spec(shape):
    nd = len(shape)
    return pl.BlockSpec(shape, lambda *_: (0,) * nd, pipeline_mode=pl.Buffered(1))


def _params(*semantics):
    return pltpu.CompilerParams(dimension_semantics=semantics, vmem_limit_bytes=VMEM_LIMIT)


def _mod_kernel(c_ref, w_ref, b_ref, o_ref):
    o_ref[0] = _dot(_silu(c_ref[...]), w_ref[0]) + b_ref[0]


def _modulation(c_all, w_mod, b_mod):
    depth, d, n = w_mod.shape
    tn = 1536
    return pl.pallas_call(
        _mod_kernel,
        out_shape=jax.ShapeDtypeStruct((depth, 8, n), F32),
        grid=(depth, n // tn),
        in_specs=[pl.BlockSpec((8, d), lambda l, j: (0, 0)),
                  pl.BlockSpec((1, d, tn), lambda l, j: (l, 0, j)),
                  pl.BlockSpec((1, 1, tn), lambda l, j: (l, 0, j))],
        out_specs=pl.BlockSpec((1, 8, tn), lambda l, j: (l, 0, j)),
        compiler_params=_params("arbitrary", "arbitrary"),
        name="modulation",
    )(c_all, w_mod, b_mod.reshape(depth, 1, n))


def _inproj_kernel(x_ref, mod_ref, n1_ref, wz, wxbc, wdt, wq, wk, wv, wuv, wg, dtb, bg, qn, kn, blk,
                   gmn, ws, bsp, z_o, xbc_o, dt_o, q_o, k_o, v_o, ygm_o, gate_o):
    tm = x_ref.shape[1]
    x = x_ref[0]
    xn = x * lax.rsqrt(jnp.mean(x * x, axis=-1, keepdims=True) + EPS) * n1_ref[...]
    hb = (xn * (1.0 + mod_ref[0, 1:2, :]) + mod_ref[0, 0:1, :]).astype(BF16)

    cw = 512
    for n0 in range(0, z_o.shape[2], cw):
        z_o[0, :, n0:n0 + cw] = _dot(hb, wz[:, n0:n0 + cw]).astype(BF16)
    for n0 in range(0, xbc_o.shape[2], cw):
        xbc_o[0, :, n0:n0 + cw] = _dot(hb, wxbc[:, n0:n0 + cw]).astype(BF16)
    for n0 in range(0, gate_o.shape[2], cw):
        gate_o[0, :, n0:n0 + cw] = _sigmoid(_dot(hb, wg[:, n0:n0 + cw]) + bg[:, n0:n0 + cw]).astype(BF16)

    dt_o[0] = _softplus(_dot(hb, wdt[...]) + dtb[...])

    def head_norm(t, w_row):
        ms = _dot((t * t).astype(BF16), blk[...])
        return t * lax.rsqrt(ms + EPS) * w_row

    q_o[0] = (head_norm(_dot(hb, wq[...]), qn[...]) * (NA_HEAD_DIM ** -0.5 * LOG2E)).astype(BF16)
    k_o[0] = head_norm(_dot(hb, wk[...]), kn[...]).astype(BF16)
    v_o[0] = _dot(hb, wv[...]).astype(BF16)

    g = _gelu_tanh(_dot(hb, wuv[...]))
    u = g[:, :GM_WIDTH]
    v = g[:, GM_WIDTH:]
    vb = (v * lax.rsqrt(jnp.mean(v * v, axis=-1, keepdims=True) + EPS) * gmn[...]).astype(BF16)
    first = lax.broadcasted_iota(jnp.int32, (GM_CHUNK, LANES), 1) < LANES // 2
    for c0 in range(0, tm, GM_CHUNK):
        for p in range(GM_WIDTH // LANES):
            cols = slice(p * LANES, (p + 1) * LANES)
            vp = vb[c0:c0 + GM_CHUNK, cols]
            zero = jnp.zeros_like(vp)
            stacked = jnp.concatenate([jnp.where(first, vp, zero), jnp.where(first, zero, vp)], axis=0)
            mixed = _dot(ws[p], stacked) + bsp[:, cols]
            ygm_o[0, c0:c0 + GM_CHUNK, cols] = (u[c0:c0 + GM_CHUNK, cols] * mixed).astype(BF16)


def _input_projection(x, mod6, lw):
    b, t, d = x.shape
    tm = min(TOKEN_TILE, t)
    tok = lambda c: pl.BlockSpec((1, tm, c), lambda i, j: (i, j, 0))
    consts = [lw["norm1"], lw["w_z"], lw["w_xbc"], lw["w_dt"], lw["w_q"], lw["w_k"], lw["w_v"], lw["w_uv"],
              lw["w_gate"], lw["dt_bias"], lw["b_gate"], lw["q_norm"], lw["k_norm"], lw["head_blk"],
              lw["gm_norm"], lw["w_spatial"], lw["b_spatial"]]
    widths = [(SSD_INNER, BF16), (SSD_CONV_DIM, BF16), (LANES, F32), (NA_WIDTH, BF16), (NA_WIDTH, BF16),
              (NA_WIDTH, BF16), (GM_WIDTH, BF16), (3 * d, BF16)]
    return pl.pallas_call(
        _inproj_kernel,
        out_shape=[jax.ShapeDtypeStruct((b, t, c), ty) for c, ty in widths],
        grid=(b, t // tm),
        in_specs=[tok(d), pl.BlockSpec((1, 6, d), lambda i, j: (i, 0, 0))] + [_const_spec(a.shape) for a in consts],
        out_specs=[tok(c) for c, _ in widths],
        compiler_params=_params("parallel", "parallel"),
        name="input_projection",
    )(x, mod6, *consts)


def _ssd_chunks(seqs, a_row, e2_ref, tri2_ref, reverse, head_off):
    q = seqs[0][0].shape[0]
    last = 0 if reverse else q - 1
    gw = SSD_INNER // SSD_GROUPS
    hpg = gw // SSD_HEAD_DIM
    groups = range(SSD_GROUPS)
    gcols = lambda g: slice(g * gw, (g + 1) * gw)
    ncols = lambda g: slice(g * SSD_STATE, (g + 1) * SSD_STATE)
    tri2 = tri2_ref[...]

    acums = []
    for xs, bm, cm, dt, h_ref in seqs:
        a = dt * a_row
        a1 = a.astype(BF16)
        r1 = a - a1.astype(F32)
        a2 = r1.astype(BF16)
        a3 = (r1 - a2.astype(F32)).astype(BF16)
        acums.append(_dot(tri2, jnp.concatenate([a1, a2], axis=0)) + _dot(tri2[:, :q], a3))

    wides = []
    for (xs, bm, cm, dt, h_ref), acum in zip(seqs, acums):
        a_last = acum[last:last + 1, :]
        per_head = jnp.concatenate([dt, dt * jnp.exp2(a_last - acum),
                                    jnp.broadcast_to(jnp.exp2(a_last), (8, LANES))], axis=0)
        hi, lo = _split_hi_lo(per_head)
        wides.append(_dot(jnp.concatenate([hi, lo], axis=1), e2_ref[...]))

    cbs, y_inters, bts, h_prevs = [], [], [], []
    for xs, bm, cm, dt, h_ref in seqs:
        cgbs = [cm[:, ncols(g)].astype(BF16) for g in groups]
        h_prev = [h_ref[:, gcols(g)] for g in groups]
        cbs.append([_dot_nt(cgbs[g], bm[:, ncols(g)].astype(BF16)) for g in groups])
        y_inters.append([_dot(cgbs[g], h_prev[g].astype(BF16)) for g in groups])
        bts.append([bm[:, ncols(g)].astype(F32).T.astype(BF16) for g in groups])
        h_prevs.append(h_prev)

    row = lax.broadcasted_iota(jnp.int32, (q, q), 0)
    col = lax.broadcasted_iota(jnp.int32, (q, q), 1)
    in_scan = (col >= row) if reverse else (col <= row)
    lane = lax.broadcasted_iota(jnp.int32, (q, LANES), 1)
    lane_head = lax.broadcasted_iota(jnp.int32, (q, gw), 1) // SSD_HEAD_DIM

    lhs, rhs, dec_outs, xw_bs, h_decs = [], [], [], [], []
    for si, (xs, bm, cm, dt, h_ref) in enumerate(seqs):
        acum, wide = acums[si], wides[si]
        acum_t = acum.T
        xdt_b = (xs * wide[0:q]).astype(BF16)
        xw_bs.append((xs * wide[q:2 * q]).astype(BF16))
        h_decs.append(wide[2 * q:2 * q + 1])
        for g in groups:
            xg = xdt_b[:, gcols(g)]
            ms, decs = [], []
            for j in range(hpg):
                hc = head_off + g * hpg + j
                colx = jnp.broadcast_to(acum[:, hc:hc + 1], (q, q))
                decay = jnp.exp2(jnp.where(in_scan, colx - acum_t[hc:hc + 1, :], NEG))
                ms.append((cbs[si][g] * decay).astype(BF16))
                decs.append(jnp.exp2(colx))
            lhs.append(jnp.concatenate(ms, axis=1))
            rhs.append(jnp.concatenate([jnp.where(lane_head == j, xg, jnp.zeros_like(xg)) for j in range(hpg)], axis=0))
            dec_outs.append(jnp.concatenate([jnp.where(lane < SSD_HEAD_DIM, decs[2 * i], decs[2 * i + 1])
                                             for i in range(hpg // 2)], axis=1))
    y_intra = [_dot(l, r) for l, r in zip(lhs, rhs)]
    h_add = [_dot(bts[si][g], xw_bs[si][:, gcols(g)]) for si in range(len(seqs)) for g in groups]
    ys = []
    for si, (xs, bm, cm, dt, h_ref) in enumerate(seqs):
        for g in groups:
            h_ref[:, gcols(g)] = h_prevs[si][g] * h_decs[si][:, gcols(g)] + h_add[si * SSD_GROUPS + g]
        ys.append(jnp.concatenate([y_intra[si * SSD_GROUPS + g] + y_inters[si][g] * dec_outs[si * SSD_GROUPS + g]
                                   for g in groups], axis=1))
    return ys


def _ssd_fwd_kernel(cur_ref, prev_ref, next_ref, dt_ref, cos_ref, sin_ref, cw_ref, cb_ref, shift_ref, a_ref, e_ref,
                    tri_ref, dsk_ref, h0_ref, xbc_o, y_o, hn_o, h_ref):
    i = pl.program_id(1)
    nc = pl.num_programs(1)
    nb, q, _ = cur_ref.shape

    @pl.when(i == 0)
    def _():
        h_ref[...] = h0_ref[...]

    pv = jnp.where(i == 0, 0.0, 1.0).astype(BF16)
    nv = jnp.where(i == nc - 1, 0.0, 1.0).astype(BF16)
    lane = lax.broadcasted_iota(jnp.int32, (q, LANES), 1)
    cos = cos_ref[...]
    sin = sin_ref[...]

    def rope(t):
        sw = jnp.where((lane & ROPE_FREQS) == 0, pltpu.roll(t, LANES - ROPE_FREQS, 1), pltpu.roll(t, ROPE_FREQS, 1))
        return t * cos + sw * sin

    cwid = 256
    side_taps = [k for k in range(SSD_CONV) if k != SSD_CONV // 2]
    seqs = []
    for s in range(nb):
        ext = jnp.concatenate([prev_ref[s] * pv, cur_ref[s], next_ref[s] * nv], axis=0)
        shifted = [_dot(shift_ref[...], ext[:, c0:c0 + cwid]) for c0 in range(0, SSD_CONV_DIM, cwid)]
        parts = []
        for ci, c0 in enumerate(range(0, SSD_CONV_DIM, cwid)):
            acc = cb_ref[:, c0:c0 + cwid] + (cw_ref[SSD_CONV // 2:SSD_CONV // 2 + 1, c0:c0 + cwid]
                                             * cur_ref[s, :, c0:c0 + cwid].astype(F32))
            for n, k in enumerate(side_taps):
                acc = acc + cw_ref[k:k + 1, c0:c0 + cwid] * shifted[ci][n * q:(n + 1) * q]
            parts.append(_silu(acc))
        xs = jnp.concatenate(parts[:SSD_INNER // cwid], axis=1)
        bc = jnp.concatenate(parts[SSD_INNER // cwid:], axis=1)
        bc = jnp.concatenate([rope(bc[:, g * LANES:(g + 1) * LANES]) for g in range(2 * SSD_GROUPS)], axis=1)
        xbc_o[s, :, :SSD_INNER] = xs.astype(BF16)
        xbc_o[s, :, SSD_INNER:] = bc.astype(BF16)
        seqs.append((xs, bc[:, :SSD_BC], bc[:, SSD_BC:], dt_ref[s], h_ref.at[s]))
    ys = _ssd_chunks(seqs, a_ref[...], e_ref, tri_ref, False, 0)
    for s in range(nb):
        y_o[s] = (ys[s] + dsk_ref[...] * seqs[s][0]).astype(BF16)

    @pl.when(i == nc - 1)
    def _():
        hn_o[...] = h_ref[...]


def _ssd_bwd_kernel(xbc_ref, dt_ref, yf_ref, a_ref, e_ref, tri_ref, h0_ref, y_o, hn_o, h_ref):
    i = pl.program_id(1)

    @pl.when(i == 0)
    def _():
        h_ref[...] = h0_ref[...]

    nb = xbc_ref.shape[0]
    seqs = [(xbc_ref[s, :, :SSD_INNER].astype(F32), xbc_ref[s, :, SSD_INNER:SSD_INNER + SSD_BC],
             xbc_ref[s, :, SSD_INNER + SSD_BC:], dt_ref[s], h_ref.at[s]) for s in range(nb)]
    ys = _ssd_chunks(seqs, a_ref[...], e_ref, tri_ref, True, SSD_HEADS)
    for s in range(nb):
        y_o[s] = (ys[s] + yf_ref[s].astype(F32)).astype(BF16)

    @pl.when(i == pl.num_programs(1) - 1)
    def _():
        hn_o[...] = h_ref[...]


def _ssd(xbc, dt, h0_f, h0_b, rope_cos, rope_sin, lw, consts):
    b, t, _ = xbc.shape
    q = SSD_CHUNK
    nc = t // q
    nb = SSD_SEQS_PER_STEP if b % SSD_SEQS_PER_STEP == 0 else 1
    hb = q // HALO
    n_hb = t // HALO
    chunk = lambda c: pl.BlockSpec((nb, q, c), lambda i, j: (i, j, 0))
    state = pl.BlockSpec((nb, SSD_STATE, SSD_INNER), lambda i, j: (i, 0, 0))
    state_shape = jax.ShapeDtypeStruct((b, SSD_STATE, SSD_INNER), F32)
    scratch_h = pltpu.VMEM((nb, SSD_STATE, SSD_INNER), F32)

    xbc_c, y_f, hn_f = pl.pallas_call(
        _ssd_fwd_kernel,
        out_shape=[jax.ShapeDtypeStruct((b, t, SSD_CONV_DIM), BF16), jax.ShapeDtypeStruct((b, t, SSD_INNER), BF16),
                   state_shape],
        grid=(b // nb, nc),
        in_specs=[chunk(SSD_CONV_DIM),
                  pl.BlockSpec((nb, HALO, SSD_CONV_DIM), lambda i, j: (i, jnp.maximum(j * hb - 1, 0), 0)),
                  pl.BlockSpec((nb, HALO, SSD_CONV_DIM), lambda i, j: (i, jnp.minimum((j + 1) * hb, n_hb - 1), 0)),
                  chunk(LANES),
                  pl.BlockSpec((q, LANES), lambda i, j: (j, 0)),
                  pl.BlockSpec((q, LANES), lambda i, j: (j, 0)),
                  _const_spec(lw["conv_w"].shape), _const_spec(lw["conv_b"].shape),
                  _const_spec(consts["conv_shift"].shape),
                  _const_spec(lw["a_fwd"].shape), _const_spec(consts["e_fwd"].shape),
                  _const_spec(consts["tri_fwd"].shape), _const_spec(lw["d_skip"].shape), state],
        out_specs=[chunk(SSD_CONV_DIM), chunk(SSD_INNER), state],
        scratch_shapes=[scratch_h],
        compiler_params=_params("parallel", "arbitrary"),
        name="ssd_forward",
    )(xbc, xbc, xbc, dt, rope_cos, rope_sin, lw["conv_w"], lw["conv_b"], consts["conv_shift"],
      lw["a_fwd"], consts["e_fwd"], consts["tri_fwd"], lw["d_skip"], h0_f)

    rchunk = lambda c: pl.BlockSpec((nb, q, c), lambda i, j: (i, nc - 1 - j, 0))
    y, hn_b = pl.pallas_call(
        _ssd_bwd_kernel,
        out_shape=[jax.ShapeDtypeStruct((b, t, SSD_INNER), BF16), state_shape],
        grid=(b // nb, nc),
        in_specs=[rchunk(SSD_CONV_DIM), rchunk(LANES), rchunk(SSD_INNER),
                  _const_spec(lw["a_bwd"].shape), _const_spec(consts["e_bwd"].shape),
                  _const_spec(consts["tri_bwd"].shape), state],
        out_specs=[rchunk(SSD_INNER), state],
        scratch_shapes=[scratch_h],
        compiler_params=_params("parallel", "arbitrary"),
        name="ssd_backward",
    )(xbc_c, dt, y_f, lw["a_bwd"], consts["e_bwd"], consts["tri_bwd"], h0_b)
    return y, hn_f, hn_b


def _bias_kernel(rpb_ref, o_ref):
    lh = pl.program_id(0)
    n_ri = 2 * NA_ROWS - 1
    n_ci = 2 * NA_COLS - 1
    lane = lax.broadcasted_iota(jnp.int32, (GRID_W, LANES), 1)
    qc = lax.broadcasted_iota(jnp.int32, (GRID_W, LANES), 0)
    kc = lane % GRID_W
    cs = jnp.clip(qc - NA_COLS // 2, 0, GRID_W - NA_COLS)
    col_ok = (kc >= cs) & (kc < cs + NA_COLS)
    ci = jnp.clip(kc - qc + (NA_COLS - 1), 0, n_ci - 1)
    tiles = []
    for ri in range(n_ri):
        base = (lh * n_ri + ri) * n_ci
        acc = jnp.full((GRID_W, LANES), NEG, F32)
        for c in range(n_ci):
            acc = jnp.where(col_ok & (ci == c), rpb_ref[base + c] * LOG2E, acc)
        tiles.append(acc)
    masked = jnp.full((GRID_W, LANES), NEG, F32)
    tiles = [masked] + tiles + [masked]
    for e in range(n_ri + 1):
        o_ref[0, e] = jnp.where(lane < GRID_W, tiles[e], tiles[e + 1])


def _bias_table(rpb):
    depth, heads, n_ri, n_ci = rpb.shape
    return pl.pallas_call(
        _bias_kernel,
        out_shape=jax.ShapeDtypeStruct((depth * heads, n_ri + 1, GRID_W, LANES), F32),
        grid=(depth * heads,),
        in_specs=[pl.BlockSpec(memory_space=pltpu.SMEM)],
        out_specs=pl.BlockSpec((1, n_ri + 1, GRID_W, LANES), lambda i: (i, 0, 0, 0)),
        compiler_params=pltpu.CompilerParams(dimension_semantics=("arbitrary",)),
        name="na_bias_table",
    )(rpb.reshape(-1))


def _na_tile(q_ref, k_ref, v_ref, kc_ref, vc_ref, bias_ref, o_ref, start, nw, plan):
    tq = q_ref.shape[1]
    n_ctx = kc_ref.shape[1]
    n_cb = n_ctx // LANES
    lane_q = lax.broadcasted_iota(jnp.int32, (tq, LANES), 1)
    lane_k = lax.broadcasted_iota(jnp.int32, (n_ctx + nw * GRID_W, LANES), 1)
    n_pairs = NA_WIDTH // LANES
    own = [lambda lane, hh=hh: (lane < NA_HEAD_DIM) == (hh == 0) for hh in range(2)]
    col = lambda p: slice(p * LANES, (p + 1) * LANES)
    scores_all = []
    for p in range(n_pairs):
        qp = q_ref[0, :, col(p)]
        keys = jnp.concatenate([kc_ref[0, :, col(p)], k_ref[0, pl.ds(start, nw * GRID_W), col(p)]], axis=0)
        for hh in range(2):
            scores_all.append(_dot_nt(jnp.where(own[hh](lane_q), qp, jnp.zeros_like(qp)), keys))
    p_mats = []
    for h, scores in enumerate(scores_all):
        p_rows = []
        for qi, row_plan in enumerate(plan):
            rows = slice(qi * GRID_W, (qi + 1) * GRID_W)
            blocks = [scores[rows, m * LANES:(m + 1) * LANES] for m in range(n_cb)]
            for m, (ent, ok) in enumerate(row_plan):
                if ok is None:
                    blocks.append(None)
                    continue
                sb = scores[rows, (n_cb + m) * LANES:(n_cb + m + 1) * LANES] + bias_ref[h, ent]
                blocks.append(sb if ok is True else jnp.where(ok, sb, NEG))
            live = [sb for sb in blocks if sb is not None]
            mx = live[0]
            for sb in live[1:]:
                mx = jnp.maximum(mx, sb)
            mx = jnp.max(mx, axis=-1, keepdims=True)
            p_rows.append(jnp.concatenate(
                [jnp.zeros((GRID_W, LANES), BF16) if sb is None else jnp.exp2(sb - mx).astype(BF16)
                 for sb in blocks], axis=1))
        p_mats.append(jnp.concatenate(p_rows, axis=0))
    for p in range(n_pairs):
        vals = jnp.concatenate([vc_ref[0, :, col(p)], v_ref[0, pl.ds(start, nw * GRID_W), col(p)]], axis=0)
        nums = [_dot(p_mats[2 * p + hh], jnp.where(own[hh](lane_k), vals, jnp.ones_like(vals))) for hh in range(2)]
        num = jnp.where(lane_q < NA_HEAD_DIM, nums[0], nums[1])
        den = pltpu.roll(jnp.where(lane_q < NA_HEAD_DIM, nums[1], nums[0]), NA_HEAD_DIM, 1)
        o_ref[0, :, col(p)] = (num / den).astype(BF16)


def _na_kernel(q_ref, k_ref, v_ref, kc_ref, vc_ref, bias_ref, o_ref, *, nw, rows_n):
    rt = NA_QROWS
    half = NA_ROWS // 2
    r0 = pl.program_id(1) * rt
    lane_r = lax.broadcasted_iota(jnp.int32, (GRID_W, LANES), 1)
    first_half = lane_r < GRID_W
    nblk = nw // 2
    interior = (r0 >= half) & (r0 - half <= rows_n - nw)
    args = (q_ref, k_ref, v_ref, kc_ref, vc_ref, bias_ref, o_ref)

    @pl.when(interior)
    def _():
        plan = []
        for qi in range(rt):
            row_plan = []
            for m in range(nblk):
                ok0 = qi <= 2 * m < qi + NA_ROWS
                ok1 = qi <= 2 * m + 1 < qi + NA_ROWS
                ok = True if ok0 and ok1 else None if not (ok0 or ok1) else first_half if ok0 else ~first_half
                row_plan.append((half + 2 * m - qi, ok))
            plan.append(row_plan)
        _na_tile(*args, pl.multiple_of((r0 - half) * GRID_W, GRID_W), nw, plan)

    @pl.when(~interior)
    def _():
        base = jnp.clip(r0 - half, 0, rows_n - nw)
        plan = []
        for qi in range(rt):
            r = r0 + qi
            rs = jnp.clip(r - half, 0, rows_n - NA_ROWS)
            row_plan = []
            for m in range(nblk):
                j0 = base + 2 * m
                ok0 = (j0 >= rs) & (j0 < rs + NA_ROWS)
                ok1 = (j0 + 1 >= rs) & (j0 + 1 < rs + NA_ROWS)
                ok = jnp.where(first_half, ok0.astype(jnp.int32), ok1.astype(jnp.int32)) > 0
                row_plan.append((jnp.clip(j0 - r + NA_ROWS, 0, 2 * NA_ROWS - 1), ok))
            plan.append(row_plan)
        _na_tile(*args, pl.multiple_of(base * GRID_W, GRID_W), nw, plan)


def _neighbourhood_attention(q, k, v, kc, vc, bias):
    b, t, w = q.shape
    n_ctx = kc.shape[1]
    rows_n = t // GRID_W
    nw = NA_QROWS + NA_ROWS
    nw += nw % 2
    tq = NA_QROWS * GRID_W
    whole = lambda n: pl.BlockSpec((1, n, w), lambda i, j: (i, 0, 0))
    return pl.pallas_call(
        functools.partial(_na_kernel, nw=nw, rows_n=rows_n),
        out_shape=jax.ShapeDtypeStruct((b, t, w), BF16),
        grid=(b, t // tq),
        in_specs=[pl.BlockSpec((1, tq, w), lambda i, j: (i, j, 0)), whole(t), whole(t), whole(n_ctx), whole(n_ctx),
                  _const_spec(bias.shape)],
        out_specs=pl.BlockSpec((1, tq, w), lambda i, j: (i, j, 0)),
        compiler_params=_params("parallel", "arbitrary"),
        name="neighbourhood_attention",
    )(q, k, v, kc, vc, bias)


def _ctx_attn_kernel(q_ref, k_ref, v_ref, o_ref):
    n = q_ref.shape[1]
    lane = lax.broadcasted_iota(jnp.int32, (n, LANES), 1)
    for p in range(NA_WIDTH // LANES):
        cols = slice(p * LANES, (p + 1) * LANES)
        qp = q_ref[0, :, cols]
        kp = k_ref[0, :, cols]
        vp = v_ref[0, :, cols]
        nums = []
        for hh in range(2):
            own = (lane < NA_HEAD_DIM) == (hh == 0)
            s = _dot_nt(jnp.where(own, qp, jnp.zeros_like(qp)), kp)
            pm = jnp.exp2(s - jnp.max(s, axis=-1, keepdims=True)).astype(BF16)
            nums.append(_dot(pm, jnp.where(own, vp, jnp.ones_like(vp))))
        num = jnp.where(lane < NA_HEAD_DIM, nums[0], nums[1])
        den = pltpu.roll(jnp.where(lane < NA_HEAD_DIM, nums[1], nums[0]), NA_HEAD_DIM, 1)
        o_ref[0, :, cols] = (num / den).astype(BF16)


def _context_attention(q, k, v):
    b, n, w = q.shape
    spec = pl.BlockSpec((1, n, w), lambda i: (i, 0, 0))
    return pl.pallas_call(
        _ctx_attn_kernel,
        out_shape=jax.ShapeDtypeStruct((b, n, w), BF16),
        grid=(b,),
        in_specs=[spec, spec, spec],
        out_specs=spec,
        compiler_params=_params("parallel"),
        name="context_attention",
    )(q, k, v)


def _merge_ffn_kernel(x_ref, mod_ref, gate_ref, yssd_ref, z_ref, yna_ref, ygm_ref, sn_ref, wa, wb, wc, wo,
                      n2_ref, wfi, wfo, o_ref, *, ffn_chunk):
    d = x_ref.shape[2]
    y = yssd_ref[0].astype(F32) * _silu(z_ref[0].astype(F32))
    y = (y * lax.rsqrt(jnp.mean(y * y, axis=-1, keepdims=True) + EPS) * sn_ref[...]).astype(BF16)
    mixed = (gate_ref[0, :, 0:d].astype(F32) * _dot(y, wa[...])
             + gate_ref[0, :, d:2 * d].astype(F32) * _dot(yna_ref[0], wb[...])
             + gate_ref[0, :, 2 * d:3 * d].astype(F32) * _dot(ygm_ref[0], wc[...]))
    x1 = x_ref[0] + mod_ref[0, 2:3, :] * _dot(mixed.astype(BF16), wo[...])
    xn = x1 * lax.rsqrt(jnp.mean(x1 * x1, axis=-1, keepdims=True) + EPS) * n2_ref[...]
    hb = (xn * (1.0 + mod_ref[0, 4:5, :]) + mod_ref[0, 3:4, :]).astype(BF16)
    hid = wfo.shape[0]
    acc = jnp.zeros_like(x1)
    for c0 in range(0, hid, ffn_chunk):
        a = _dot(hb, wfi[:, c0:c0 + ffn_chunk])
        g = _dot(hb, wfi[:, hid + c0:hid + c0 + ffn_chunk])
        acc = acc + _dot((_silu(a) * g).astype(BF16), wfo[c0:c0 + ffn_chunk, :])
    o_ref[0] = x1 + mod_ref[0, 5:6, :] * acc


def _merge_ffn(x, mod6, gate, y_ssd, z, y_na, y_gm, lw):
    b, t, d = x.shape
    tm = min(TOKEN_TILE, t)
    tok = lambda c: pl.BlockSpec((1, tm, c), lambda i, j: (i, j, 0))
    consts = [lw["ssd_norm"], lw["w_branch_ssd"], lw["w_branch_na"], lw["w_branch_gm"], lw["w_out"], lw["norm2"],
              lw["w_ffn_in"], lw["w_ffn_out"]]
    return pl.pallas_call(
        functools.partial(_merge_ffn_kernel, ffn_chunk=256),
        out_shape=jax.ShapeDtypeStruct((b, t, d), F32),
        grid=(b, t // tm),
        in_specs=[tok(d), pl.BlockSpec((1, 6, d), lambda i, j: (i, 0, 0)),
                  tok(3 * d), tok(SSD_INNER), tok(SSD_INNER), tok(NA_WIDTH), tok(GM_WIDTH)]
                 + [_const_spec(a.shape) for a in consts],
        out_specs=tok(d),
        compiler_params=_params("parallel", "parallel"),
        name="merge_out_ffn",
    )(x, mod6, gate, y_ssd, z, y_na, y_gm, *consts)


def _shared_constants(n_ctx, seq):
    pos = jnp.arange(seq)
    freqs = ROPE_BASE ** (-jnp.arange(ROPE_FREQS, dtype=F32) / ROPE_FREQS)
    ang_row = (pos // GRID_W).astype(F32)[:, None] * freqs
    ang_col = (pos % GRID_W).astype(F32)[:, None] * freqs
    cos = jnp.concatenate([jnp.cos(ang_row), jnp.cos(ang_row), jnp.cos(ang_col), jnp.cos(ang_col)], axis=1)
    sin = jnp.concatenate([-jnp.sin(ang_row), jnp.sin(ang_row), -jnp.sin(ang_col), jnp.sin(ang_col)], axis=1)

    r = jnp.arange(SSD_CHUNK)
    twice = lambda m, axis: jnp.concatenate([m, m], axis=axis).astype(BF16)
    lane_head = jnp.arange(SSD_INNER) // SSD_HEAD_DIM
    rows = jnp.arange(LANES)
    src = jnp.arange(SSD_CHUNK + 2 * HALO)
    offs = [k - SSD_CONV // 2 for k in range(SSD_CONV) if k != SSD_CONV // 2]
    conv_shift = jnp.concatenate([(src[None, :] == r[:, None] + HALO + o) for o in offs], axis=0).astype(BF16)
    return dict(rope_cos=cos, rope_sin=sin,
                ctx_cos=jnp.ones((n_ctx, LANES), F32), ctx_sin=jnp.zeros((n_ctx, LANES), F32),
                tri_fwd=twice(r[None, :] <= r[:, None], 1), tri_bwd=twice(r[None, :] >= r[:, None], 1),
                e_fwd=twice(rows[:, None] == lane_head[None, :], 0),
                e_bwd=twice(rows[:, None] == lane_head[None, :] + SSD_HEADS, 0),
                conv_shift=conv_shift)


def _layer_weights(l, p):
    d = p["w_in"].shape[1]
    sizes = (SSD_INNER, SSD_CONV_DIM, 2 * SSD_HEADS, NA_WIDTH, NA_WIDTH, NA_WIDTH, 2 * GM_WIDTH, 3 * d)
    names = ("w_z", "w_xbc", "w_dt", "w_q", "w_k", "w_v", "w_uv", "w_gate")
    lw, start = {}, 0
    w_in = p["w_in"][l]
    for name, size in zip(names, sizes):
        lw[name] = w_in[:, start:start + size].astype(BF16)
        start += size
    pad_lanes = lambda v: jnp.pad(v, (0, LANES - v.shape[0])).reshape(1, LANES)
    lw["w_dt"] = jnp.pad(lw["w_dt"], ((0, 0), (0, LANES - 2 * SSD_HEADS)))
    lw["dt_bias"] = pad_lanes(p["dt_bias"][l].reshape(-1))
    a = -jnp.exp(p["a_log"][l].astype(F32))
    lw["a_fwd"] = pad_lanes(a[0] * LOG2E)
    lw["a_bwd"] = pad_lanes(jnp.concatenate([jnp.zeros((SSD_HEADS,), F32), a[1] * LOG2E]))
    row = lambda v: v.reshape(1, -1)
    lw["norm1"] = row(p["norm1"][l])
    lw["norm2"] = row(p["norm2"][l])
    lw["b_gate"] = row(p["b_gate"][l])
    lw["q_norm"] = row(jnp.tile(p["q_norm"][l], NA_HEADS))
    lw["k_norm"] = row(jnp.tile(p["k_norm"][l], NA_HEADS))
    head = jnp.arange(NA_WIDTH) // NA_HEAD_DIM
    lw["head_blk"] = ((head[:, None] == head[None, :]).astype(F32) / NA_HEAD_DIM).astype(BF16)
    lw["gm_norm"] = row(p["gm_norm"][l])
    w_s = p["w_spatial"][l].astype(BF16)
    lw["w_spatial"] = jnp.concatenate([w_s[0::2], w_s[1::2]], axis=2)
    lw["b_spatial"] = jnp.repeat(p["b_spatial"][l].T, GM_WIDTH // GM_GROUPS, axis=1)
    lw["conv_w"] = jnp.pad(p["conv_w"][l], ((0, 8 - SSD_CONV), (0, 0)))
    lw["conv_b"] = row(p["conv_b"][l])
    lw["d_skip"] = row(jnp.repeat(p["d_skip"][l], SSD_HEAD_DIM))
    lw["ssd_norm"] = row(p["ssd_norm"][l])
    for name in ("w_branch_ssd", "w_branch_na", "w_branch_gm", "w_out", "w_ffn_in", "w_ffn_out"):
        lw[name] = p[name][l].astype(BF16)
    return lw


def kernel(x, c, ctx, c_ctx, w_mod, b_mod, norm1, w_in, b_gate, conv_w, conv_b, a_log, dt_bias, d_skip, ssd_norm,
           q_norm, k_norm, rpb, gm_norm, w_spatial, b_spatial, w_branch_ssd, w_branch_na, w_branch_gm, w_out,
           norm2, w_ffn_in, w_ffn_out):
    p = dict(norm1=norm1, w_in=w_in, b_gate=b_gate, conv_w=conv_w, conv_b=conv_b, a_log=a_log, dt_bias=dt_bias,
             d_skip=d_skip, ssd_norm=ssd_norm, q_norm=q_norm, k_norm=k_norm, gm_norm=gm_norm, w_spatial=w_spatial,
             b_spatial=b_spatial, w_branch_ssd=w_branch_ssd, w_branch_na=w_branch_na, w_branch_gm=w_branch_gm,
             w_out=w_out, norm2=norm2, w_ffn_in=w_ffn_in, w_ffn_out=w_ffn_out)
    b, seq, d = x.shape
    n_ctx = ctx.shape[1]
    depth = w_mod.shape[0]

    c_all = jnp.zeros((8, d), F32).at[:b].set(c).at[b].set(c_ctx)
    mod = _modulation(c_all, w_mod, b_mod)
    bias = _bias_table(rpb)
    consts = _shared_constants(n_ctx, seq)
    zero_state = jnp.zeros((b, SSD_STATE, SSD_INNER), F32)

    xc = ctx
    for l in range(depth):
        lw = _layer_weights(l, p)
        mod_x = mod[l, :b].reshape(b, 6, d)
        mod_c = jnp.broadcast_to(mod[l, b].reshape(1, 6, d), (b, 6, d))
        last = l == depth - 1

        zc, xbcc, dtc, qc, kc, vc, ygm_c, gate_c = _input_projection(xc, mod_c, lw)
        z, xbc, dt, q, k, v, y_gm, gate = _input_projection(x, mod_x, lw)

        yssd_c, s_f, s_b = _ssd(xbcc, dtc, zero_state, zero_state, consts["ctx_cos"], consts["ctx_sin"], lw, consts)
        y_ssd, _, _ = _ssd(xbc, dt, s_f, s_b, consts["rope_cos"], consts["rope_sin"], lw, consts)

        y_na = _neighbourhood_attention(q, k, v, kc, vc, bias[l * NA_HEADS:(l + 1) * NA_HEADS])
        x = _merge_ffn(x, mod_x, gate, y_ssd, z, y_na, y_gm, lw)
        if not last:
            xc = _merge_ffn(xc, mod_c, gate_c, yssd_c, zc, _context_attention(qc, kc, vc), ygm_c, lw)
    return x
```

```python
import functools
import math

import jax
import jax.numpy as jnp
from jax import lax
from jax.experimental import pallas as pl
from jax.experimental.pallas import tpu as pltpu

F32 = jnp.float32
BF16 = jnp.bfloat16

EPS = 1e-6
GRID_W = 64

SSD_INNER = 1024
SSD_HEAD_DIM = 64
SSD_HEADS = 16
SSD_GROUPS = 4
SSD_STATE = 128
SSD_CONV = 5
SSD_CHUNK = 128
SSD_BC = SSD_GROUPS * SSD_STATE
SSD_CONV_DIM = SSD_INNER + 2 * SSD_BC
ROPE_FREQS = 32
ROPE_BASE = 10000.0

NA_HEAD_DIM = 64
NA_WIDTH = 512
NA_HEADS = 8
NA_ROWS = 8
NA_COLS = 16
NA_QROWS = 4

GM_WIDTH = 512
GM_GROUPS = 8
GM_CHUNK = 128

LANES = 128
HALO = 16
NEG = -1e30
LOG2E = math.log2(math.e)
VMEM_LIMIT = 56 * 1024 * 1024
TOKEN_TILE = 512
SSD_SEQS_PER_STEP = 4
MERGE_ROWS = 256


def _dot(a, b):
    return jnp.dot(a, b, preferred_element_type=F32)


def _dot_nt(a, b):
    return lax.dot_general(a, b, (((1,), (1,)), ((), ())), preferred_element_type=F32)


def _silu(x):
    return x / (1.0 + jnp.exp(-x))


def _sigmoid(x):
    return 1.0 / (1.0 + jnp.exp(-x))


def _gelu_tanh(x):
    return 0.5 * x * (1.0 + jnp.tanh(math.sqrt(2.0 / math.pi) * (x + 0.044715 * (x * x * x))))


def _softplus(x):
    return jnp.maximum(x, 0.0) + jnp.log(1.0 + jnp.exp(-jnp.abs(x)))


def _split_hi_lo(v):
    hi = v.astype(BF16)
    lo = (v - hi.astype(F32)).astype(BF16)
    return hi, lo


def _const_spec(shape):
    nd = len(shape)
    return pl.BlockSpec(shape, lambda *_: (0,) * nd, pipeline_mode=pl.Buffered(1))


def _params(*semantics):
    return pltpu.CompilerParams(dimension_semantics=semantics, vmem_limit_bytes=VMEM_LIMIT)


def _mod_kernel(c_ref, w_ref, b_ref, o_ref):
    o_ref[0] = _dot(_silu(c_ref[...]), w_ref[0]) + b_ref[0]


def _modulation(c_all, w_mod, b_mod):
    depth, d, n = w_mod.shape
    tn = 1536
    return pl.pallas_call(
        _mod_kernel,
        out_shape=jax.ShapeDtypeStruct((depth, 8, n), F32),
        grid=(depth, n // tn),
        in_specs=[pl.BlockSpec((8, d), lambda l, j: (0, 0)),
                  pl.BlockSpec((1, d, tn), lambda l, j: (l, 0, j)),
                  pl.BlockSpec((1, 1, tn), lambda l, j: (l, 0, j))],
        out_specs=pl.BlockSpec((1, 8, tn), lambda l, j: (l, 0, j)),
        compiler_params=_params("arbitrary", "arbitrary"),
        name="modulation",
    )(c_all, w_mod, b_mod.reshape(depth, 1, n))


def _inproj_kernel(x_ref, mod_ref, n1_ref, wz, wxbc, wdt, wq, wk, wv, wuv, wg, dtb, bg, qn, kn, blk,
                   gmn, ws, bsp, z_o, xbc_o, dt_o, q_o, k_o, v_o, ygm_o, gate_o):
    tm = x_ref.shape[1]
    x = x_ref[0]
    xn = x * lax.rsqrt(jnp.mean(x * x, axis=-1, keepdims=True) + EPS) * n1_ref[...]
    hb = (xn * (1.0 + mod_ref[0, 1:2, :]) + mod_ref[0, 0:1, :]).astype(BF16)

    cw = 512

    def head_norm(t, w_row):
        ms = _dot((t * t).astype(BF16), blk[...])
        return t * lax.rsqrt(ms + EPS) * w_row

    qf = _dot(hb, wq[...])
    kf = _dot(hb, wk[...])
    g = _gelu_tanh(_dot(hb, wuv[...]))
    dt_o[0] = _softplus(_dot(hb, wdt[...]) + dtb[...])
    v_o[0] = _dot(hb, wv[...]).astype(BF16)
    for n0 in range(0, z_o.shape[2], cw):
        z_o[0, :, n0:n0 + cw] = _dot(hb, wz[:, n0:n0 + cw]).astype(BF16)

    q_o[0] = (head_norm(qf, qn[...]) * (NA_HEAD_DIM ** -0.5 * LOG2E)).astype(BF16)
    k_o[0] = head_norm(kf, kn[...]).astype(BF16)
    for n0 in range(0, xbc_o.shape[2], cw):
        xbc_o[0, :, n0:n0 + cw] = _dot(hb, wxbc[:, n0:n0 + cw]).astype(BF16)

    u = g[:, :GM_WIDTH]
    v = g[:, GM_WIDTH:]
    vb = (v * lax.rsqrt(jnp.mean(v * v, axis=-1, keepdims=True) + EPS) * gmn[...]).astype(BF16)
    first = lax.broadcasted_iota(jnp.int32, (GM_CHUNK, LANES), 1) < LANES // 2
    for c0 in range(0, tm, GM_CHUNK):
        for p in range(GM_WIDTH // LANES):
            cols = slice(p * LANES, (p + 1) * LANES)
            vp = vb[c0:c0 + GM_CHUNK, cols]
            zero = jnp.zeros_like(vp)
            stacked = jnp.concatenate([jnp.where(first, vp, zero), jnp.where(first, zero, vp)], axis=0)
            mixed = _dot(ws[p], stacked) + bsp[:, cols]
            ygm_o[0, c0:c0 + GM_CHUNK, cols] = (u[c0:c0 + GM_CHUNK, cols] * mixed).astype(BF16)

    for n0 in range(0, gate_o.shape[2], cw):
        gate_o[0, :, n0:n0 + cw] = _sigmoid(_dot(hb, wg[:, n0:n0 + cw]) + bg[:, n0:n0 + cw]).astype(BF16)


def _input_projection(x, mod6, lw):
    b, t, d = x.shape
    tm = min(TOKEN_TILE, t)
    tok = lambda c: pl.BlockSpec((1, tm, c), lambda i, j: (i, j, 0))
    consts = [lw["norm1"], lw["w_z"], lw["w_xbc"], lw["w_dt"], lw["w_q"], lw["w_k"], lw["w_v"], lw["w_uv"],
              lw["w_gate"], lw["dt_bias"], lw["b_gate"], lw["q_norm"], lw["k_norm"], lw["head_blk"],
              lw["gm_norm"], lw["w_spatial"], lw["b_spatial"]]
    widths = [(SSD_INNER, BF16), (SSD_CONV_DIM, BF16), (LANES, F32), (NA_WIDTH, BF16), (NA_WIDTH, BF16),
              (NA_WIDTH, BF16), (GM_WIDTH, BF16), (3 * d, BF16)]
    return pl.pallas_call(
        _inproj_kernel,
        out_shape=[jax.ShapeDtypeStruct((b, t, c), ty) for c, ty in widths],
        grid=(b, t // tm),
        in_specs=[tok(d), pl.BlockSpec((1, 6, d), lambda i, j: (i, 0, 0))] + [_const_spec(a.shape) for a in consts],
        out_specs=[tok(c) for c, _ in widths],
        compiler_params=_params("parallel", "parallel"),
        name="input_projection",
    )(x, mod6, *consts)


def _ssd_chunks(seqs, a_row, e2_ref, tri2_ref, reverse, head_off):
    q = seqs[0][0].shape[0]
    last = 0 if reverse else q - 1
    gw = SSD_INNER // SSD_GROUPS
    hpg = gw // SSD_HEAD_DIM
    groups = range(SSD_GROUPS)
    gcols = lambda g: slice(g * gw, (g + 1) * gw)
    ncols = lambda g: slice(g * SSD_STATE, (g + 1) * SSD_STATE)
    tri2 = tri2_ref[...]

    acums = []
    for xs, bm, cm, dt, h_ref in seqs:
        a = dt * a_row
        a1 = a.astype(BF16)
        r1 = a - a1.astype(F32)
        a2 = r1.astype(BF16)
        a3 = (r1 - a2.astype(F32)).astype(BF16)
        acums.append(_dot(tri2, jnp.concatenate([a1, a2], axis=0)) + _dot(tri2[:, :q], a3))

    wides = []
    for (xs, bm, cm, dt, h_ref), acum in zip(seqs, acums):
        a_last = acum[last:last + 1, :]
        per_head = jnp.concatenate([dt, dt * jnp.exp2(a_last - acum),
                                    jnp.broadcast_to(jnp.exp2(a_last), (8, LANES))], axis=0)
        hi, lo = _split_hi_lo(per_head)
        wides.append(_dot(jnp.concatenate([hi, lo], axis=1), e2_ref[...]))

    cbs, y_inters, bts, h_prevs = [], [], [], []
    for xs, bm, cm, dt, h_ref in seqs:
        cgbs = [cm[:, ncols(g)].astype(BF16) for g in groups]
        h_prev = [h_ref[:, gcols(g)] for g in groups]
        cbs.append([_dot_nt(cgbs[g], bm[:, ncols(g)].astype(BF16)) for g in groups])
        y_inters.append([_dot(cgbs[g], h_prev[g].astype(BF16)) for g in groups])
        bts.append([bm[:, ncols(g)].astype(F32).T.astype(BF16) for g in groups])
        h_prevs.append(h_prev)

    row = lax.broadcasted_iota(jnp.int32, (q, q), 0)
    col = lax.broadcasted_iota(jnp.int32, (q, q), 1)
    in_scan = (col >= row) if reverse else (col <= row)
    lane = lax.broadcasted_iota(jnp.int32, (q, LANES), 1)
    lane_head = lax.broadcasted_iota(jnp.int32, (q, gw), 1) // SSD_HEAD_DIM

    lhs, rhs, dec_outs, xw_bs, h_decs = [], [], [], [], []
    for si, (xs, bm, cm, dt, h_ref) in enumerate(seqs):
        acum, wide = acums[si], wides[si]
        acum_t = acum.T
        xdt_b = (xs * wide[0:q]).astype(BF16)
        xw_bs.append((xs * wide[q:2 * q]).astype(BF16))
        h_decs.append(wide[2 * q:2 * q + 1])
        for g in groups:
            xg = xdt_b[:, gcols(g)]
            ms, decs = [], []
            for j in range(hpg):
                hc = head_off + g * hpg + j
                colx = jnp.broadcast_to(acum[:, hc:hc + 1], (q, q))
                decay = jnp.exp2(jnp.where(in_scan, colx - acum_t[hc:hc + 1, :], NEG))
                ms.append((cbs[si][g] * decay).astype(BF16))
                decs.append(jnp.exp2(colx))
            lhs.append(jnp.concatenate(ms, axis=1))
            rhs.append(jnp.concatenate([jnp.where(lane_head == j, xg, jnp.zeros_like(xg)) for j in range(hpg)], axis=0))
            dec_outs.append(jnp.concatenate([jnp.where(lane < SSD_HEAD_DIM, decs[2 * i], decs[2 * i + 1])
                                             for i in range(hpg // 2)], axis=1))
    y_intra = [_dot(l, r) for l, r in zip(lhs, rhs)]
    h_add = [_dot(bts[si][g], xw_bs[si][:, gcols(g)]) for si in range(len(seqs)) for g in groups]
    ys = []
    for si, (xs, bm, cm, dt, h_ref) in enumerate(seqs):
        for g in groups:
            h_ref[:, gcols(g)] = h_prevs[si][g] * h_decs[si][:, gcols(g)] + h_add[si * SSD_GROUPS + g]
        ys.append(jnp.concatenate([y_intra[si * SSD_GROUPS + g] + y_inters[si][g] * dec_outs[si * SSD_GROUPS + g]
                                   for g in groups], axis=1))
    return ys


def _ssd_fwd_kernel(cur_ref, prev_ref, next_ref, dt_ref, cos_ref, sin_ref, cw_ref, cb_ref, shift_ref, a_ref, e_ref,
                    tri_ref, dsk_ref, h0_ref, xbc_o, y_o, hn_o, h_ref):
    i = pl.program_id(1)
    nc = pl.num_programs(1)
    nb, q, _ = cur_ref.shape

    @pl.when(i == 0)
    def _():
        h_ref[...] = h0_ref[...]

    pv = jnp.where(i == 0, 0.0, 1.0).astype(BF16)
    nv = jnp.where(i == nc - 1, 0.0, 1.0).astype(BF16)
    lane = lax.broadcasted_iota(jnp.int32, (q, LANES), 1)
    cos = cos_ref[...]
    sin = sin_ref[...]

    def rope(t):
        sw = jnp.where((lane & ROPE_FREQS) == 0, pltpu.roll(t, LANES - ROPE_FREQS, 1), pltpu.roll(t, ROPE_FREQS, 1))
        return t * cos + sw * sin

    cwid = 256
    seqs = []
    for s in range(nb):
        ext = jnp.concatenate([prev_ref[s] * pv, cur_ref[s], next_ref[s] * nv], axis=0)
        parts = []
        for c0 in range(0, SSD_CONV_DIM, cwid):
            e = ext[:, c0:c0 + cwid]
            taps = jnp.concatenate([e * cw_ref[k:k + 1, c0:c0 + cwid] for k in range(SSD_CONV)], axis=0)
            parts.append(_silu(_dot(shift_ref[...], taps) + cb_ref[:, c0:c0 + cwid]))
        xs = jnp.concatenate(parts[:SSD_INNER // cwid], axis=1)
        bc = jnp.concatenate(parts[SSD_INNER // cwid:], axis=1)
        bc = jnp.concatenate([rope(bc[:, g * LANES:(g + 1) * LANES]) for g in range(2 * SSD_GROUPS)], axis=1)
        xbc_o[s, :, :SSD_INNER] = xs.astype(BF16)
        xbc_o[s, :, SSD_INNER:] = bc.astype(BF16)
        seqs.append((xs, bc[:, :SSD_BC], bc[:, SSD_BC:], dt_ref[s], h_ref.at[s]))
    ys = _ssd_chunks(seqs, a_ref[...], e_ref, tri_ref, False, 0)
    for s in range(nb):
        y_o[s] = (ys[s] + dsk_ref[...] * seqs[s][0]).astype(BF16)

    @pl.when(i == nc - 1)
    def _():
        hn_o[...] = h_ref[...]


def _ssd_bwd_kernel(xbc_ref, dt_ref, yf_ref, a_ref, e_ref, tri_ref, h0_ref, y_o, hn_o, h_ref):
    i = pl.program_id(1)

    @pl.when(i == 0)
    def _():
        h_ref[...] = h0_ref[...]

    nb = xbc_ref.shape[0]
    seqs = [(xbc_ref[s, :, :SSD_INNER].astype(F32), xbc_ref[s, :, SSD_INNER:SSD_INNER + SSD_BC],
             xbc_ref[s, :, SSD_INNER + SSD_BC:], dt_ref[s], h_ref.at[s]) for s in range(nb)]
    ys = _ssd_chunks(seqs, a_ref[...], e_ref, tri_ref, True, SSD_HEADS)
    for s in range(nb):
        y_o[s] = (ys[s] + yf_ref[s].astype(F32)).astype(BF16)

    @pl.when(i == pl.num_programs(1) - 1)
    def _():
        hn_o[...] = h_ref[...]


def _ssd(xbc, dt, h0_f, h0_b, rope_cos, rope_sin, lw, consts):
    b, t, _ = xbc.shape
    q = SSD_CHUNK
    nc = t // q
    nb = SSD_SEQS_PER_STEP if b % SSD_SEQS_PER_STEP == 0 else 1
    hb = q // HALO
    n_hb = t // HALO
    chunk = lambda c: pl.BlockSpec((nb, q, c), lambda i, j: (i, j, 0))
    state = pl.BlockSpec((nb, SSD_STATE, SSD_INNER), lambda i, j: (i, 0, 0))
    state_shape = jax.ShapeDtypeStruct((b, SSD_STATE, SSD_INNER), F32)
    scratch_h = pltpu.VMEM((nb, SSD_STATE, SSD_INNER), F32)

    xbc_c, y_f, hn_f = pl.pallas_call(
        _ssd_fwd_kernel,
        out_shape=[jax.ShapeDtypeStruct((b, t, SSD_CONV_DIM), BF16), jax.ShapeDtypeStruct((b, t, SSD_INNER), BF16),
                   state_shape],
        grid=(b // nb, nc),
        in_specs=[chunk(SSD_CONV_DIM),
                  pl.BlockSpec((nb, HALO, SSD_CONV_DIM), lambda i, j: (i, jnp.maximum(j * hb - 1, 0), 0)),
                  pl.BlockSpec((nb, HALO, SSD_CONV_DIM), lambda i, j: (i, jnp.minimum((j + 1) * hb, n_hb - 1), 0)),
                  chunk(LANES),
                  pl.BlockSpec((q, LANES), lambda i, j: (j, 0)),
                  pl.BlockSpec((q, LANES), lambda i, j: (j, 0)),
                  _const_spec(lw["conv_w"].shape), _const_spec(lw["conv_b"].shape),
                  _const_spec(consts["conv_shift"].shape),
                  _const_spec(lw["a_fwd"].shape), _const_spec(consts["e_fwd"].shape),
                  _const_spec(consts["tri_fwd"].shape), _const_spec(lw["d_skip"].shape), state],
        out_specs=[chunk(SSD_CONV_DIM), chunk(SSD_INNER), state],
        scratch_shapes=[scratch_h],
        compiler_params=_params("parallel", "arbitrary"),
        name="ssd_forward",
    )(xbc, xbc, xbc, dt, rope_cos, rope_sin, lw["conv_w"], lw["conv_b"], consts["conv_shift"],
      lw["a_fwd"], consts["e_fwd"], consts["tri_fwd"], lw["d_skip"], h0_f)

    rchunk = lambda c: pl.BlockSpec((nb, q, c), lambda i, j: (i, nc - 1 - j, 0))
    y, hn_b = pl.pallas_call(
        _ssd_bwd_kernel,
        out_shape=[jax.ShapeDtypeStruct((b, t, SSD_INNER), BF16), state_shape],
        grid=(b // nb, nc),
        in_specs=[rchunk(SSD_CONV_DIM), rchunk(LANES), rchunk(SSD_INNER),
                  _const_spec(lw["a_bwd"].shape), _const_spec(consts["e_bwd"].shape),
                  _const_spec(consts["tri_bwd"].shape), state],
        out_specs=[rchunk(SSD_INNER), state],
        scratch_shapes=[scratch_h],
        compiler_params=_params("parallel", "arbitrary"),
        name="ssd_backward",
    )(xbc_c, dt, y_f, lw["a_bwd"], consts["e_bwd"], consts["tri_bwd"], h0_b)
    return y, hn_f, hn_b


def _bias_kernel(rpb_ref, o_ref):
    lh = pl.program_id(0)
    n_ri = 2 * NA_ROWS - 1
    n_ci = 2 * NA_COLS - 1
    lane = lax.broadcasted_iota(jnp.int32, (GRID_W, LANES), 1)
    qc = lax.broadcasted_iota(jnp.int32, (GRID_W, LANES), 0)
    kc = lane % GRID_W
    cs = jnp.clip(qc - NA_COLS // 2, 0, GRID_W - NA_COLS)
    col_ok = (kc >= cs) & (kc < cs + NA_COLS)
    ci = jnp.clip(kc - qc + (NA_COLS - 1), 0, n_ci - 1)
    tiles = []
    for ri in range(n_ri):
        base = (lh * n_ri + ri) * n_ci
        acc = jnp.full((GRID_W, LANES), NEG, F32)
        for c in range(n_ci):
            acc = jnp.where(col_ok & (ci == c), rpb_ref[base + c] * LOG2E, acc)
        tiles.append(acc)
    masked = jnp.full((GRID_W, LANES), NEG, F32)
    tiles = [masked] + tiles + [masked]
    for e in range(n_ri + 1):
        o_ref[0, e] = jnp.where(lane < GRID_W, tiles[e], tiles[e + 1])


def _bias_table(rpb):
    depth, heads, n_ri, n_ci = rpb.shape
    return pl.pallas_call(
        _bias_kernel,
        out_shape=jax.ShapeDtypeStruct((depth * heads, n_ri + 1, GRID_W, LANES), F32),
        grid=(depth * heads,),
        in_specs=[pl.BlockSpec(memory_space=pltpu.SMEM)],
        out_specs=pl.BlockSpec((1, n_ri + 1, GRID_W, LANES), lambda i: (i, 0, 0, 0)),
        compiler_params=pltpu.CompilerParams(dimension_semantics=("arbitrary",)),
        name="na_bias_table",
    )(rpb.reshape(-1))


def _na_tile(q_ref, k_ref, v_ref, kc_ref, vc_ref, bias_ref, o_ref, start, nw, plan):
    tq = q_ref.shape[1]
    n_ctx = kc_ref.shape[1]
    n_cb = n_ctx // LANES
    lane_q = lax.broadcasted_iota(jnp.int32, (tq, LANES), 1)
    lane_k = lax.broadcasted_iota(jnp.int32, (n_ctx + nw * GRID_W, LANES), 1)
    n_pairs = NA_WIDTH // LANES
    own = [lambda lane, hh=hh: (lane < NA_HEAD_DIM) == (hh == 0) for hh in range(2)]
    col = lambda p: slice(p * LANES, (p + 1) * LANES)
    scores_all = []
    for p in range(n_pairs):
        qp = q_ref[0, :, col(p)]
        keys = jnp.concatenate([kc_ref[0, :, col(p)], k_ref[0, pl.ds(start, nw * GRID_W), col(p)]], axis=0)
        for hh in range(2):
            scores_all.append(_dot_nt(jnp.where(own[hh](lane_q), qp, jnp.zeros_like(qp)), keys))
    p_mats = []
    for h, scores in enumerate(scores_all):
        p_rows = []
        for qi, row_plan in enumerate(plan):
            rows = slice(qi * GRID_W, (qi + 1) * GRID_W)
            blocks = [scores[rows, m * LANES:(m + 1) * LANES] for m in range(n_cb)]
            for m, (ent, ok) in enumerate(row_plan):
                if ok is None:
                    blocks.append(None)
                    continue
                sb = scores[rows, (n_cb + m) * LANES:(n_cb + m + 1) * LANES] + bias_ref[h, ent]
                blocks.append(sb if ok is True else jnp.where(ok, sb, NEG))
            live = [sb for sb in blocks if sb is not None]
            mx = live[0]
            for sb in live[1:]:
                mx = jnp.maximum(mx, sb)
            mx = jnp.max(mx, axis=-1, keepdims=True)
            p_rows.append(jnp.concatenate(
                [jnp.zeros((GRID_W, LANES), BF16) if sb is None else jnp.exp2(sb - mx).astype(BF16)
                 for sb in blocks], axis=1))
        p_mats.append(jnp.concatenate(p_rows, axis=0))
    for p in range(n_pairs):
        vals = jnp.concatenate([vc_ref[0, :, col(p)], v_ref[0, pl.ds(start, nw * GRID_W), col(p)]], axis=0)
        nums = [_dot(p_mats[2 * p + hh], jnp.where(own[hh](lane_k), vals, jnp.ones_like(vals))) for hh in range(2)]
        num = jnp.where(lane_q < NA_HEAD_DIM, nums[0], nums[1])
        den = pltpu.roll(jnp.where(lane_q < NA_HEAD_DIM, nums[1], nums[0]), NA_HEAD_DIM, 1)
        o_ref[0, :, col(p)] = (num / den).astype(BF16)


def _na_kernel(q_ref, k_ref, v_ref, kc_ref, vc_ref, bias_ref, o_ref, *, nw, rows_n):
    rt = NA_QROWS
    half = NA_ROWS // 2
    r0 = pl.program_id(1) * rt
    lane_r = lax.broadcasted_iota(jnp.int32, (GRID_W, LANES), 1)
    first_half = lane_r < GRID_W
    nblk = nw // 2
    interior = (r0 >= half) & (r0 - half <= rows_n - nw)
    args = (q_ref, k_ref, v_ref, kc_ref, vc_ref, bias_ref, o_ref)

    @pl.when(interior)
    def _():
        plan = []
        for qi in range(rt):
            row_plan = []
            for m in range(nblk):
                ok0 = qi <= 2 * m < qi + NA_ROWS
                ok1 = qi <= 2 * m + 1 < qi + NA_ROWS
                ok = True if ok0 and ok1 else None if not (ok0 or ok1) else first_half if ok0 else ~first_half
                row_plan.append((half + 2 * m - qi, ok))
            plan.append(row_plan)
        _na_tile(*args, pl.multiple_of((r0 - half) * GRID_W, GRID_W), nw, plan)

    @pl.when(~interior)
    def _():
        base = jnp.clip(r0 - half, 0, rows_n - nw)
        plan = []
        for qi in range(rt):
            r = r0 + qi
            rs = jnp.clip(r - half, 0, rows_n - NA_ROWS)
            row_plan = []
            for m in range(nblk):
                j0 = base + 2 * m
                ok0 = (j0 >= rs) & (j0 < rs + NA_ROWS)
                ok1 = (j0 + 1 >= rs) & (j0 + 1 < rs + NA_ROWS)
                ok = jnp.where(first_half, ok0.astype(jnp.int32), ok1.astype(jnp.int32)) > 0
                row_plan.append((jnp.clip(j0 - r + NA_ROWS, 0, 2 * NA_ROWS - 1), ok))
            plan.append(row_plan)
        _na_tile(*args, pl.multiple_of(base * GRID_W, GRID_W), nw, plan)


def _neighbourhood_attention(q, k, v, kc, vc, bias):
    b, t, w = q.shape
    n_ctx = kc.shape[1]
    rows_n = t // GRID_W
    nw = NA_QROWS + NA_ROWS
    nw += nw % 2
    tq = NA_QROWS * GRID_W
    whole = lambda n: pl.BlockSpec((1, n, w), lambda i, j: (i, 0, 0))
    return pl.pallas_call(
        functools.partial(_na_kernel, nw=nw, rows_n=rows_n),
        out_shape=jax.ShapeDtypeStruct((b, t, w), BF16),
        grid=(b, t // tq),
        in_specs=[pl.BlockSpec((1, tq, w), lambda i, j: (i, j, 0)), whole(t), whole(t), whole(n_ctx), whole(n_ctx),
                  _const_spec(bias.shape)],
        out_specs=pl.BlockSpec((1, tq, w), lambda i, j: (i, j, 0)),
        compiler_params=_params("parallel", "arbitrary"),
        name="neighbourhood_attention",
    )(q, k, v, kc, vc, bias)


def _ctx_attn_kernel(q_ref, k_ref, v_ref, o_ref):
    n = q_ref.shape[1]
    lane = lax.broadcasted_iota(jnp.int32, (n, LANES), 1)
    for p in range(NA_WIDTH // LANES):
        cols = slice(p * LANES, (p + 1) * LANES)
        qp = q_ref[0, :, cols]
        kp = k_ref[0, :, cols]
        vp = v_ref[0, :, cols]
        nums = []
        for hh in range(2):
            own = (lane < NA_HEAD_DIM) == (hh == 0)
            s = _dot_nt(jnp.where(own, qp, jnp.zeros_like(qp)), kp)
            pm = jnp.exp2(s - jnp.max(s, axis=-1, keepdims=True)).astype(BF16)
            nums.append(_dot(pm, jnp.where(own, vp, jnp.ones_like(vp))))
        num = jnp.where(lane < NA_HEAD_DIM, nums[0], nums[1])
        den = pltpu.roll(jnp.where(lane < NA_HEAD_DIM, nums[1], nums[0]), NA_HEAD_DIM, 1)
        o_ref[0, :, cols] = (num / den).astype(BF16)


def _context_attention(q, k, v):
    b, n, w = q.shape
    spec = pl.BlockSpec((1, n, w), lambda i: (i, 0, 0))
    return pl.pallas_call(
        _ctx_attn_kernel,
        out_shape=jax.ShapeDtypeStruct((b, n, w), BF16),
        grid=(b,),
        in_specs=[spec, spec, spec],
        out_specs=spec,
        compiler_params=_params("parallel"),
        name="context_attention",
    )(q, k, v)


def _merge_ffn_kernel(x_ref, mod_ref, gate_ref, yssd_ref, z_ref, yna_ref, ygm_ref, sn_ref, wa, wb, wc, wo,
                      n2_ref, wfi, wfo, o_ref, *, ffn_chunk):
    tm, d = x_ref.shape[1], x_ref.shape[2]
    hid = wfo.shape[0]
    parts = [slice(r0, r0 + MERGE_ROWS) for r0 in range(0, tm, MERGE_ROWS)]

    side = [gate_ref[0, r, d:2 * d].astype(F32) * _dot(yna_ref[0, r, :], wb[...])
            + gate_ref[0, r, 2 * d:3 * d].astype(F32) * _dot(ygm_ref[0, r, :], wc[...]) for r in parts]
    ys = []
    for r in parts:
        y = yssd_ref[0, r, :].astype(F32) * _silu(z_ref[0, r, :].astype(F32))
        ys.append((y * lax.rsqrt(jnp.mean(y * y, axis=-1, keepdims=True) + EPS) * sn_ref[...]).astype(BF16))
    mixed = [(sd + gate_ref[0, r, 0:d].astype(F32) * _dot(y, wa[...])).astype(BF16)
             for r, y, sd in zip(parts, ys, side)]
    x1s = [x_ref[0, r, :] + mod_ref[0, 2:3, :] * _dot(m, wo[...]) for r, m in zip(parts, mixed)]
    hbs = []
    for x1 in x1s:
        xn = x1 * lax.rsqrt(jnp.mean(x1 * x1, axis=-1, keepdims=True) + EPS) * n2_ref[...]
        hbs.append((xn * (1.0 + mod_ref[0, 4:5, :]) + mod_ref[0, 3:4, :]).astype(BF16))
    hb = jnp.concatenate(hbs, axis=0)
    x1 = jnp.concatenate(x1s, axis=0)
    ts = []
    for c0 in range(0, hid, ffn_chunk):
        a = _dot(hb, wfi[:, c0:c0 + ffn_chunk])
        g = _dot(hb, wfi[:, hid + c0:hid + c0 + ffn_chunk])
        ts.append((_silu(a) * g).astype(BF16))
    o_ref[0] = x1 + mod_ref[0, 5:6, :] * _dot(jnp.concatenate(ts, axis=1), wfo[...])


def _merge_ffn(x, mod6, gate, y_ssd, z, y_na, y_gm, lw):
    b, t, d = x.shape
    tm = min(TOKEN_TILE, t)
    tok = lambda c: pl.BlockSpec((1, tm, c), lambda i, j: (i, j, 0))
    consts = [lw["ssd_norm"], lw["w_branch_ssd"], lw["w_branch_na"], lw["w_branch_gm"], lw["w_out"], lw["norm2"],
              lw["w_ffn_in"], lw["w_ffn_out"]]
    return pl.pallas_call(
        functools.partial(_merge_ffn_kernel, ffn_chunk=256),
        out_shape=jax.ShapeDtypeStruct((b, t, d), F32),
        grid=(b, t // tm),
        in_specs=[tok(d), pl.BlockSpec((1, 6, d), lambda i, j: (i, 0, 0)),
                  tok(3 * d), tok(SSD_INNER), tok(SSD_INNER), tok(NA_WIDTH), tok(GM_WIDTH)]
                 + [_const_spec(a.shape) for a in consts],
        out_specs=tok(d),
        compiler_params=_params("parallel", "parallel"),
        name="merge_out_ffn",
    )(x, mod6, gate, y_ssd, z, y_na, y_gm, *consts)


def _shared_constants(n_ctx, seq):
    pos = jnp.arange(seq)
    freqs = ROPE_BASE ** (-jnp.arange(ROPE_FREQS, dtype=F32) / ROPE_FREQS)
    ang_row = (pos // GRID_W).astype(F32)[:, None] * freqs
    ang_col = (pos % GRID_W).astype(F32)[:, None] * freqs
    cos = jnp.concatenate([jnp.cos(ang_row), jnp.cos(ang_row), jnp.cos(ang_col), jnp.cos(ang_col)], axis=1)
    sin = jnp.concatenate([-jnp.sin(ang_row), jnp.sin(ang_row), -jnp.sin(ang_col), jnp.sin(ang_col)], axis=1)

    r = jnp.arange(SSD_CHUNK)
    twice = lambda m, axis: jnp.concatenate([m, m], axis=axis).astype(BF16)
    lane_head = jnp.arange(SSD_INNER) // SSD_HEAD_DIM
    rows = jnp.arange(LANES)
    src = jnp.arange(SSD_CHUNK + 2 * HALO)
    conv_shift = jnp.concatenate([(src[None, :] == r[:, None] + HALO + k - SSD_CONV // 2) for k in range(SSD_CONV)],
                                 axis=1).astype(BF16)
    return dict(rope_cos=cos, rope_sin=sin,
                ctx_cos=jnp.ones((n_ctx, LANES), F32), ctx_sin=jnp.zeros((n_ctx, LANES), F32),
                tri_fwd=twice(r[None, :] <= r[:, None], 1), tri_bwd=twice(r[None, :] >= r[:, None], 1),
                e_fwd=twice(rows[:, None] == lane_head[None, :], 0),
                e_bwd=twice(rows[:, None] == lane_head[None, :] + SSD_HEADS, 0),
                conv_shift=conv_shift)


def _layer_weights(l, p):
    d = p["w_in"].shape[1]
    sizes = (SSD_INNER, SSD_CONV_DIM, 2 * SSD_HEADS, NA_WIDTH, NA_WIDTH, NA_WIDTH, 2 * GM_WIDTH, 3 * d)
    names = ("w_z", "w_xbc", "w_dt", "w_q", "w_k", "w_v", "w_uv", "w_gate")
    lw, start = {}, 0
    w_in = p["w_in"][l]
    for name, size in zip(names, sizes):
        lw[name] = w_in[:, start:start + size].astype(BF16)
        start += size
    pad_lanes = lambda v: jnp.pad(v, (0, LANES - v.shape[0])).reshape(1, LANES)
    lw["w_dt"] = jnp.pad(lw["w_dt"], ((0, 0), (0, LANES - 2 * SSD_HEADS)))
    lw["dt_bias"] = pad_lanes(p["dt_bias"][l].reshape(-1))
    a = -jnp.exp(p["a_log"][l].astype(F32))
    lw["a_fwd"] = pad_lanes(a[0] * LOG2E)
    lw["a_bwd"] = pad_lanes(jnp.concatenate([jnp.zeros((SSD_HEADS,), F32), a[1] * LOG2E]))
    row = lambda v: v.reshape(1, -1)
    lw["norm1"] = row(p["norm1"][l])
    lw["norm2"] = row(p["norm2"][l])
    lw["b_gate"] = row(p["b_gate"][l])
    lw["q_norm"] = row(jnp.tile(p["q_norm"][l], NA_HEADS))
    lw["k_norm"] = row(jnp.tile(p["k_norm"][l], NA_HEADS))
    head = jnp.arange(NA_WIDTH) // NA_HEAD_DIM
    lw["head_blk"] = ((head[:, None] == head[None, :]).astype(F32) / NA_HEAD_DIM).astype(BF16)
    lw["gm_norm"] = row(p["gm_norm"][l])
    w_s = p["w_spatial"][l].astype(BF16)
    lw["w_spatial"] = jnp.concatenate([w_s[0::2], w_s[1::2]], axis=2)
    lw["b_spatial"] = jnp.repeat(p["b_spatial"][l].T, GM_WIDTH // GM_GROUPS, axis=1)
    lw["conv_w"] = jnp.pad(p["conv_w"][l], ((0, 8 - SSD_CONV), (0, 0))).astype(BF16)
    lw["conv_b"] = row(p["conv_b"][l])
    lw["d_skip"] = row(jnp.repeat(p["d_skip"][l], SSD_HEAD_DIM))
    lw["ssd_norm"] = row(p["ssd_norm"][l])
    for name in ("w_branch_ssd", "w_branch_na", "w_branch_gm", "w_out", "w_ffn_in", "w_ffn_out"):
        lw[name] = p[name][l].astype(BF16)
    return lw


def kernel(x, c, ctx, c_ctx, w_mod, b_mod, norm1, w_in, b_gate, conv_w, conv_b, a_log, dt_bias, d_skip, ssd_norm,
           q_norm, k_norm, rpb, gm_norm, w_spatial, b_spatial, w_branch_ssd, w_branch_na, w_branch_gm, w_out,
           norm2, w_ffn_in, w_ffn_out):
    p = dict(norm1=norm1, w_in=w_in, b_gate=b_gate, conv_w=conv_w, conv_b=conv_b, a_log=a_log, dt_bias=dt_bias,
             d_skip=d_skip, ssd_norm=ssd_norm, q_norm=q_norm, k_norm=k_norm, gm_norm=gm_norm, w_spatial=w_spatial,
             b_spatial=b_spatial, w_branch_ssd=w_branch_ssd, w_branch_na=w_branch_na, w_branch_gm=w_branch_gm,
             w_out=w_out, norm2=norm2, w_ffn_in=w_ffn_in, w_ffn_out=w_ffn_out)
    b, seq, d = x.shape
    n_ctx = ctx.shape[1]
    depth = w_mod.shape[0]

    c_all = jnp.zeros((8, d), F32).at[:b].set(c).at[b].set(c_ctx)
    mod = _modulation(c_all, w_mod, b_mod)
    bias = _bias_table(rpb)
    consts = _shared_constants(n_ctx, seq)
    zero_state = jnp.zeros((b, SSD_STATE, SSD_INNER), F32)

    xc = ctx
    for l in range(depth):
        lw = _layer_weights(l, p)
        mod_x = mod[l, :b].reshape(b, 6, d)
        mod_c = mod[l, b].reshape(1, 6, d)
        last = l == depth - 1

        flat = lambda t: t.reshape(1, b * n_ctx, t.shape[-1])
        per_sample = lambda t: t.reshape(b, n_ctx, t.shape[-1])
        zc, xbcc, dtc, qc, kc, vc, ygm_c, gate_c = [per_sample(t) for t in _input_projection(flat(xc), mod_c, lw)]
        z, xbc, dt, q, k, v, y_gm, gate = _input_projection(x, mod_x, lw)

        yssd_c, s_f, s_b = _ssd(xbcc, dtc, zero_state, zero_state, consts["ctx_cos"], consts["ctx_sin"], lw, consts)
        y_ssd, _, _ = _ssd(xbc, dt, s_f, s_b, consts["rope_cos"], consts["rope_sin"], lw, consts)

        y_na = _neighbourhood_attention(q, k, v, kc, vc, bias[l * NA_HEADS:(l + 1) * NA_HEADS])
        x = _merge_ffn(x, mod_x, gate, y_ssd, z, y_na, y_gm, lw)
        if not last:
            yna_c = _context_attention(qc, kc, vc)
            xc = per_sample(_merge_ffn(flat(xc), mod_c, flat(gate_c), flat(yssd_c), flat(zc), flat(yna_c), flat(ygm_c), lw))
    return x
```

```python
import functools
import math

import jax
import jax.numpy as jnp
from jax import lax
from jax.experimental import pallas as pl
from jax.experimental.pallas import tpu as pltpu

F32 = jnp.float32
BF16 = jnp.bfloat16

EPS = 1e-6
GRID_W = 64

SSD_INNER = 1024
SSD_HEAD_DIM = 64
SSD_HEADS = 16
SSD_GROUPS = 4
SSD_STATE = 128
SSD_CONV = 5
SSD_CHUNK = 128
SSD_BC = SSD_GROUPS * SSD_STATE
SSD_CONV_DIM = SSD_INNER + 2 * SSD_BC
ROPE_FREQS = 32
ROPE_BASE = 10000.0

NA_HEAD_DIM = 64
NA_WIDTH = 512
NA_HEADS = 8
NA_ROWS = 8
NA_COLS = 16
NA_QROWS = 4

GM_WIDTH = 512
GM_GROUPS = 8
GM_CHUNK = 128

LANES = 128
HALO = 16
NEG = -1e30
LOG2E = math.log2(math.e)
VMEM_LIMIT = 56 * 1024 * 1024
TOKEN_TILE = 512
SSD_SEQS_PER_STEP = 4


def _dot(a, b):
    return jnp.dot(a, b, preferred_element_type=F32)


def _dot_nt(a, b):
    return lax.dot_general(a, b, (((1,), (1,)), ((), ())), preferred_element_type=F32)


def _silu(x):
    return x / (1.0 + jnp.exp(-x))


def _sigmoid(x):
    return 1.0 / (1.0 + jnp.exp(-x))


def _gelu_tanh(x):
    return 0.5 * x * (1.0 + jnp.tanh(math.sqrt(2.0 / math.pi) * (x + 0.044715 * (x * x * x))))


def _softplus(x):
    return jnp.maximum(x, 0.0) + jnp.log(1.0 + jnp.exp(-jnp.abs(x)))


def _split_hi_lo(v):
    hi = v.astype(BF16)
    lo = (v - hi.astype(F32)).astype(BF16)
    return hi, lo


def _const_spec(shape):
    nd = len(shape)
    return pl.BlockSpec(shape, lambda *_: (0,) * nd, pipeline_mode=pl.Buffered(1))


def _params(*semantics):
    return pltpu.CompilerParams(dimension_semantics=semantics, vmem_limit_bytes=VMEM_LIMIT)


def _mod_kernel(c_ref, w_ref, b_ref, o_ref):
    o_ref[0] = _dot(_silu(c_ref[...]), w_ref[0]) + b_ref[0]


def _modulation(c_all, w_mod, b_mod):
    depth, d, n = w_mod.shape
    tn = 1536
    return pl.pallas_call(
        _mod_kernel,
        out_shape=jax.ShapeDtypeStruct((depth, 8, n), F32),
        grid=(depth, n // tn),
        in_specs=[pl.BlockSpec((8, d), lambda l, j: (0, 0)),
                  pl.BlockSpec((1, d, tn), lambda l, j: (l, 0, j)),
                  pl.BlockSpec((1, 1, tn), lambda l, j: (l, 0, j))],
        out_specs=pl.BlockSpec((1, 8, tn), lambda l, j: (l, 0, j)),
        compiler_params=_params("arbitrary", "arbitrary"),
        name="modulation",
    )(c_all, w_mod, b_mod.reshape(depth, 1, n))


def _inproj_kernel(x_ref, mod_ref, n1_ref, wz, wxbc, wdt, wq, wk, wv, wuv, wg, dtb, bg, qn, kn, blk,
                   gmn, ws, bsp, z_o, xbc_o, dt_o, q_o, k_o, v_o, ygm_o, gate_o):
    tm = x_ref.shape[1]
    x = x_ref[0]
    xn = x * lax.rsqrt(jnp.mean(x * x, axis=-1, keepdims=True) + EPS) * n1_ref[...]
    hb = (xn * (1.0 + mod_ref[0, 1:2, :]) + mod_ref[0, 0:1, :]).astype(BF16)

    cw = 512

    def head_norm(t, w_row):
        ms = _dot((t * t).astype(BF16), blk[...])
        return t * lax.rsqrt(ms + EPS) * w_row

    qf = _dot(hb, wq[...])
    kf = _dot(hb, wk[...])
    g = _gelu_tanh(_dot(hb, wuv[...]))
    dt_o[0] = _softplus(_dot(hb, wdt[...]) + dtb[...])
    v_o[0] = _dot(hb, wv[...]).astype(BF16)
    for n0 in range(0, z_o.shape[2], cw):
        z_o[0, :, n0:n0 + cw] = _dot(hb, wz[:, n0:n0 + cw]).astype(BF16)

    q_o[0] = (head_norm(qf, qn[...]) * (NA_HEAD_DIM ** -0.5 * LOG2E)).astype(BF16)
    k_o[0] = head_norm(kf, kn[...]).astype(BF16)
    for n0 in range(0, xbc_o.shape[2], cw):
        xbc_o[0, :, n0:n0 + cw] = _dot(hb, wxbc[:, n0:n0 + cw]).astype(BF16)

    u = g[:, :GM_WIDTH]
    v = g[:, GM_WIDTH:]
    vb = (v * lax.rsqrt(jnp.mean(v * v, axis=-1, keepdims=True) + EPS) * gmn[...]).astype(BF16)
    first = lax.broadcasted_iota(jnp.int32, (GM_CHUNK, LANES), 1) < LANES // 2
    for c0 in range(0, tm, GM_CHUNK):
        for p in range(GM_WIDTH // LANES):
            cols = slice(p * LANES, (p + 1) * LANES)
            vp = vb[c0:c0 + GM_CHUNK, cols]
            zero = jnp.zeros_like(vp)
            stacked = jnp.concatenate([jnp.where(first, vp, zero), jnp.where(first, zero, vp)], axis=0)
            mixed = _dot(ws[p], stacked) + bsp[:, cols]
            ygm_o[0, c0:c0 + GM_CHUNK, cols] = (u[c0:c0 + GM_CHUNK, cols] * mixed).astype(BF16)

    for n0 in range(0, gate_o.shape[2], cw):
        gate_o[0, :, n0:n0 + cw] = _sigmoid(_dot(hb, wg[:, n0:n0 + cw]) + bg[:, n0:n0 + cw]).astype(BF16)


def _input_projection(x, mod6, lw):
    b, t, d = x.shape
    tm = min(TOKEN_TILE, t)
    tok = lambda c: pl.BlockSpec((1, tm, c), lambda i, j: (i, j, 0))
    consts = [lw["norm1"], lw["w_z"], lw["w_xbc"], lw["w_dt"], lw["w_q"], lw["w_k"], lw["w_v"], lw["w_uv"],
              lw["w_gate"], lw["dt_bias"], lw["b_gate"], lw["q_norm"], lw["k_norm"], lw["head_blk"],
              lw["gm_norm"], lw["w_spatial"], lw["b_spatial"]]
    widths = [(SSD_INNER, BF16), (SSD_CONV_DIM, BF16), (LANES, F32), (NA_WIDTH, BF16), (NA_WIDTH, BF16),
              (NA_WIDTH, BF16), (GM_WIDTH, BF16), (3 * d, BF16)]
    return pl.pallas_call(
        _inproj_kernel,
        out_shape=[jax.ShapeDtypeStruct((b, t, c), ty) for c, ty in widths],
        grid=(b, t // tm),
        in_specs=[tok(d), pl.BlockSpec((1, 6, d), lambda i, j: (i, 0, 0))] + [_const_spec(a.shape) for a in consts],
        out_specs=[tok(c) for c, _ in widths],
        compiler_params=_params("parallel", "parallel"),
        name="input_projection",
    )(x, mod6, *consts)


def _ssd_chunks(seqs, a_row, e2_ref, tri2_ref, reverse, head_off):
    q = seqs[0][0].shape[0]
    last = 0 if reverse else q - 1
    gw = SSD_INNER // SSD_GROUPS
    hpg = gw // SSD_HEAD_DIM
    groups = range(SSD_GROUPS)
    gcols = lambda g: slice(g * gw, (g + 1) * gw)
    ncols = lambda g: slice(g * SSD_STATE, (g + 1) * SSD_STATE)
    tri2 = tri2_ref[...]

    acums = []
    for xs, bm, cm, dt, h_ref in seqs:
        a = dt * a_row
        a1 = a.astype(BF16)
        r1 = a - a1.astype(F32)
        a2 = r1.astype(BF16)
        a3 = (r1 - a2.astype(F32)).astype(BF16)
        acums.append(_dot(tri2, jnp.concatenate([a1, a2], axis=0)) + _dot(tri2[:, :q], a3))

    wides = []
    for (xs, bm, cm, dt, h_ref), acum in zip(seqs, acums):
        a_last = acum[last:last + 1, :]
        per_head = jnp.concatenate([dt, dt * jnp.exp2(a_last - acum),
                                    jnp.broadcast_to(jnp.exp2(a_last), (8, LANES))], axis=0)
        hi, lo = _split_hi_lo(per_head)
        wides.append(_dot(jnp.concatenate([hi, lo], axis=1), e2_ref[...]))

    cbs, y_inters, bts, h_prevs = [], [], [], []
    for xs, bm, cm, dt, h_ref in seqs:
        cgbs = [cm[:, ncols(g)].astype(BF16) for g in groups]
        h_prev = [h_ref[:, gcols(g)] for g in groups]
        cbs.append([_dot_nt(cgbs[g], bm[:, ncols(g)].astype(BF16)) for g in groups])
        y_inters.append([_dot(cgbs[g], h_prev[g].astype(BF16)) for g in groups])
        bts.append([bm[:, ncols(g)].astype(F32).T.astype(BF16) for g in groups])
        h_prevs.append(h_prev)

    row = lax.broadcasted_iota(jnp.int32, (q, q), 0)
    col = lax.broadcasted_iota(jnp.int32, (q, q), 1)
    in_scan = (col >= row) if reverse else (col <= row)
    lane = lax.broadcasted_iota(jnp.int32, (q, LANES), 1)
    lane_head = lax.broadcasted_iota(jnp.int32, (q, gw), 1) // SSD_HEAD_DIM

    lhs, rhs, dec_outs, xw_bs, h_decs = [], [], [], [], []
    for si, (xs, bm, cm, dt, h_ref) in enumerate(seqs):
        acum, wide = acums[si], wides[si]
        acum_t = acum.T
        xdt_b = (xs * wide[0:q]).astype(BF16)
        xw_bs.append((xs * wide[q:2 * q]).astype(BF16))
        h_decs.append(wide[2 * q:2 * q + 1])
        for g in groups:
            xg = xdt_b[:, gcols(g)]
            ms, decs = [], []
            for j in range(hpg):
                hc = head_off + g * hpg + j
                colx = jnp.broadcast_to(acum[:, hc:hc + 1], (q, q))
                decay = jnp.exp2(jnp.where(in_scan, colx - acum_t[hc:hc + 1, :], NEG))
                ms.append((cbs[si][g] * decay).astype(BF16))
                decs.append(jnp.exp2(colx))
            lhs.append(jnp.concatenate(ms, axis=1))
            rhs.append(jnp.concatenate([jnp.where(lane_head == j, xg, jnp.zeros_like(xg)) for j in range(hpg)], axis=0))
            dec_outs.append(jnp.concatenate([jnp.where(lane < SSD_HEAD_DIM, decs[2 * i], decs[2 * i + 1])
                                             for i in range(hpg // 2)], axis=1))
    y_intra = [_dot(l, r) for l, r in zip(lhs, rhs)]
    h_add = [_dot(bts[si][g], xw_bs[si][:, gcols(g)]) for si in range(len(seqs)) for g in groups]
    ys = []
    for si, (xs, bm, cm, dt, h_ref) in enumerate(seqs):
        for g in groups:
            h_ref[:, gcols(g)] = h_prevs[si][g] * h_decs[si][:, gcols(g)] + h_add[si * SSD_GROUPS + g]
        ys.append(jnp.concatenate([y_intra[si * SSD_GROUPS + g] + y_inters[si][g] * dec_outs[si * SSD_GROUPS + g]
                                   for g in groups], axis=1))
    return ys


def _ssd_fwd_kernel(cur_ref, prev_ref, next_ref, dt_ref, cos_ref, sin_ref, cw_ref, cb_ref, shift_ref, a_ref, e_ref,
                    tri_ref, dsk_ref, h0_ref, xbc_o, y_o, hn_o, h_ref):
    i = pl.program_id(1)
    nc = pl.num_programs(1)
    nb, q, _ = cur_ref.shape

    @pl.when(i == 0)
    def _():
        h_ref[...] = h0_ref[...]

    pv = jnp.where(i == 0, 0.0, 1.0).astype(BF16)
    nv = jnp.where(i == nc - 1, 0.0, 1.0).astype(BF16)
    lane = lax.broadcasted_iota(jnp.int32, (q, LANES), 1)
    cos = cos_ref[...]
    sin = sin_ref[...]

    def rope(t):
        sw = jnp.where((lane & ROPE_FREQS) == 0, pltpu.roll(t, LANES - ROPE_FREQS, 1), pltpu.roll(t, ROPE_FREQS, 1))
        return t * cos + sw * sin

    cwid = 256
    seqs = []
    for s in range(nb):
        ext = jnp.concatenate([prev_ref[s] * pv, cur_ref[s], next_ref[s] * nv], axis=0)
        parts = []
        for c0 in range(0, SSD_CONV_DIM, cwid):
            e = ext[:, c0:c0 + cwid]
            taps = jnp.concatenate([e * cw_ref[k:k + 1, c0:c0 + cwid] for k in range(SSD_CONV)], axis=0)
            parts.append(_silu(_dot(shift_ref[...], taps) + cb_ref[:, c0:c0 + cwid]))
        xs = jnp.concatenate(parts[:SSD_INNER // cwid], axis=1)
        bc = jnp.concatenate(parts[SSD_INNER // cwid:], axis=1)
        bc = jnp.concatenate([rope(bc[:, g * LANES:(g + 1) * LANES]) for g in range(2 * SSD_GROUPS)], axis=1)
        xbc_o[s, :, :SSD_INNER] = xs.astype(BF16)
        xbc_o[s, :, SSD_INNER:] = bc.astype(BF16)
        seqs.append((xs, bc[:, :SSD_BC], bc[:, SSD_BC:], dt_ref[s], h_ref.at[s]))
    ys = _ssd_chunks(seqs, a_ref[...], e_ref, tri_ref, False, 0)
    for s in range(nb):
        y_o[s] = (ys[s] + dsk_ref[...] * seqs[s][0]).astype(BF16)

    @pl.when(i == nc - 1)
    def _():
        hn_o[...] = h_ref[...]


def _ssd_bwd_kernel(xbc_ref, dt_ref, yf_ref, a_ref, e_ref, tri_ref, h0_ref, y_o, hn_o, h_ref):
    i = pl.program_id(1)

    @pl.when(i == 0)
    def _():
        h_ref[...] = h0_ref[...]

    nb = xbc_ref.shape[0]
    seqs = [(xbc_ref[s, :, :SSD_INNER].astype(F32), xbc_ref[s, :, SSD_INNER:SSD_INNER + SSD_BC],
             xbc_ref[s, :, SSD_INNER + SSD_BC:], dt_ref[s], h_ref.at[s]) for s in range(nb)]
    ys = _ssd_chunks(seqs, a_ref[...], e_ref, tri_ref, True, SSD_HEADS)
    for s in range(nb):
        y_o[s] = (ys[s] + yf_ref[s].astype(F32)).astype(BF16)

    @pl.when(i == pl.num_programs(1) - 1)
    def _():
        hn_o[...] = h_ref[...]


def _ssd(xbc, dt, h0_f, h0_b, rope_cos, rope_sin, lw, consts):
    b, t, _ = xbc.shape
    q = SSD_CHUNK
    nc = t // q
    nb = SSD_SEQS_PER_STEP if b % SSD_SEQS_PER_STEP == 0 else 1
    hb = q // HALO
    n_hb = t // HALO
    chunk = lambda c: pl.BlockSpec((nb, q, c), lambda i, j: (i, j, 0))
    state = pl.BlockSpec((nb, SSD_STATE, SSD_INNER), lambda i, j: (i, 0, 0))
    state_shape = jax.ShapeDtypeStruct((b, SSD_STATE, SSD_INNER), F32)
    scratch_h = pltpu.VMEM((nb, SSD_STATE, SSD_INNER), F32)

    xbc_c, y_f, hn_f = pl.pallas_call(
        _ssd_fwd_kernel,
        out_shape=[jax.ShapeDtypeStruct((b, t, SSD_CONV_DIM), BF16), jax.ShapeDtypeStruct((b, t, SSD_INNER), BF16),
                   state_shape],
        grid=(b // nb, nc),
        in_specs=[chunk(SSD_CONV_DIM),
                  pl.BlockSpec((nb, HALO, SSD_CONV_DIM), lambda i, j: (i, jnp.maximum(j * hb - 1, 0), 0)),
                  pl.BlockSpec((nb, HALO, SSD_CONV_DIM), lambda i, j: (i, jnp.minimum((j + 1) * hb, n_hb - 1), 0)),
                  chunk(LANES),
                  pl.BlockSpec((q, LANES), lambda i, j: (j, 0)),
                  pl.BlockSpec((q, LANES), lambda i, j: (j, 0)),
                  _const_spec(lw["conv_w"].shape), _const_spec(lw["conv_b"].shape),
                  _const_spec(consts["conv_shift"].shape),
                  _const_spec(lw["a_fwd"].shape), _const_spec(consts["e_fwd"].shape),
                  _const_spec(consts["tri_fwd"].shape), _const_spec(lw["d_skip"].shape), state],
        out_specs=[chunk(SSD_CONV_DIM), chunk(SSD_INNER), state],
        scratch_shapes=[scratch_h],
        compiler_params=_params("parallel", "arbitrary"),
        name="ssd_forward",
    )(xbc, xbc, xbc, dt, rope_cos, rope_sin, lw["conv_w"], lw["conv_b"], consts["conv_shift"],
      lw["a_fwd"], consts["e_fwd"], consts["tri_fwd"], lw["d_skip"], h0_f)

    rchunk = lambda c: pl.BlockSpec((nb, q, c), lambda i, j: (i, nc - 1 - j, 0))
    y, hn_b = pl.pallas_call(
        _ssd_bwd_kernel,
        out_shape=[jax.ShapeDtypeStruct((b, t, SSD_INNER), BF16), state_shape],
        grid=(b // nb, nc),
        in_specs=[rchunk(SSD_CONV_DIM), rchunk(LANES), rchunk(SSD_INNER),
                  _const_spec(lw["a_bwd"].shape), _const_spec(consts["e_bwd"].shape),
                  _const_spec(consts["tri_bwd"].shape), state],
        out_specs=[rchunk(SSD_INNER), state],
        scratch_shapes=[scratch_h],
        compiler_params=_params("parallel", "arbitrary"),
        name="ssd_backward",
    )(xbc_c, dt, y_f, lw["a_bwd"], consts["e_bwd"], consts["tri_bwd"], h0_b)
    return y, hn_f, hn_b


def _bias_kernel(rpb_ref, o_ref):
    lh = pl.program_id(0)
    n_ri = 2 * NA_ROWS - 1
    n_ci = 2 * NA_COLS - 1
    lane = lax.broadcasted_iota(jnp.int32, (GRID_W, LANES), 1)
    qc = lax.broadcasted_iota(jnp.int32, (GRID_W, LANES), 0)
    kc = lane % GRID_W
    cs = jnp.clip(qc - NA_COLS // 2, 0, GRID_W - NA_COLS)
    col_ok = (kc >= cs) & (kc < cs + NA_COLS)
    ci = jnp.clip(kc - qc + (NA_COLS - 1), 0, n_ci - 1)
    tiles = []
    for ri in range(n_ri):
        base = (lh * n_ri + ri) * n_ci
        acc = jnp.full((GRID_W, LANES), NEG, F32)
        for c in range(n_ci):
            acc = jnp.where(col_ok & (ci == c), rpb_ref[base + c] * LOG2E, acc)
        tiles.append(acc)
    masked = jnp.full((GRID_W, LANES), NEG, F32)
    tiles = [masked] + tiles + [masked]
    for e in range(n_ri + 1):
        o_ref[0, e] = jnp.where(lane < GRID_W, tiles[e], tiles[e + 1])


def _bias_table(rpb):
    depth, heads, n_ri, n_ci = rpb.shape
    return pl.pallas_call(
        _bias_kernel,
        out_shape=jax.ShapeDtypeStruct((depth * heads, n_ri + 1, GRID_W, LANES), F32),
        grid=(depth * heads,),
        in_specs=[pl.BlockSpec(memory_space=pltpu.SMEM)],
        out_specs=pl.BlockSpec((1, n_ri + 1, GRID_W, LANES), lambda i: (i, 0, 0, 0)),
        compiler_params=pltpu.CompilerParams(dimension_semantics=("arbitrary",)),
        name="na_bias_table",
    )(rpb.reshape(-1))


def _na_tile(q_ref, k_ref, v_ref, kc_ref, vc_ref, bias_ref, o_ref, start, nw, plan):
    tq = q_ref.shape[1]
    n_ctx = kc_ref.shape[1]
    n_cb = n_ctx // LANES
    lane_q = lax.broadcasted_iota(jnp.int32, (tq, LANES), 1)
    lane_k = lax.broadcasted_iota(jnp.int32, (n_ctx + nw * GRID_W, LANES), 1)
    n_pairs = NA_WIDTH // LANES
    own = [lambda lane, hh=hh: (lane < NA_HEAD_DIM) == (hh == 0) for hh in range(2)]
    col = lambda p: slice(p * LANES, (p + 1) * LANES)
    scores_all = []
    for p in range(n_pairs):
        qp = q_ref[0, :, col(p)]
        keys = jnp.concatenate([kc_ref[0, :, col(p)], k_ref[0, pl.ds(start, nw * GRID_W), col(p)]], axis=0)
        for hh in range(2):
            scores_all.append(_dot_nt(jnp.where(own[hh](lane_q), qp, jnp.zeros_like(qp)), keys))
    p_mats = []
    for h, scores in enumerate(scores_all):
        p_rows = []
        for qi, row_plan in enumerate(plan):
            rows = slice(qi * GRID_W, (qi + 1) * GRID_W)
            blocks = [scores[rows, m * LANES:(m + 1) * LANES] for m in range(n_cb)]
            for m, (ent, ok) in enumerate(row_plan):
                if ok is None:
                    blocks.append(None)
                    continue
                sb = scores[rows, (n_cb + m) * LANES:(n_cb + m + 1) * LANES] + bias_ref[h, ent]
                blocks.append(sb if ok is True else jnp.where(ok, sb, NEG))
            live = [sb for sb in blocks if sb is not None]
            mx = live[0]
            for sb in live[1:]:
                mx = jnp.maximum(mx, sb)
            mx = jnp.max(mx, axis=-1, keepdims=True)
            p_rows.append(jnp.concatenate(
                [jnp.zeros((GRID_W, LANES), BF16) if sb is None else jnp.exp2(sb - mx).astype(BF16)
                 for sb in blocks], axis=1))
        p_mats.append(jnp.concatenate(p_rows, axis=0))
    for p in range(n_pairs):
        vals = jnp.concatenate([vc_ref[0, :, col(p)], v_ref[0, pl.ds(start, nw * GRID_W), col(p)]], axis=0)
        nums = [_dot(p_mats[2 * p + hh], jnp.where(own[hh](lane_k), vals, jnp.ones_like(vals))) for hh in range(2)]
        num = jnp.where(lane_q < NA_HEAD_DIM, nums[0], nums[1])
        den = pltpu.roll(jnp.where(lane_q < NA_HEAD_DIM, nums[1], nums[0]), NA_HEAD_DIM, 1)
        o_ref[0, :, col(p)] = (num / den).astype(BF16)


def _na_kernel(q_ref, k_ref, v_ref, kc_ref, vc_ref, bias_ref, o_ref, *, nw, rows_n):
    rt = NA_QROWS
    half = NA_ROWS // 2
    r0 = pl.program_id(1) * rt
    lane_r = lax.broadcasted_iota(jnp.int32, (GRID_W, LANES), 1)
    first_half = lane_r < GRID_W
    nblk = nw // 2
    interior = (r0 >= half) & (r0 - half <= rows_n - nw)
    args = (q_ref, k_ref, v_ref, kc_ref, vc_ref, bias_ref, o_ref)

    @pl.when(interior)
    def _():
        plan = []
        for qi in range(rt):
            row_plan = []
            for m in range(nblk):
                ok0 = qi <= 2 * m < qi + NA_ROWS
                ok1 = qi <= 2 * m + 1 < qi + NA_ROWS
                ok = True if ok0 and ok1 else None if not (ok0 or ok1) else first_half if ok0 else ~first_half
                row_plan.append((half + 2 * m - qi, ok))
            plan.append(row_plan)
        _na_tile(*args, pl.multiple_of((r0 - half) * GRID_W, GRID_W), nw, plan)

    @pl.when(~interior)
    def _():
        base = jnp.clip(r0 - half, 0, rows_n - nw)
        plan = []
        for qi in range(rt):
            r = r0 + qi
            rs = jnp.clip(r - half, 0, rows_n - NA_ROWS)
            row_plan = []
            for m in range(nblk):
                j0 = base + 2 * m
                ok0 = (j0 >= rs) & (j0 < rs + NA_ROWS)
                ok1 = (j0 + 1 >= rs) & (j0 + 1 < rs + NA_ROWS)
                ok = jnp.where(first_half, ok0.astype(jnp.int32), ok1.astype(jnp.int32)) > 0
                row_plan.append((jnp.clip(j0 - r + NA_ROWS, 0, 2 * NA_ROWS - 1), ok))
            plan.append(row_plan)
        _na_tile(*args, pl.multiple_of(base * GRID_W, GRID_W), nw, plan)


def _neighbourhood_attention(q, k, v, kc, vc, bias):
    b, t, w = q.shape
    n_ctx = kc.shape[1]
    rows_n = t // GRID_W
    nw = NA_QROWS + NA_ROWS
    nw += nw % 2
    tq = NA_QROWS * GRID_W
    whole = lambda n: pl.BlockSpec((1, n, w), lambda i, j: (i, 0, 0))
    return pl.pallas_call(
        functools.partial(_na_kernel, nw=nw, rows_n=rows_n),
        out_shape=jax.ShapeDtypeStruct((b, t, w), BF16),
        grid=(b, t // tq),
        in_specs=[pl.BlockSpec((1, tq, w), lambda i, j: (i, j, 0)), whole(t), whole(t), whole(n_ctx), whole(n_ctx),
                  _const_spec(bias.shape)],
        out_specs=pl.BlockSpec((1, tq, w), lambda i, j: (i, j, 0)),
        compiler_params=_params("parallel", "arbitrary"),
        name="neighbourhood_attention",
    )(q, k, v, kc, vc, bias)


def _ctx_attn_kernel(q_ref, k_ref, v_ref, o_ref):
    n = q_ref.shape[1]
    lane = lax.broadcasted_iota(jnp.int32, (n, LANES), 1)
    for p in range(NA_WIDTH // LANES):
        cols = slice(p * LANES, (p + 1) * LANES)
        qp = q_ref[0, :, cols]
        kp = k_ref[0, :, cols]
        vp = v_ref[0, :, cols]
        nums = []
        for hh in range(2):
            own = (lane < NA_HEAD_DIM) == (hh == 0)
            s = _dot_nt(jnp.where(own, qp, jnp.zeros_like(qp)), kp)
            pm = jnp.exp2(s - jnp.max(s, axis=-1, keepdims=True)).astype(BF16)
            nums.append(_dot(pm, jnp.where(own, vp, jnp.ones_like(vp))))
        num = jnp.where(lane < NA_HEAD_DIM, nums[0], nums[1])
        den = pltpu.roll(jnp.where(lane < NA_HEAD_DIM, nums[1], nums[0]), NA_HEAD_DIM, 1)
        o_ref[0, :, cols] = (num / den).astype(BF16)


def _context_attention(q, k, v):
    b, n, w = q.shape
    spec = pl.BlockSpec((1, n, w), lambda i: (i, 0, 0))
    return pl.pallas_call(
        _ctx_attn_kernel,
        out_shape=jax.ShapeDtypeStruct((b, n, w), BF16),
        grid=(b,),
        in_specs=[spec, spec, spec],
        out_specs=spec,
        compiler_params=_params("parallel"),
        name="context_attention",
    )(q, k, v)


def _merge_ffn_kernel(x_ref, mod_ref, gate_ref, yssd_ref, z_ref, yna_ref, ygm_ref, sn_ref, wa, wb, wc, wo,
                      n2_ref, wfi, wfo, o_ref, *, ffn_chunk):
    d = x_ref.shape[2]
    y = yssd_ref[0].astype(F32) * _silu(z_ref[0].astype(F32))
    y = (y * lax.rsqrt(jnp.mean(y * y, axis=-1, keepdims=True) + EPS) * sn_ref[...]).astype(BF16)
    mixed = (gate_ref[0, :, 0:d].astype(F32) * _dot(y, wa[...])
             + gate_ref[0, :, d:2 * d].astype(F32) * _dot(yna_ref[0], wb[...])
             + gate_ref[0, :, 2 * d:3 * d].astype(F32) * _dot(ygm_ref[0], wc[...]))
    x1 = x_ref[0] + mod_ref[0, 2:3, :] * _dot(mixed.astype(BF16), wo[...])
    xn = x1 * lax.rsqrt(jnp.mean(x1 * x1, axis=-1, keepdims=True) + EPS) * n2_ref[...]
    hb = (xn * (1.0 + mod_ref[0, 4:5, :]) + mod_ref[0, 3:4, :]).astype(BF16)
    hid = wfo.shape[0]
    acc = jnp.zeros_like(x1)
    for c0 in range(0, hid, ffn_chunk):
        a = _dot(hb, wfi[:, c0:c0 + ffn_chunk])
        g = _dot(hb, wfi[:, hid + c0:hid + c0 + ffn_chunk])
        acc = acc + _dot((_silu(a) * g).astype(BF16), wfo[c0:c0 + ffn_chunk, :])
    o_ref[0] = x1 + mod_ref[0, 5:6, :] * acc


def _merge_ffn(x, mod6, gate, y_ssd, z, y_na, y_gm, lw):
    b, t, d = x.shape
    tm = min(TOKEN_TILE, t)
    tok = lambda c: pl.BlockSpec((1, tm, c), lambda i, j: (i, j, 0))
    consts = [lw["ssd_norm"], lw["w_branch_ssd"], lw["w_branch_na"], lw["w_branch_gm"], lw["w_out"], lw["norm2"],
              lw["w_ffn_in"], lw["w_ffn_out"]]
    return pl.pallas_call(
        functools.partial(_merge_ffn_kernel, ffn_chunk=256),
        out_shape=jax.ShapeDtypeStruct((b, t, d), F32),
        grid=(b, t // tm),
        in_specs=[tok(d), pl.BlockSpec((1, 6, d), lambda i, j: (i, 0, 0)),
                  tok(3 * d), tok(SSD_INNER), tok(SSD_INNER), tok(NA_WIDTH), tok(GM_WIDTH)]
                 + [_const_spec(a.shape) for a in consts],
        out_specs=tok(d),
        compiler_params=_params("parallel", "parallel"),
        name="merge_out_ffn",
    )(x, mod6, gate, y_ssd, z, y_na, y_gm, *consts)


def _cast_kernel(w_ref, o_ref):
    o_ref[...] = w_ref[0].astype(BF16)


def _weight_bf16(w, l, col0=0, width=None):
    _, rows, cols = w.shape
    width = cols if width is None else width
    cbw = next(c for c in (width, 1024, 512, 256, LANES) if col0 % c == 0 and width % c == 0)
    rb = 256
    return pl.pallas_call(
        _cast_kernel,
        out_shape=jax.ShapeDtypeStruct((rows, width), BF16),
        grid=(rows // rb, width // cbw),
        in_specs=[pl.BlockSpec((1, rb, cbw), lambda i, j: (l, i, col0 // cbw + j))],
        out_specs=pl.BlockSpec((rb, cbw), lambda i, j: (i, j)),
        compiler_params=_params("parallel", "parallel"),
        name="weight_to_bf16",
    )(w)


def _shared_constants(n_ctx, seq):
    pos = jnp.arange(seq)
    freqs = ROPE_BASE ** (-jnp.arange(ROPE_FREQS, dtype=F32) / ROPE_FREQS)
    ang_row = (pos // GRID_W).astype(F32)[:, None] * freqs
    ang_col = (pos % GRID_W).astype(F32)[:, None] * freqs
    cos = jnp.concatenate([jnp.cos(ang_row), jnp.cos(ang_row), jnp.cos(ang_col), jnp.cos(ang_col)], axis=1)
    sin = jnp.concatenate([-jnp.sin(ang_row), jnp.sin(ang_row), -jnp.sin(ang_col), jnp.sin(ang_col)], axis=1)

    r = jnp.arange(SSD_CHUNK)
    twice = lambda m, axis: jnp.concatenate([m, m], axis=axis).astype(BF16)
    lane_head = jnp.arange(SSD_INNER) // SSD_HEAD_DIM
    rows = jnp.arange(LANES)
    src = jnp.arange(SSD_CHUNK + 2 * HALO)
    conv_shift = jnp.concatenate([(src[None, :] == r[:, None] + HALO + k - SSD_CONV // 2) for k in range(SSD_CONV)],
                                 axis=1).astype(BF16)
    return dict(rope_cos=cos, rope_sin=sin,
                ctx_cos=jnp.ones((n_ctx, LANES), F32), ctx_sin=jnp.zeros((n_ctx, LANES), F32),
                tri_fwd=twice(r[None, :] <= r[:, None], 1), tri_bwd=twice(r[None, :] >= r[:, None], 1),
                e_fwd=twice(rows[:, None] == lane_head[None, :], 0),
                e_bwd=twice(rows[:, None] == lane_head[None, :] + SSD_HEADS, 0),
                conv_shift=conv_shift)


def _layer_weights(l, p):
    d = p["w_in"].shape[1]
    sizes = (SSD_INNER, SSD_CONV_DIM, 2 * SSD_HEADS, NA_WIDTH, NA_WIDTH, NA_WIDTH, 2 * GM_WIDTH, 3 * d)
    names = ("w_z", "w_xbc", "w_dt", "w_q", "w_k", "w_v", "w_uv", "w_gate")
    lw, start = {}, 0
    w_in = p["w_in"][l]
    for name, size in zip(names, sizes):
        if start % LANES == 0 and size % LANES == 0:
            lw[name] = _weight_bf16(p["w_in"], l, start, size)
        else:
            lw[name] = w_in[:, start:start + size].astype(BF16)
        start += size
    pad_lanes = lambda v: jnp.pad(v, (0, LANES - v.shape[0])).reshape(1, LANES)
    lw["w_dt"] = jnp.pad(lw["w_dt"], ((0, 0), (0, LANES - 2 * SSD_HEADS)))
    lw["dt_bias"] = pad_lanes(p["dt_bias"][l].reshape(-1))
    a = -jnp.exp(p["a_log"][l].astype(F32))
    lw["a_fwd"] = pad_lanes(a[0] * LOG2E)
    lw["a_bwd"] = pad_lanes(jnp.concatenate([jnp.zeros((SSD_HEADS,), F32), a[1] * LOG2E]))
    row = lambda v: v.reshape(1, -1)
    lw["norm1"] = row(p["norm1"][l])
    lw["norm2"] = row(p["norm2"][l])
    lw["b_gate"] = row(p["b_gate"][l])
    lw["q_norm"] = row(jnp.tile(p["q_norm"][l], NA_HEADS))
    lw["k_norm"] = row(jnp.tile(p["k_norm"][l], NA_HEADS))
    head = jnp.arange(NA_WIDTH) // NA_HEAD_DIM
    lw["head_blk"] = ((head[:, None] == head[None, :]).astype(F32) / NA_HEAD_DIM).astype(BF16)
    lw["gm_norm"] = row(p["gm_norm"][l])
    w_s = p["w_spatial"][l].astype(BF16)
    lw["w_spatial"] = jnp.concatenate([w_s[0::2], w_s[1::2]], axis=2)
    lw["b_spatial"] = jnp.repeat(p["b_spatial"][l].T, GM_WIDTH // GM_GROUPS, axis=1)
    lw["conv_w"] = jnp.pad(p["conv_w"][l], ((0, 8 - SSD_CONV), (0, 0))).astype(BF16)
    lw["conv_b"] = row(p["conv_b"][l])
    lw["d_skip"] = row(jnp.repeat(p["d_skip"][l], SSD_HEAD_DIM))
    lw["ssd_norm"] = row(p["ssd_norm"][l])
    for name in ("w_branch_ssd", "w_branch_na", "w_branch_gm", "w_out", "w_ffn_in", "w_ffn_out"):
        lw[name] = _weight_bf16(p[name], l)
    return lw


def kernel(x, c, ctx, c_ctx, w_mod, b_mod, norm1, w_in, b_gate, conv_w, conv_b, a_log, dt_bias, d_skip, ssd_norm,
           q_norm, k_norm, rpb, gm_norm, w_spatial, b_spatial, w_branch_ssd, w_branch_na, w_branch_gm, w_out,
           norm2, w_ffn_in, w_ffn_out):
    p = dict(norm1=norm1, w_in=w_in, b_gate=b_gate, conv_w=conv_w, conv_b=conv_b, a_log=a_log, dt_bias=dt_bias,
             d_skip=d_skip, ssd_norm=ssd_norm, q_norm=q_norm, k_norm=k_norm, gm_norm=gm_norm, w_spatial=w_spatial,
             b_spatial=b_spatial, w_branch_ssd=w_branch_ssd, w_branch_na=w_branch_na, w_branch_gm=w_branch_gm,
             w_out=w_out, norm2=norm2, w_ffn_in=w_ffn_in, w_ffn_out=w_ffn_out)
    b, seq, d = x.shape
    n_ctx = ctx.shape[1]
    depth = w_mod.shape[0]

    c_all = jnp.zeros((8, d), F32).at[:b].set(c).at[b].set(c_ctx)
    mod = _modulation(c_all, w_mod, b_mod)
    bias = _bias_table(rpb)
    consts = _shared_constants(n_ctx, seq)
    zero_state = jnp.zeros((b, SSD_STATE, SSD_INNER), F32)

    xc = ctx
    for l in range(depth):
        lw = _layer_weights(l, p)
        mod_x = mod[l, :b].reshape(b, 6, d)
        mod_c = mod[l, b].reshape(1, 6, d)
        last = l == depth - 1

        flat = lambda t: t.reshape(1, b * n_ctx, t.shape[-1])
        per_sample = lambda t: t.reshape(b, n_ctx, t.shape[-1])
        zc, xbcc, dtc, qc, kc, vc, ygm_c, gate_c = [per_sample(t) for t in _input_projection(flat(xc), mod_c, lw)]
        z, xbc, dt, q, k, v, y_gm, gate = _input_projection(x, mod_x, lw)

        yssd_c, s_f, s_b = _ssd(xbcc, dtc, zero_state, zero_state, consts["ctx_cos"], consts["ctx_sin"], lw, consts)
        y_ssd, _, _ = _ssd(xbc, dt, s_f, s_b, consts["rope_cos"], consts["rope_sin"], lw, consts)

        y_na = _neighbourhood_attention(q, k, v, kc, vc, bias[l * NA_HEADS:(l + 1) * NA_HEADS])
        x = _merge_ffn(x, mod_x, gate, y_ssd, z, y_na, y_gm, lw)
        if not last:
            yna_c = _context_attention(qc, kc, vc)
            xc = per_sample(_merge_ffn(flat(xc), mod_c, flat(gate_c), flat(yssd_c), flat(zc), flat(yna_c), flat(ygm_c), lw))
    return x
```

```python
import functools
import math

import jax
import jax.numpy as jnp
from jax import lax
from jax.experimental import pallas as pl
from jax.experimental.pallas import tpu as pltpu

F32 = jnp.float32
BF16 = jnp.bfloat16

EPS = 1e-6
GRID_W = 64

SSD_INNER = 1024
SSD_HEAD_DIM = 64
SSD_HEADS = 16
SSD_GROUPS = 4
SSD_STATE = 128
SSD_CONV = 5
SSD_CHUNK = 128
SSD_BC = SSD_GROUPS * SSD_STATE
SSD_CONV_DIM = SSD_INNER + 2 * SSD_BC
ROPE_FREQS = 32
ROPE_BASE = 10000.0

NA_HEAD_DIM = 64
NA_WIDTH = 512
NA_HEADS = 8
NA_ROWS = 8
NA_COLS = 16
NA_QROWS = 4

GM_WIDTH = 512
GM_GROUPS = 8
GM_CHUNK = 128

LANES = 128
HALO = 16
NEG = -1e30
LOG2E = math.log2(math.e)
VMEM_LIMIT = 56 * 1024 * 1024
TOKEN_TILE = 512
SSD_SEQS_PER_STEP = 4


def _dot(a, b):
    return jnp.dot(a, b, preferred_element_type=F32)


def _dot_nt(a, b):
    return lax.dot_general(a, b, (((1,), (1,)), ((), ())), preferred_element_type=F32)


def _silu(x):
    return x / (1.0 + jnp.exp(-x))


def _sigmoid(x):
    return 1.0 / (1.0 + jnp.exp(-x))


def _gelu_tanh(x):
    return 0.5 * x * (1.0 + jnp.tanh(math.sqrt(2.0 / math.pi) * (x + 0.044715 * (x * x * x))))


def _softplus(x):
    return jnp.maximum(x, 0.0) + jnp.log(1.0 + jnp.exp(-jnp.abs(x)))


def _split_hi_lo(v):
    hi = v.astype(BF16)
    lo = (v - hi.astype(F32)).astype(BF16)
    return hi, lo


def _const_spec(shape):
    nd = len(shape)
    return pl.BlockSpec(shape, lambda *_: (0,) * nd, pipeline_mode=pl.Buffered(1))


def _params(*semantics):
    return pltpu.CompilerParams(dimension_semantics=semantics, vmem_limit_bytes=VMEM_LIMIT)


def _mod_kernel(c_ref, w_ref, b_ref, o_ref):
    o_ref[0] = _dot(_silu(c_ref[...]), w_ref[0]) + b_ref[0]


def _modulation(c_all, w_mod, b_mod):
    depth, d, n = w_mod.shape
    tn = 1536
    return pl.pallas_call(
        _mod_kernel,
        out_shape=jax.ShapeDtypeStruct((depth, 8, n), F32),
        grid=(depth, n // tn),
        in_specs=[pl.BlockSpec((8, d), lambda l, j: (0, 0)),
                  pl.BlockSpec((1, d, tn), lambda l, j: (l, 0, j)),
                  pl.BlockSpec((1, 1, tn), lambda l, j: (l, 0, j))],
        out_specs=pl.BlockSpec((1, 8, tn), lambda l, j: (l, 0, j)),
        compiler_params=_params("arbitrary", "arbitrary"),
        name="modulation",
    )(c_all, w_mod, b_mod.reshape(depth, 1, n))


def _inproj_kernel(x_ref, mod_ref, n1_ref, wz, wxbc, wdt, wq, wk, wv, wuv, wg, dtb, bg, qn, kn, blk,
                   gmn, ws, bsp, z_o, xbc_o, dt_o, q_o, k_o, v_o, ygm_o, gate_o):
    tm = x_ref.shape[1]
    x = x_ref[0]
    xn = x * lax.rsqrt(jnp.mean(x * x, axis=-1, keepdims=True) + EPS) * n1_ref[...]
    hb = (xn * (1.0 + mod_ref[0, 1:2, :]) + mod_ref[0, 0:1, :]).astype(BF16)

    cw = 512

    def head_norm(t, w_row):
        ms = _dot((t * t).astype(BF16), blk[...])
        return t * lax.rsqrt(ms + EPS) * w_row

    qf = _dot(hb, wq[...])
    kf = _dot(hb, wk[...])
    g = _gelu_tanh(_dot(hb, wuv[...]))
    dt_o[0] = _softplus(_dot(hb, wdt[...]) + dtb[...])
    v_o[0] = _dot(hb, wv[...]).astype(BF16)
    for n0 in range(0, z_o.shape[2], cw):
        z_o[0, :, n0:n0 + cw] = _dot(hb, wz[:, n0:n0 + cw]).astype(BF16)

    q_o[0] = (head_norm(qf, qn[...]) * (NA_HEAD_DIM ** -0.5 * LOG2E)).astype(BF16)
    k_o[0] = head_norm(kf, kn[...]).astype(BF16)
    for n0 in range(0, xbc_o.shape[2], cw):
        xbc_o[0, :, n0:n0 + cw] = _dot(hb, wxbc[:, n0:n0 + cw]).astype(BF16)

    u = g[:, :GM_WIDTH]
    v = g[:, GM_WIDTH:]
    vb = (v * lax.rsqrt(jnp.mean(v * v, axis=-1, keepdims=True) + EPS) * gmn[...]).astype(BF16)
    first = lax.broadcasted_iota(jnp.int32, (GM_CHUNK, LANES), 1) < LANES // 2
    for c0 in range(0, tm, GM_CHUNK):
        for p in range(GM_WIDTH // LANES):
            cols = slice(p * LANES, (p + 1) * LANES)
            vp = vb[c0:c0 + GM_CHUNK, cols]
            zero = jnp.zeros_like(vp)
            stacked = jnp.concatenate([jnp.where(first, vp, zero), jnp.where(first, zero, vp)], axis=0)
            mixed = _dot(ws[p], stacked) + bsp[:, cols]
            ygm_o[0, c0:c0 + GM_CHUNK, cols] = (u[c0:c0 + GM_CHUNK, cols] * mixed).astype(BF16)

    for n0 in range(0, gate_o.shape[2], cw):
        gate_o[0, :, n0:n0 + cw] = _sigmoid(_dot(hb, wg[:, n0:n0 + cw]) + bg[:, n0:n0 + cw]).astype(BF16)


def _input_projection(x, mod6, lw):
    b, t, d = x.shape
    tm = min(TOKEN_TILE, t)
    tok = lambda c: pl.BlockSpec((1, tm, c), lambda i, j: (i, j, 0))
    consts = [lw["norm1"], lw["w_z"], lw["w_xbc"], lw["w_dt"], lw["w_q"], lw["w_k"], lw["w_v"], lw["w_uv"],
              lw["w_gate"], lw["dt_bias"], lw["b_gate"], lw["q_norm"], lw["k_norm"], lw["head_blk"],
              lw["gm_norm"], lw["w_spatial"], lw["b_spatial"]]
    widths = [(SSD_INNER, BF16), (SSD_CONV_DIM, BF16), (LANES, F32), (NA_WIDTH, BF16), (NA_WIDTH, BF16),
              (NA_WIDTH, BF16), (GM_WIDTH, BF16), (3 * d, BF16)]
    return pl.pallas_call(
        _inproj_kernel,
        out_shape=[jax.ShapeDtypeStruct((b, t, c), ty) for c, ty in widths],
        grid=(b, t // tm),
        in_specs=[tok(d), pl.BlockSpec((1, 6, d), lambda i, j: (i, 0, 0))] + [_const_spec(a.shape) for a in consts],
        out_specs=[tok(c) for c, _ in widths],
        compiler_params=_params("parallel", "parallel"),
        name="input_projection",
    )(x, mod6, *consts)


def _ssd_chunks(seqs, a_row, e2_ref, tri2_ref, reverse, head_off):
    q = seqs[0][0].shape[0]
    last = 0 if reverse else q - 1
    gw = SSD_INNER // SSD_GROUPS
    hpg = gw // SSD_HEAD_DIM
    groups = range(SSD_GROUPS)
    gcols = lambda g: slice(g * gw, (g + 1) * gw)
    ncols = lambda g: slice(g * SSD_STATE, (g + 1) * SSD_STATE)
    tri2 = tri2_ref[...]

    acums = []
    for xs, bm, cm, dt, h_ref in seqs:
        a = dt * a_row
        a1 = a.astype(BF16)
        r1 = a - a1.astype(F32)
        a2 = r1.astype(BF16)
        a3 = (r1 - a2.astype(F32)).astype(BF16)
        acums.append(_dot(tri2, jnp.concatenate([a1, a2], axis=0)) + _dot(tri2[:, :q], a3))

    wides = []
    for (xs, bm, cm, dt, h_ref), acum in zip(seqs, acums):
        a_last = acum[last:last + 1, :]
        per_head = jnp.concatenate([dt, dt * jnp.exp2(a_last - acum),
                                    jnp.broadcast_to(jnp.exp2(a_last), (8, LANES))], axis=0)
        hi, lo = _split_hi_lo(per_head)
        wides.append(_dot(jnp.concatenate([hi, lo], axis=1), e2_ref[...]))

    cbs, y_inters, bts, h_prevs = [], [], [], []
    for xs, bm, cm, dt, h_ref in seqs:
        cgbs = [cm[:, ncols(g)].astype(BF16) for g in groups]
        h_prev = [h_ref[:, gcols(g)] for g in groups]
        cbs.append([_dot_nt(cgbs[g], bm[:, ncols(g)].astype(BF16)) for g in groups])
        y_inters.append([_dot(cgbs[g], h_prev[g].astype(BF16)) for g in groups])
        bts.append([bm[:, ncols(g)].astype(F32).T.astype(BF16) for g in groups])
        h_prevs.append(h_prev)

    row = lax.broadcasted_iota(jnp.int32, (q, q), 0)
    col = lax.broadcasted_iota(jnp.int32, (q, q), 1)
    in_scan = (col >= row) if reverse else (col <= row)
    lane = lax.broadcasted_iota(jnp.int32, (q, LANES), 1)
    lane_head = lax.broadcasted_iota(jnp.int32, (q, gw), 1) // SSD_HEAD_DIM

    lhs, rhs, dec_outs, xw_bs, h_decs = [], [], [], [], []
    for si, (xs, bm, cm, dt, h_ref) in enumerate(seqs):
        acum, wide = acums[si], wides[si]
        acum_t = acum.T
        xdt_b = (xs * wide[0:q]).astype(BF16)
        xw_bs.append((xs * wide[q:2 * q]).astype(BF16))
        h_decs.append(wide[2 * q:2 * q + 1])
        for g in groups:
            xg = xdt_b[:, gcols(g)]
            ms, decs = [], []
            for j in range(hpg):
                hc = head_off + g * hpg + j
                colx = jnp.broadcast_to(acum[:, hc:hc + 1], (q, q))
                decay = jnp.exp2(jnp.where(in_scan, colx - acum_t[hc:hc + 1, :], NEG))
                ms.append((cbs[si][g] * decay).astype(BF16))
                decs.append(jnp.exp2(colx))
            lhs.append(jnp.concatenate(ms, axis=1))
            rhs.append(jnp.concatenate([jnp.where(lane_head == j, xg, jnp.zeros_like(xg)) for j in range(hpg)], axis=0))
            dec_outs.append(jnp.concatenate([jnp.where(lane < SSD_HEAD_DIM, decs[2 * i], decs[2 * i + 1])
                                             for i in range(hpg // 2)], axis=1))
    y_intra = [_dot(l, r) for l, r in zip(lhs, rhs)]
    h_add = [_dot(bts[si][g], xw_bs[si][:, gcols(g)]) for si in range(len(seqs)) for g in groups]
    ys = []
    for si, (xs, bm, cm, dt, h_ref) in enumerate(seqs):
        for g in groups:
            h_ref[:, gcols(g)] = h_prevs[si][g] * h_decs[si][:, gcols(g)] + h_add[si * SSD_GROUPS + g]
        ys.append(jnp.concatenate([y_intra[si * SSD_GROUPS + g] + y_inters[si][g] * dec_outs[si * SSD_GROUPS + g]
                                   for g in groups], axis=1))
    return ys


def _ssd_fwd_kernel(cur_ref, prev_ref, next_ref, dt_ref, cos_ref, sin_ref, cw_ref, cb_ref, shift_ref, a_ref, e_ref,
                    tri_ref, dsk_ref, h0_ref, xbc_o, y_o, hn_o, h_ref):
    i = pl.program_id(1)
    nc = pl.num_programs(1)
    nb, q, _ = cur_ref.shape

    @pl.when(i == 0)
    def _():
        h_ref[...] = h0_ref[...]

    pv = jnp.where(i == 0, 0.0, 1.0).astype(BF16)
    nv = jnp.where(i == nc - 1, 0.0, 1.0).astype(BF16)
    lane = lax.broadcasted_iota(jnp.int32, (q, LANES), 1)
    cos = cos_ref[...]
    sin = sin_ref[...]

    def rope(t):
        sw = jnp.where((lane & ROPE_FREQS) == 0, pltpu.roll(t, LANES - ROPE_FREQS, 1), pltpu.roll(t, ROPE_FREQS, 1))
        return t * cos + sw * sin

    cwid = 256
    seqs = []
    for s in range(nb):
        lo = jnp.concatenate([prev_ref[s] * pv, cur_ref[s]], axis=0)
        hi = jnp.concatenate([cur_ref[s], next_ref[s] * nv], axis=0)
        srcs = [lo] * (SSD_CONV // 2) + [cur_ref[s]] + [hi] * (SSD_CONV // 2)
        parts = []
        for c0 in range(0, SSD_CONV_DIM, cwid):
            taps = jnp.concatenate([src[:, c0:c0 + cwid] * cw_ref[k:k + 1, c0:c0 + cwid]
                                    for k, src in enumerate(srcs)], axis=0)
            parts.append(_silu(_dot(shift_ref[...], taps) + cb_ref[:, c0:c0 + cwid]))
        xs = jnp.concatenate(parts[:SSD_INNER // cwid], axis=1)
        bc = jnp.concatenate(parts[SSD_INNER // cwid:], axis=1)
        bc = jnp.concatenate([rope(bc[:, g * LANES:(g + 1) * LANES]) for g in range(2 * SSD_GROUPS)], axis=1)
        xbc_o[s, :, :SSD_INNER] = xs.astype(BF16)
        xbc_o[s, :, SSD_INNER:] = bc.astype(BF16)
        seqs.append((xs, bc[:, :SSD_BC], bc[:, SSD_BC:], dt_ref[s], h_ref.at[s]))
    ys = _ssd_chunks(seqs, a_ref[...], e_ref, tri_ref, False, 0)
    for s in range(nb):
        y_o[s] = (ys[s] + dsk_ref[...] * seqs[s][0]).astype(BF16)

    @pl.when(i == nc - 1)
    def _():
        hn_o[...] = h_ref[...]


def _ssd_bwd_kernel(xbc_ref, dt_ref, yf_ref, a_ref, e_ref, tri_ref, h0_ref, y_o, hn_o, h_ref):
    i = pl.program_id(1)

    @pl.when(i == 0)
    def _():
        h_ref[...] = h0_ref[...]

    nb = xbc_ref.shape[0]
    seqs = [(xbc_ref[s, :, :SSD_INNER].astype(F32), xbc_ref[s, :, SSD_INNER:SSD_INNER + SSD_BC],
             xbc_ref[s, :, SSD_INNER + SSD_BC:], dt_ref[s], h_ref.at[s]) for s in range(nb)]
    ys = _ssd_chunks(seqs, a_ref[...], e_ref, tri_ref, True, SSD_HEADS)
    for s in range(nb):
        y_o[s] = (ys[s] + yf_ref[s].astype(F32)).astype(BF16)

    @pl.when(i == pl.num_programs(1) - 1)
    def _():
        hn_o[...] = h_ref[...]


def _ssd(xbc, dt, h0_f, h0_b, rope_cos, rope_sin, lw, consts):
    b, t, _ = xbc.shape
    q = SSD_CHUNK
    nc = t // q
    nb = SSD_SEQS_PER_STEP if b % SSD_SEQS_PER_STEP == 0 else 1
    hb = q // HALO
    n_hb = t // HALO
    chunk = lambda c: pl.BlockSpec((nb, q, c), lambda i, j: (i, j, 0))
    state = pl.BlockSpec((nb, SSD_STATE, SSD_INNER), lambda i, j: (i, 0, 0))
    state_shape = jax.ShapeDtypeStruct((b, SSD_STATE, SSD_INNER), F32)
    scratch_h = pltpu.VMEM((nb, SSD_STATE, SSD_INNER), F32)

    xbc_c, y_f, hn_f = pl.pallas_call(
        _ssd_fwd_kernel,
        out_shape=[jax.ShapeDtypeStruct((b, t, SSD_CONV_DIM), BF16), jax.ShapeDtypeStruct((b, t, SSD_INNER), BF16),
                   state_shape],
        grid=(b // nb, nc),
        in_specs=[chunk(SSD_CONV_DIM),
                  pl.BlockSpec((nb, HALO, SSD_CONV_DIM), lambda i, j: (i, jnp.maximum(j * hb - 1, 0), 0)),
                  pl.BlockSpec((nb, HALO, SSD_CONV_DIM), lambda i, j: (i, jnp.minimum((j + 1) * hb, n_hb - 1), 0)),
                  chunk(LANES),
                  pl.BlockSpec((q, LANES), lambda i, j: (j, 0)),
                  pl.BlockSpec((q, LANES), lambda i, j: (j, 0)),
                  _const_spec(lw["conv_w"].shape), _const_spec(lw["conv_b"].shape),
                  _const_spec(consts["conv_shift"].shape),
                  _const_spec(lw["a_fwd"].shape), _const_spec(consts["e_fwd"].shape),
                  _const_spec(consts["tri_fwd"].shape), _const_spec(lw["d_skip"].shape), state],
        out_specs=[chunk(SSD_CONV_DIM), chunk(SSD_INNER), state],
        scratch_shapes=[scratch_h],
        compiler_params=_params("parallel", "arbitrary"),
        name="ssd_forward",
    )(xbc, xbc, xbc, dt, rope_cos, rope_sin, lw["conv_w"], lw["conv_b"], consts["conv_shift"],
      lw["a_fwd"], consts["e_fwd"], consts["tri_fwd"], lw["d_skip"], h0_f)

    rchunk = lambda c: pl.BlockSpec((nb, q, c), lambda i, j: (i, nc - 1 - j, 0))
    y, hn_b = pl.pallas_call(
        _ssd_bwd_kernel,
        out_shape=[jax.ShapeDtypeStruct((b, t, SSD_INNER), BF16), state_shape],
        grid=(b // nb, nc),
        in_specs=[rchunk(SSD_CONV_DIM), rchunk(LANES), rchunk(SSD_INNER),
                  _const_spec(lw["a_bwd"].shape), _const_spec(consts["e_bwd"].shape),
                  _const_spec(consts["tri_bwd"].shape), state],
        out_specs=[rchunk(SSD_INNER), state],
        scratch_shapes=[scratch_h],
        compiler_params=_params("parallel", "arbitrary"),
        name="ssd_backward",
    )(xbc_c, dt, y_f, lw["a_bwd"], consts["e_bwd"], consts["tri_bwd"], h0_b)
    return y, hn_f, hn_b


def _bias_kernel(rpb_ref, o_ref):
    lh = pl.program_id(0)
    n_ri = 2 * NA_ROWS - 1
    n_ci = 2 * NA_COLS - 1
    lane = lax.broadcasted_iota(jnp.int32, (GRID_W, LANES), 1)
    qc = lax.broadcasted_iota(jnp.int32, (GRID_W, LANES), 0)
    kc = lane % GRID_W
    cs = jnp.clip(qc - NA_COLS // 2, 0, GRID_W - NA_COLS)
    col_ok = (kc >= cs) & (kc < cs + NA_COLS)
    ci = jnp.clip(kc - qc + (NA_COLS - 1), 0, n_ci - 1)
    tiles = []
    for ri in range(n_ri):
        base = (lh * n_ri + ri) * n_ci
        acc = jnp.full((GRID_W, LANES), NEG, F32)
        for c in range(n_ci):
            acc = jnp.where(col_ok & (ci == c), rpb_ref[base + c] * LOG2E, acc)
        tiles.append(acc)
    masked = jnp.full((GRID_W, LANES), NEG, F32)
    tiles = [masked] + tiles + [masked]
    for e in range(n_ri + 1):
        o_ref[0, e] = jnp.where(lane < GRID_W, tiles[e], tiles[e + 1])


def _bias_table(rpb):
    depth, heads, n_ri, n_ci = rpb.shape
    return pl.pallas_call(
        _bias_kernel,
        out_shape=jax.ShapeDtypeStruct((depth * heads, n_ri + 1, GRID_W, LANES), F32),
        grid=(depth * heads,),
        in_specs=[pl.BlockSpec(memory_space=pltpu.SMEM)],
        out_specs=pl.BlockSpec((1, n_ri + 1, GRID_W, LANES), lambda i: (i, 0, 0, 0)),
        compiler_params=pltpu.CompilerParams(dimension_semantics=("arbitrary",)),
        name="na_bias_table",
    )(rpb.reshape(-1))


def _na_tile(q_ref, k_ref, v_ref, kc_ref, vc_ref, bias_ref, o_ref, start, nw, plan):
    tq = q_ref.shape[1]
    n_ctx = kc_ref.shape[1]
    n_cb = n_ctx // LANES
    lane_q = lax.broadcasted_iota(jnp.int32, (tq, LANES), 1)
    lane_k = lax.broadcasted_iota(jnp.int32, (n_ctx + nw * GRID_W, LANES), 1)
    n_pairs = NA_WIDTH // LANES
    own = [lambda lane, hh=hh: (lane < NA_HEAD_DIM) == (hh == 0) for hh in range(2)]
    col = lambda p: slice(p * LANES, (p + 1) * LANES)
    scores_all = []
    for p in range(n_pairs):
        qp = q_ref[0, :, col(p)]
        keys = jnp.concatenate([kc_ref[0, :, col(p)], k_ref[0, pl.ds(start, nw * GRID_W), col(p)]], axis=0)
        for hh in range(2):
            scores_all.append(_dot_nt(jnp.where(own[hh](lane_q), qp, jnp.zeros_like(qp)), keys))
    p_mats = []
    for h, scores in enumerate(scores_all):
        p_rows = []
        for qi, row_plan in enumerate(plan):
            rows = slice(qi * GRID_W, (qi + 1) * GRID_W)
            blocks = [scores[rows, m * LANES:(m + 1) * LANES] for m in range(n_cb)]
            for m, (ent, ok) in enumerate(row_plan):
                if ok is None:
                    blocks.append(None)
                    continue
                sb = scores[rows, (n_cb + m) * LANES:(n_cb + m + 1) * LANES] + bias_ref[h, ent]
                blocks.append(sb if ok is True else jnp.where(ok, sb, NEG))
            live = [sb for sb in blocks if sb is not None]
            mx = live[0]
            for sb in live[1:]:
                mx = jnp.maximum(mx, sb)
            mx = jnp.max(mx, axis=-1, keepdims=True)
            p_rows.append(jnp.concatenate(
                [jnp.zeros((GRID_W, LANES), BF16) if sb is None else jnp.exp2(sb - mx).astype(BF16)
                 for sb in blocks], axis=1))
        p_mats.append(jnp.concatenate(p_rows, axis=0))
    for p in range(n_pairs):
        vals = jnp.concatenate([vc_ref[0, :, col(p)], v_ref[0, pl.ds(start, nw * GRID_W), col(p)]], axis=0)
        nums = [_dot(p_mats[2 * p + hh], jnp.where(own[hh](lane_k), vals, jnp.ones_like(vals))) for hh in range(2)]
        num = jnp.where(lane_q < NA_HEAD_DIM, nums[0], nums[1])
        den = pltpu.roll(jnp.where(lane_q < NA_HEAD_DIM, nums[1], nums[0]), NA_HEAD_DIM, 1)
        o_ref[0, :, col(p)] = (num / den).astype(BF16)


def _na_kernel(q_ref, k_ref, v_ref, kc_ref, vc_ref, bias_ref, o_ref, *, nw, rows_n):
    rt = NA_QROWS
    half = NA_ROWS // 2
    r0 = pl.program_id(1) * rt
    lane_r = lax.broadcasted_iota(jnp.int32, (GRID_W, LANES), 1)
    first_half = lane_r < GRID_W
    nblk = nw // 2
    interior = (r0 >= half) & (r0 - half <= rows_n - nw)
    args = (q_ref, k_ref, v_ref, kc_ref, vc_ref, bias_ref, o_ref)

    @pl.when(interior)
    def _():
        plan = []
        for qi in range(rt):
            row_plan = []
            for m in range(nblk):
                ok0 = qi <= 2 * m < qi + NA_ROWS
                ok1 = qi <= 2 * m + 1 < qi + NA_ROWS
                ok = True if ok0 and ok1 else None if not (ok0 or ok1) else first_half if ok0 else ~first_half
                row_plan.append((half + 2 * m - qi, ok))
            plan.append(row_plan)
        _na_tile(*args, pl.multiple_of((r0 - half) * GRID_W, GRID_W), nw, plan)

    @pl.when(~interior)
    def _():
        base = jnp.clip(r0 - half, 0, rows_n - nw)
        plan = []
        for qi in range(rt):
            r = r0 + qi
            rs = jnp.clip(r - half, 0, rows_n - NA_ROWS)
            row_plan = []
            for m in range(nblk):
                j0 = base + 2 * m
                ok0 = (j0 >= rs) & (j0 < rs + NA_ROWS)
                ok1 = (j0 + 1 >= rs) & (j0 + 1 < rs + NA_ROWS)
                ok = jnp.where(first_half, ok0.astype(jnp.int32), ok1.astype(jnp.int32)) > 0
                row_plan.append((jnp.clip(j0 - r + NA_ROWS, 0, 2 * NA_ROWS - 1), ok))
            plan.append(row_plan)
        _na_tile(*args, pl.multiple_of(base * GRID_W, GRID_W), nw, plan)


def _neighbourhood_attention(q, k, v, kc, vc, bias, layer):
    b, t, w = q.shape
    n_ctx = kc.shape[1]
    rows_n = t // GRID_W
    nw = NA_QROWS + NA_ROWS
    nw += nw % 2
    tq = NA_QROWS * GRID_W
    whole = lambda n: pl.BlockSpec((1, n, w), lambda i, j: (i, 0, 0))
    return pl.pallas_call(
        functools.partial(_na_kernel, nw=nw, rows_n=rows_n),
        out_shape=jax.ShapeDtypeStruct((b, t, w), BF16),
        grid=(b, t // tq),
        in_specs=[pl.BlockSpec((1, tq, w), lambda i, j: (i, j, 0)), whole(t), whole(t), whole(n_ctx), whole(n_ctx),
                  pl.BlockSpec((NA_HEADS,) + bias.shape[1:], lambda i, j: (layer, 0, 0, 0),
                               pipeline_mode=pl.Buffered(1))],
        out_specs=pl.BlockSpec((1, tq, w), lambda i, j: (i, j, 0)),
        compiler_params=_params("parallel", "arbitrary"),
        name="neighbourhood_attention",
    )(q, k, v, kc, vc, bias)


def _ctx_attn_kernel(q_ref, k_ref, v_ref, o_ref):
    n = q_ref.shape[1]
    lane = lax.broadcasted_iota(jnp.int32, (n, LANES), 1)
    for p in range(NA_WIDTH // LANES):
        cols = slice(p * LANES, (p + 1) * LANES)
        qp = q_ref[0, :, cols]
        kp = k_ref[0, :, cols]
        vp = v_ref[0, :, cols]
        nums = []
        for hh in range(2):
            own = (lane < NA_HEAD_DIM) == (hh == 0)
            s = _dot_nt(jnp.where(own, qp, jnp.zeros_like(qp)), kp)
            pm = jnp.exp2(s - jnp.max(s, axis=-1, keepdims=True)).astype(BF16)
            nums.append(_dot(pm, jnp.where(own, vp, jnp.ones_like(vp))))
        num = jnp.where(lane < NA_HEAD_DIM, nums[0], nums[1])
        den = pltpu.roll(jnp.where(lane < NA_HEAD_DIM, nums[1], nums[0]), NA_HEAD_DIM, 1)
        o_ref[0, :, cols] = (num / den).astype(BF16)


def _context_attention(q, k, v):
    b, n, w = q.shape
    spec = pl.BlockSpec((1, n, w), lambda i: (i, 0, 0))
    return pl.pallas_call(
        _ctx_attn_kernel,
        out_shape=jax.ShapeDtypeStruct((b, n, w), BF16),
        grid=(b,),
        in_specs=[spec, spec, spec],
        out_specs=spec,
        compiler_params=_params("parallel"),
        name="context_attention",
    )(q, k, v)


def _merge_ffn_kernel(x_ref, mod_ref, gate_ref, yssd_ref, z_ref, yna_ref, ygm_ref, sn_ref, wa, wb, wc, wo,
                      n2_ref, wfi, wfo, o_ref, *, ffn_chunk):
    d = x_ref.shape[2]
    y = yssd_ref[0].astype(F32) * _silu(z_ref[0].astype(F32))
    y = (y * lax.rsqrt(jnp.mean(y * y, axis=-1, keepdims=True) + EPS) * sn_ref[...]).astype(BF16)
    mixed = (gate_ref[0, :, 0:d].astype(F32) * _dot(y, wa[...])
             + gate_ref[0, :, d:2 * d].astype(F32) * _dot(yna_ref[0], wb[...])
             + gate_ref[0, :, 2 * d:3 * d].astype(F32) * _dot(ygm_ref[0], wc[...]))
    x1 = x_ref[0] + mod_ref[0, 2:3, :] * _dot(mixed.astype(BF16), wo[...])
    xn = x1 * lax.rsqrt(jnp.mean(x1 * x1, axis=-1, keepdims=True) + EPS) * n2_ref[...]
    hb = (xn * (1.0 + mod_ref[0, 4:5, :]) + mod_ref[0, 3:4, :]).astype(BF16)
    hid = wfo.shape[0]
    acc = jnp.zeros_like(x1)
    for c0 in range(0, hid, ffn_chunk):
        a = _dot(hb, wfi[:, c0:c0 + ffn_chunk])
        g = _dot(hb, wfi[:, hid + c0:hid + c0 + ffn_chunk])
        acc = acc + _dot((_silu(a) * g).astype(BF16), wfo[c0:c0 + ffn_chunk, :])
    o_ref[0] = x1 + mod_ref[0, 5:6, :] * acc


def _merge_ffn(x, mod6, gate, y_ssd, z, y_na, y_gm, lw):
    b, t, d = x.shape
    tm = min(TOKEN_TILE, t)
    tok = lambda c: pl.BlockSpec((1, tm, c), lambda i, j: (i, j, 0))
    consts = [lw["ssd_norm"], lw["w_branch_ssd"], lw["w_branch_na"], lw["w_branch_gm"], lw["w_out"], lw["norm2"],
              lw["w_ffn_in"], lw["w_ffn_out"]]
    return pl.pallas_call(
        functools.partial(_merge_ffn_kernel, ffn_chunk=256),
        out_shape=jax.ShapeDtypeStruct((b, t, d), F32),
        grid=(b, t // tm),
        in_specs=[tok(d), pl.BlockSpec((1, 6, d), lambda i, j: (i, 0, 0)),
                  tok(3 * d), tok(SSD_INNER), tok(SSD_INNER), tok(NA_WIDTH), tok(GM_WIDTH)]
                 + [_const_spec(a.shape) for a in consts],
        out_specs=tok(d),
        compiler_params=_params("parallel", "parallel"),
        name="merge_out_ffn",
    )(x, mod6, gate, y_ssd, z, y_na, y_gm, *consts)


def _cast_kernel(w_ref, o_ref):
    o_ref[...] = w_ref[0].astype(BF16)


def _weight_bf16(w, l):
    _, rows, cols = w.shape
    rb = next(r for r in (rows, rows // 2, rows // 4, rows // 8) if r * cols * 4 <= (6 << 20) and r % 16 == 0)
    return pl.pallas_call(
        _cast_kernel,
        out_shape=jax.ShapeDtypeStruct((rows, cols), BF16),
        grid=(rows // rb,),
        in_specs=[pl.BlockSpec((1, rb, cols), lambda i: (l, i, 0))],
        out_specs=pl.BlockSpec((rb, cols), lambda i: (i, 0)),
        compiler_params=_params("parallel"),
        name="weight_to_bf16",
    )(w)


def _shared_constants(n_ctx, seq):
    pos = jnp.arange(seq)
    freqs = ROPE_BASE ** (-jnp.arange(ROPE_FREQS, dtype=F32) / ROPE_FREQS)
    ang_row = (pos // GRID_W).astype(F32)[:, None] * freqs
    ang_col = (pos % GRID_W).astype(F32)[:, None] * freqs
    cos = jnp.concatenate([jnp.cos(ang_row), jnp.cos(ang_row), jnp.cos(ang_col), jnp.cos(ang_col)], axis=1)
    sin = jnp.concatenate([-jnp.sin(ang_row), jnp.sin(ang_row), -jnp.sin(ang_col), jnp.sin(ang_col)], axis=1)

    r = jnp.arange(SSD_CHUNK)
    twice = lambda m, axis: jnp.concatenate([m, m], axis=axis).astype(BF16)
    lane_head = jnp.arange(SSD_INNER) // SSD_HEAD_DIM
    rows = jnp.arange(LANES)
    blocks = []
    for k in range(SSD_CONV):
        off = k - SSD_CONV // 2
        src = jnp.arange(SSD_CHUNK + (HALO if off else 0))
        blocks.append(src[None, :] == r[:, None] + off + (HALO if off < 0 else 0))
    conv_shift = jnp.concatenate(blocks, axis=1).astype(BF16)
    return dict(rope_cos=cos, rope_sin=sin,
                ctx_cos=jnp.ones((n_ctx, LANES), F32), ctx_sin=jnp.zeros((n_ctx, LANES), F32),
                tri_fwd=twice(r[None, :] <= r[:, None], 1), tri_bwd=twice(r[None, :] >= r[:, None], 1),
                e_fwd=twice(rows[:, None] == lane_head[None, :], 0),
                e_bwd=twice(rows[:, None] == lane_head[None, :] + SSD_HEADS, 0),
                conv_shift=conv_shift)


def _layer_weights(l, p):
    d = p["w_in"].shape[1]
    sizes = (SSD_INNER, SSD_CONV_DIM, 2 * SSD_HEADS, NA_WIDTH, NA_WIDTH, NA_WIDTH, 2 * GM_WIDTH, 3 * d)
    names = ("w_z", "w_xbc", "w_dt", "w_q", "w_k", "w_v", "w_uv", "w_gate")
    lw, start = {}, 0
    w_in = p["w_in"][l]
    for name, size in zip(names, sizes):
        lw[name] = w_in[:, start:start + size].astype(BF16)
        start += size
    pad_lanes = lambda v: jnp.pad(v, (0, LANES - v.shape[0])).reshape(1, LANES)
    lw["w_dt"] = jnp.pad(lw["w_dt"], ((0, 0), (0, LANES - 2 * SSD_HEADS)))
    lw["dt_bias"] = pad_lanes(p["dt_bias"][l].reshape(-1))
    a = -jnp.exp(p["a_log"][l].astype(F32))
    lw["a_fwd"] = pad_lanes(a[0] * LOG2E)
    lw["a_bwd"] = pad_lanes(jnp.concatenate([jnp.zeros((SSD_HEADS,), F32), a[1] * LOG2E]))
    row = lambda v: v.reshape(1, -1)
    lw["norm1"] = row(p["norm1"][l])
    lw["norm2"] = row(p["norm2"][l])
    lw["b_gate"] = row(p["b_gate"][l])
    lw["q_norm"] = row(jnp.tile(p["q_norm"][l], NA_HEADS))
    lw["k_norm"] = row(jnp.tile(p["k_norm"][l], NA_HEADS))
    head = jnp.arange(NA_WIDTH) // NA_HEAD_DIM
    lw["head_blk"] = ((head[:, None] == head[None, :]).astype(F32) / NA_HEAD_DIM).astype(BF16)
    lw["gm_norm"] = row(p["gm_norm"][l])
    w_s = p["w_spatial"][l].astype(BF16)
    lw["w_spatial"] = jnp.concatenate([w_s[0::2], w_s[1::2]], axis=2)
    lw["b_spatial"] = jnp.repeat(p["b_spatial"][l].T, GM_WIDTH // GM_GROUPS, axis=1)
    lw["conv_w"] = jnp.pad(p["conv_w"][l], ((0, 8 - SSD_CONV), (0, 0))).astype(BF16)
    lw["conv_b"] = row(p["conv_b"][l])
    lw["d_skip"] = row(jnp.repeat(p["d_skip"][l], SSD_HEAD_DIM))
    lw["ssd_norm"] = row(p["ssd_norm"][l])
    for name in ("w_branch_ssd", "w_branch_na", "w_branch_gm", "w_out", "w_ffn_in", "w_ffn_out"):
        lw[name] = _weight_bf16(p[name], l)
    return lw


def kernel(x, c, ctx, c_ctx, w_mod, b_mod, norm1, w_in, b_gate, conv_w, conv_b, a_log, dt_bias, d_skip, ssd_norm,
           q_norm, k_norm, rpb, gm_norm, w_spatial, b_spatial, w_branch_ssd, w_branch_na, w_branch_gm, w_out,
           norm2, w_ffn_in, w_ffn_out):
    p = dict(norm1=norm1, w_in=w_in, b_gate=b_gate, conv_w=conv_w, conv_b=conv_b, a_log=a_log, dt_bias=dt_bias,
             d_skip=d_skip, ssd_norm=ssd_norm, q_norm=q_norm, k_norm=k_norm, gm_norm=gm_norm, w_spatial=w_spatial,
             b_spatial=b_spatial, w_branch_ssd=w_branch_ssd, w_branch_na=w_branch_na, w_branch_gm=w_branch_gm,
             w_out=w_out, norm2=norm2, w_ffn_in=w_ffn_in, w_ffn_out=w_ffn_out)
    b, seq, d = x.shape
    n_ctx = ctx.shape[1]
    depth = w_mod.shape[0]

    c_all = jnp.zeros((8, d), F32).at[:b].set(c).at[b].set(c_ctx)
    mod = _modulation(c_all, w_mod, b_mod)
    bias = _bias_table(rpb)
    consts = _shared_constants(n_ctx, seq)
    zero_state = jnp.zeros((b, SSD_STATE, SSD_INNER), F32)

    xc = ctx
    for l in range(depth):
        lw = _layer_weights(l, p)
        mod_x = mod[l, :b].reshape(b, 6, d)
        mod_c = mod[l, b].reshape(1, 6, d)
        last = l == depth - 1

        flat = lambda t: t.reshape(1, b * n_ctx, t.shape[-1])
        per_sample = lambda t: t.reshape(b, n_ctx, t.shape[-1])
        zc, xbcc, dtc, qc, kc, vc, ygm_c, gate_c = [per_sample(t) for t in _input_projection(flat(xc), mod_c, lw)]
        z, xbc, dt, q, k, v, y_gm, gate = _input_projection(x, mod_x, lw)

        yssd_c, s_f, s_b = _ssd(xbcc, dtc, zero_state, zero_state, consts["ctx_cos"], consts["ctx_sin"], lw, consts)
        y_ssd, _, _ = _ssd(xbc, dt, s_f, s_b, consts["rope_cos"], consts["rope_sin"], lw, consts)

        y_na = _neighbourhood_attention(q, k, v, kc, vc, bias, l)
        x = _merge_ffn(x, mod_x, gate, y_ssd, z, y_na, y_gm, lw)
        if not last:
            yna_c = _context_attention(qc, kc, vc)
            xc = per_sample(_merge_ffn(flat(xc), mod_c, flat(gate_c), flat(yssd_c), flat(zc), flat(yna_c), flat(ygm_c), lw))
    return x
```

```python
import functools
import math

import jax
import jax.numpy as jnp
from jax import lax
from jax.experimental import pallas as pl
from jax.experimental.pallas import tpu as pltpu

F32 = jnp.float32
BF16 = jnp.bfloat16

EPS = 1e-6
GRID_W = 64

SSD_INNER = 1024
SSD_HEAD_DIM = 64
SSD_HEADS = 16
SSD_GROUPS = 4
SSD_STATE = 128
SSD_CONV = 5
SSD_CHUNK = 128
SSD_BC = SSD_GROUPS * SSD_STATE
SSD_CONV_DIM = SSD_INNER + 2 * SSD_BC
ROPE_FREQS = 32
ROPE_BASE = 10000.0

NA_HEAD_DIM = 64
NA_WIDTH = 512
NA_HEADS = 8
NA_ROWS = 8
NA_COLS = 16
NA_QROWS = 4
NA_TILES_PER_STEP = 2

GM_WIDTH = 512
GM_GROUPS = 8
GM_CHUNK = 128

LANES = 128
HALO = 16
NEG = -1e30
LOG2E = math.log2(math.e)
VMEM_LIMIT = 56 * 1024 * 1024
TOKEN_TILE = 512
SSD_SEQS_PER_STEP = 4


def _dot(a, b):
    return jnp.dot(a, b, preferred_element_type=F32)


def _dot_nt(a, b):
    return lax.dot_general(a, b, (((1,), (1,)), ((), ())), preferred_element_type=F32)


def _silu(x):
    return x / (1.0 + jnp.exp(-x))


def _sigmoid(x):
    return 1.0 / (1.0 + jnp.exp(-x))


def _gelu_tanh(x):
    return 0.5 * x * (1.0 + jnp.tanh(math.sqrt(2.0 / math.pi) * (x + 0.044715 * (x * x * x))))


def _softplus(x):
    return jnp.maximum(x, 0.0) + jnp.log(1.0 + jnp.exp(-jnp.abs(x)))


def _split_hi_lo(v):
    hi = v.astype(BF16)
    lo = (v - hi.astype(F32)).astype(BF16)
    return hi, lo


def _const_spec(shape):
    nd = len(shape)
    return pl.BlockSpec(shape, lambda *_: (0,) * nd, pipeline_mode=pl.Buffered(1))


def _params(*semantics):
    return pltpu.CompilerParams(dimension_semantics=semantics, vmem_limit_bytes=VMEM_LIMIT)


def _mod_kernel(c_ref, w_ref, b_ref, o_ref):
    o_ref[0] = _dot(_silu(c_ref[...]), w_ref[0]) + b_ref[0]


def _modulation(c_all, w_mod, b_mod):
    depth, d, n = w_mod.shape
    tn = 1536
    return pl.pallas_call(
        _mod_kernel,
        out_shape=jax.ShapeDtypeStruct((depth, 8, n), F32),
        grid=(depth, n // tn),
        in_specs=[pl.BlockSpec((8, d), lambda l, j: (0, 0)),
                  pl.BlockSpec((1, d, tn), lambda l, j: (l, 0, j)),
                  pl.BlockSpec((1, 1, tn), lambda l, j: (l, 0, j))],
        out_specs=pl.BlockSpec((1, 8, tn), lambda l, j: (l, 0, j)),
        compiler_params=_params("arbitrary", "arbitrary"),
        name="modulation",
    )(c_all, w_mod, b_mod.reshape(depth, 1, n))


def _inproj_kernel(x_ref, mod_ref, n1_ref, wz, wxbc, wdt, wq, wk, wv, wuv, wg, dtb, bg, qn, kn, blk,
                   gmn, ws, bsp, z_o, xbc_o, dt_o, q_o, k_o, v_o, ygm_o, gate_o):
    tm = x_ref.shape[1]
    x = x_ref[0]
    xn = x * lax.rsqrt(jnp.mean(x * x, axis=-1, keepdims=True) + EPS) * n1_ref[...]
    hb = (xn * (1.0 + mod_ref[0, 1:2, :]) + mod_ref[0, 0:1, :]).astype(BF16)

    cw = 512

    def head_norm(t, w_row):
        ms = _dot((t * t).astype(BF16), blk[...])
        return t * lax.rsqrt(ms + EPS) * w_row

    qf = _dot(hb, wq[...])
    kf = _dot(hb, wk[...])
    g = _gelu_tanh(_dot(hb, wuv[...]))
    dt_o[0] = _softplus(_dot(hb, wdt[...]) + dtb[...])
    v_o[0] = _dot(hb, wv[...]).astype(BF16)
    for n0 in range(0, z_o.shape[2], cw):
        z_o[0, :, n0:n0 + cw] = _dot(hb, wz[:, n0:n0 + cw]).astype(BF16)

    q_o[0] = (head_norm(qf, qn[...]) * (NA_HEAD_DIM ** -0.5 * LOG2E)).astype(BF16)
    k_o[0] = head_norm(kf, kn[...]).astype(BF16)
    for n0 in range(0, xbc_o.shape[2], cw):
        xbc_o[0, :, n0:n0 + cw] = _dot(hb, wxbc[:, n0:n0 + cw]).astype(BF16)

    u = g[:, :GM_WIDTH]
    v = g[:, GM_WIDTH:]
    vb = (v * lax.rsqrt(jnp.mean(v * v, axis=-1, keepdims=True) + EPS) * gmn[...]).astype(BF16)
    first = lax.broadcasted_iota(jnp.int32, (GM_CHUNK, LANES), 1) < LANES // 2
    for c0 in range(0, tm, GM_CHUNK):
        for p in range(GM_WIDTH // LANES):
            cols = slice(p * LANES, (p + 1) * LANES)
            vp = vb[c0:c0 + GM_CHUNK, cols]
            zero = jnp.zeros_like(vp)
            stacked = jnp.concatenate([jnp.where(first, vp, zero), jnp.where(first, zero, vp)], axis=0)
            mixed = _dot(ws[p], stacked) + bsp[:, cols]
            ygm_o[0, c0:c0 + GM_CHUNK, cols] = (u[c0:c0 + GM_CHUNK, cols] * mixed).astype(BF16)

    for n0 in range(0, gate_o.shape[2], cw):
        gate_o[0, :, n0:n0 + cw] = _sigmoid(_dot(hb, wg[:, n0:n0 + cw]) + bg[:, n0:n0 + cw]).astype(BF16)


def _input_projection(x, mod6, lw):
    b, t, d = x.shape
    tm = min(TOKEN_TILE, t)
    tok = lambda c: pl.BlockSpec((1, tm, c), lambda i, j: (i, j, 0))
    consts = [lw["norm1"], lw["w_z"], lw["w_xbc"], lw["w_dt"], lw["w_q"], lw["w_k"], lw["w_v"], lw["w_uv"],
              lw["w_gate"], lw["dt_bias"], lw["b_gate"], lw["q_norm"], lw["k_norm"], lw["head_blk"],
              lw["gm_norm"], lw["w_spatial"], lw["b_spatial"]]
    widths = [(SSD_INNER, BF16), (SSD_CONV_DIM, BF16), (LANES, F32), (NA_WIDTH, BF16), (NA_WIDTH, BF16),
              (NA_WIDTH, BF16), (GM_WIDTH, BF16), (3 * d, BF16)]
    return pl.pallas_call(
        _inproj_kernel,
        out_shape=[jax.ShapeDtypeStruct((b, t, c), ty) for c, ty in widths],
        grid=(b, t // tm),
        in_specs=[tok(d), pl.BlockSpec((1, 6, d), lambda i, j: (i, 0, 0))] + [_const_spec(a.shape) for a in consts],
        out_specs=[tok(c) for c, _ in widths],
        compiler_params=_params("parallel", "parallel"),
        name="input_projection",
    )(x, mod6, *consts)


def _ssd_chunks(seqs, a_row, e2_ref, tri2_ref, reverse, head_off):
    q = seqs[0][0].shape[0]
    last = 0 if reverse else q - 1
    gw = SSD_INNER // SSD_GROUPS
    hpg = gw // SSD_HEAD_DIM
    groups = range(SSD_GROUPS)
    gcols = lambda g: slice(g * gw, (g + 1) * gw)
    ncols = lambda g: slice(g * SSD_STATE, (g + 1) * SSD_STATE)
    tri2 = tri2_ref[...]

    acums = []
    for xs, bm, cm, dt, h_ref in seqs:
        a = dt * a_row
        a1 = a.astype(BF16)
        r1 = a - a1.astype(F32)
        a2 = r1.astype(BF16)
        a3 = (r1 - a2.astype(F32)).astype(BF16)
        acums.append(_dot(tri2, jnp.concatenate([a1, a2], axis=0)) + _dot(tri2[:, :q], a3))

    wides = []
    for (xs, bm, cm, dt, h_ref), acum in zip(seqs, acums):
        a_last = acum[last:last + 1, :]
        per_head = jnp.concatenate([dt, dt * jnp.exp2(a_last - acum),
                                    jnp.broadcast_to(jnp.exp2(a_last), (8, LANES))], axis=0)
        hi, lo = _split_hi_lo(per_head)
        wides.append(_dot(jnp.concatenate([hi, lo], axis=1), e2_ref[...]))

    cbs, y_inters, bts, h_prevs = [], [], [], []
    for xs, bm, cm, dt, h_ref in seqs:
        cgbs = [cm[:, ncols(g)].astype(BF16) for g in groups]
        h_prev = [h_ref[:, gcols(g)] for g in groups]
        cbs.append([_dot_nt(cgbs[g], bm[:, ncols(g)].astype(BF16)) for g in groups])
        y_inters.append([_dot(cgbs[g], h_prev[g].astype(BF16)) for g in groups])
        bts.append([bm[:, ncols(g)].astype(F32).T.astype(BF16) for g in groups])
        h_prevs.append(h_prev)

    row = lax.broadcasted_iota(jnp.int32, (q, q), 0)
    col = lax.broadcasted_iota(jnp.int32, (q, q), 1)
    in_scan = (col >= row) if reverse else (col <= row)
    lane = lax.broadcasted_iota(jnp.int32, (q, LANES), 1)
    lane_head = lax.broadcasted_iota(jnp.int32, (q, gw), 1) // SSD_HEAD_DIM

    lhs, rhs, dec_outs, xw_bs, h_decs = [], [], [], [], []
    for si, (xs, bm, cm, dt, h_ref) in enumerate(seqs):
        acum, wide = acums[si], wides[si]
        acum_t = acum.T
        xdt_b = (xs * wide[0:q]).astype(BF16)
        xw_bs.append((xs * wide[q:2 * q]).astype(BF16))
        h_decs.append(wide[2 * q:2 * q + 1])
        for g in groups:
            xg = xdt_b[:, gcols(g)]
            ms, decs = [], []
            for j in range(hpg):
                hc = head_off + g * hpg + j
                colx = jnp.broadcast_to(acum[:, hc:hc + 1], (q, q))
                decay = jnp.exp2(jnp.where(in_scan, colx - acum_t[hc:hc + 1, :], NEG))
                ms.append((cbs[si][g] * decay).astype(BF16))
                decs.append(jnp.exp2(colx))
            lhs.append(jnp.concatenate(ms, axis=1))
            rhs.append(jnp.concatenate([jnp.where(lane_head == j, xg, jnp.zeros_like(xg)) for j in range(hpg)], axis=0))
            dec_outs.append(jnp.concatenate([jnp.where(lane < SSD_HEAD_DIM, decs[2 * i], decs[2 * i + 1])
                                             for i in range(hpg // 2)], axis=1))
    y_intra = [_dot(l, r) for l, r in zip(lhs, rhs)]
    h_add = [_dot(bts[si][g], xw_bs[si][:, gcols(g)]) for si in range(len(seqs)) for g in groups]
    ys = []
    for si, (xs, bm, cm, dt, h_ref) in enumerate(seqs):
        for g in groups:
            h_ref[:, gcols(g)] = h_prevs[si][g] * h_decs[si][:, gcols(g)] + h_add[si * SSD_GROUPS + g]
        ys.append(jnp.concatenate([y_intra[si * SSD_GROUPS + g] + y_inters[si][g] * dec_outs[si * SSD_GROUPS + g]
                                   for g in groups], axis=1))
    return ys


def _ssd_fwd_kernel(cur_ref, prev_ref, next_ref, dt_ref, cos_ref, sin_ref, cw_ref, cb_ref, shift_ref, a_ref, e_ref,
                    tri_ref, dsk_ref, h0_ref, xbc_o, y_o, hn_o, h_ref):
    i = pl.program_id(1)
    nc = pl.num_programs(1)
    nb, q, _ = cur_ref.shape

    @pl.when(i == 0)
    def _():
        h_ref[...] = h0_ref[...]

    pv = jnp.where(i == 0, 0.0, 1.0).astype(BF16)
    nv = jnp.where(i == nc - 1, 0.0, 1.0).astype(BF16)
    lane = lax.broadcasted_iota(jnp.int32, (q, LANES), 1)
    cos = cos_ref[...]
    sin = sin_ref[...]

    def rope(t):
        sw = jnp.where((lane & ROPE_FREQS) == 0, pltpu.roll(t, LANES - ROPE_FREQS, 1), pltpu.roll(t, ROPE_FREQS, 1))
        return t * cos + sw * sin

    cwid = 256
    seqs = []
    for s in range(nb):
        ext = jnp.concatenate([prev_ref[s] * pv, cur_ref[s], next_ref[s] * nv], axis=0)
        parts = []
        for c0 in range(0, SSD_CONV_DIM, cwid):
            e = ext[:, c0:c0 + cwid]
            taps = jnp.concatenate([e * cw_ref[k:k + 1, c0:c0 + cwid] for k in range(SSD_CONV)], axis=0)
            parts.append(_silu(_dot(shift_ref[...], taps) + cb_ref[:, c0:c0 + cwid]))
        xs = jnp.concatenate(parts[:SSD_INNER // cwid], axis=1)
        bc = jnp.concatenate(parts[SSD_INNER // cwid:], axis=1)
        bc = jnp.concatenate([rope(bc[:, g * LANES:(g + 1) * LANES]) for g in range(2 * SSD_GROUPS)], axis=1)
        xbc_o[s, :, :SSD_INNER] = xs.astype(BF16)
        xbc_o[s, :, SSD_INNER:] = bc.astype(BF16)
        seqs.append((xs, bc[:, :SSD_BC], bc[:, SSD_BC:], dt_ref[s], h_ref.at[s]))
    ys = _ssd_chunks(seqs, a_ref[...], e_ref, tri_ref, False, 0)
    for s in range(nb):
        y_o[s] = (ys[s] + dsk_ref[...] * seqs[s][0]).astype(BF16)

    @pl.when(i == nc - 1)
    def _():
        hn_o[...] = h_ref[...]


def _ssd_bwd_kernel(xbc_ref, dt_ref, yf_ref, a_ref, e_ref, tri_ref, h0_ref, y_o, hn_o, h_ref):
    i = pl.program_id(1)

    @pl.when(i == 0)
    def _():
        h_ref[...] = h0_ref[...]

    nb = xbc_ref.shape[0]
    seqs = [(xbc_ref[s, :, :SSD_INNER].astype(F32), xbc_ref[s, :, SSD_INNER:SSD_INNER + SSD_BC],
             xbc_ref[s, :, SSD_INNER + SSD_BC:], dt_ref[s], h_ref.at[s]) for s in range(nb)]
    ys = _ssd_chunks(seqs, a_ref[...], e_ref, tri_ref, True, SSD_HEADS)
    for s in range(nb):
        y_o[s] = (ys[s] + yf_ref[s].astype(F32)).astype(BF16)

    @pl.when(i == pl.num_programs(1) - 1)
    def _():
        hn_o[...] = h_ref[...]


def _ssd(xbc, dt, h0_f, h0_b, rope_cos, rope_sin, lw, consts):
    b, t, _ = xbc.shape
    q = SSD_CHUNK
    nc = t // q
    nb = SSD_SEQS_PER_STEP if b % SSD_SEQS_PER_STEP == 0 else 1
    hb = q // HALO
    n_hb = t // HALO
    chunk = lambda c: pl.BlockSpec((nb, q, c), lambda i, j: (i, j, 0))
    state = pl.BlockSpec((nb, SSD_STATE, SSD_INNER), lambda i, j: (i, 0, 0))
    state_shape = jax.ShapeDtypeStruct((b, SSD_STATE, SSD_INNER), F32)
    scratch_h = pltpu.VMEM((nb, SSD_STATE, SSD_INNER), F32)

    xbc_c, y_f, hn_f = pl.pallas_call(
        _ssd_fwd_kernel,
        out_shape=[jax.ShapeDtypeStruct((b, t, SSD_CONV_DIM), BF16), jax.ShapeDtypeStruct((b, t, SSD_INNER), BF16),
                   state_shape],
        grid=(b // nb, nc),
        in_specs=[chunk(SSD_CONV_DIM),
                  pl.BlockSpec((nb, HALO, SSD_CONV_DIM), lambda i, j: (i, jnp.maximum(j * hb - 1, 0), 0)),
                  pl.BlockSpec((nb, HALO, SSD_CONV_DIM), lambda i, j: (i, jnp.minimum((j + 1) * hb, n_hb - 1), 0)),
                  chunk(LANES),
                  pl.BlockSpec((q, LANES), lambda i, j: (j, 0)),
                  pl.BlockSpec((q, LANES), lambda i, j: (j, 0)),
                  _const_spec(lw["conv_w"].shape), _const_spec(lw["conv_b"].shape),
                  _const_spec(consts["conv_shift"].shape),
                  _const_spec(lw["a_fwd"].shape), _const_spec(consts["e_fwd"].shape),
                  _const_spec(consts["tri_fwd"].shape), _const_spec(lw["d_skip"].shape), state],
        out_specs=[chunk(SSD_CONV_DIM), chunk(SSD_INNER), state],
        scratch_shapes=[scratch_h],
        compiler_params=_params("parallel", "arbitrary"),
        name="ssd_forward",
    )(xbc, xbc, xbc, dt, rope_cos, rope_sin, lw["conv_w"], lw["conv_b"], consts["conv_shift"],
      lw["a_fwd"], consts["e_fwd"], consts["tri_fwd"], lw["d_skip"], h0_f)

    rchunk = lambda c: pl.BlockSpec((nb, q, c), lambda i, j: (i, nc - 1 - j, 0))
    y, hn_b = pl.pallas_call(
        _ssd_bwd_kernel,
        out_shape=[jax.ShapeDtypeStruct((b, t, SSD_INNER), BF16), state_shape],
        grid=(b // nb, nc),
        in_specs=[rchunk(SSD_CONV_DIM), rchunk(LANES), rchunk(SSD_INNER),
                  _const_spec(lw["a_bwd"].shape), _const_spec(consts["e_bwd"].shape),
                  _const_spec(consts["tri_bwd"].shape), state],
        out_specs=[rchunk(SSD_INNER), state],
        scratch_shapes=[scratch_h],
        compiler_params=_params("parallel", "arbitrary"),
        name="ssd_backward",
    )(xbc_c, dt, y_f, lw["a_bwd"], consts["e_bwd"], consts["tri_bwd"], h0_b)
    return y, hn_f, hn_b


def _bias_kernel(rpb_ref, o_ref):
    lh = pl.program_id(0)
    n_ri = 2 * NA_ROWS - 1
    n_ci = 2 * NA_COLS - 1
    lane = lax.broadcasted_iota(jnp.int32, (GRID_W, LANES), 1)
    qc = lax.broadcasted_iota(jnp.int32, (GRID_W, LANES), 0)
    kc = lane % GRID_W
    cs = jnp.clip(qc - NA_COLS // 2, 0, GRID_W - NA_COLS)
    col_ok = (kc >= cs) & (kc < cs + NA_COLS)
    ci = jnp.clip(kc - qc + (NA_COLS - 1), 0, n_ci - 1)
    tiles = []
    for ri in range(n_ri):
        base = (lh * n_ri + ri) * n_ci
        acc = jnp.full((GRID_W, LANES), NEG, F32)
        for c in range(n_ci):
            acc = jnp.where(col_ok & (ci == c), rpb_ref[base + c] * LOG2E, acc)
        tiles.append(acc)
    masked = jnp.full((GRID_W, LANES), NEG, F32)
    tiles = [masked] + tiles + [masked]
    for e in range(n_ri + 1):
        o_ref[0, e] = jnp.where(lane < GRID_W, tiles[e], tiles[e + 1])


def _bias_table(rpb):
    depth, heads, n_ri, n_ci = rpb.shape
    return pl.pallas_call(
        _bias_kernel,
        out_shape=jax.ShapeDtypeStruct((depth * heads, n_ri + 1, GRID_W, LANES), F32),
        grid=(depth * heads,),
        in_specs=[pl.BlockSpec(memory_space=pltpu.SMEM)],
        out_specs=pl.BlockSpec((1, n_ri + 1, GRID_W, LANES), lambda i: (i, 0, 0, 0)),
        compiler_params=pltpu.CompilerParams(dimension_semantics=("arbitrary",)),
        name="na_bias_table",
    )(rpb.reshape(-1))


def _na_tile(q_ref, k_ref, v_ref, kc_ref, vc_ref, bias_ref, o_ref, row0, start, nw, plan):
    tq = NA_QROWS * GRID_W
    qrows = slice(row0, row0 + tq)
    n_ctx = kc_ref.shape[1]
    n_cb = n_ctx // LANES
    lane_q = lax.broadcasted_iota(jnp.int32, (tq, LANES), 1)
    lane_k = lax.broadcasted_iota(jnp.int32, (n_ctx + nw * GRID_W, LANES), 1)
    n_pairs = NA_WIDTH // LANES
    own = [lambda lane, hh=hh: (lane < NA_HEAD_DIM) == (hh == 0) for hh in range(2)]
    col = lambda p: slice(p * LANES, (p + 1) * LANES)
    scores_all = []
    for p in range(n_pairs):
        qp = q_ref[0, qrows, col(p)]
        keys = jnp.concatenate([kc_ref[0, :, col(p)], k_ref[0, pl.ds(start, nw * GRID_W), col(p)]], axis=0)
        for hh in range(2):
            scores_all.append(_dot_nt(jnp.where(own[hh](lane_q), qp, jnp.zeros_like(qp)), keys))
    p_mats = []
    for h, scores in enumerate(scores_all):
        p_rows = []
        for qi, row_plan in enumerate(plan):
            rows = slice(qi * GRID_W, (qi + 1) * GRID_W)
            blocks = [scores[rows, m * LANES:(m + 1) * LANES] for m in range(n_cb)]
            for m, (ent, ok) in enumerate(row_plan):
                if ok is None:
                    blocks.append(None)
                    continue
                sb = scores[rows, (n_cb + m) * LANES:(n_cb + m + 1) * LANES] + bias_ref[h, ent]
                blocks.append(sb if ok is True else jnp.where(ok, sb, NEG))
            live = [sb for sb in blocks if sb is not None]
            mx = live[0]
            for sb in live[1:]:
                mx = jnp.maximum(mx, sb)
            mx = jnp.max(mx, axis=-1, keepdims=True)
            p_rows.append(jnp.concatenate(
                [jnp.zeros((GRID_W, LANES), BF16) if sb is None else jnp.exp2(sb - mx).astype(BF16)
                 for sb in blocks], axis=1))
        p_mats.append(jnp.concatenate(p_rows, axis=0))
    for p in range(n_pairs):
        vals = jnp.concatenate([vc_ref[0, :, col(p)], v_ref[0, pl.ds(start, nw * GRID_W), col(p)]], axis=0)
        nums = [_dot(p_mats[2 * p + hh], jnp.where(own[hh](lane_k), vals, jnp.ones_like(vals))) for hh in range(2)]
        num = jnp.where(lane_q < NA_HEAD_DIM, nums[0], nums[1])
        den = pltpu.roll(jnp.where(lane_q < NA_HEAD_DIM, nums[1], nums[0]), NA_HEAD_DIM, 1)
        o_ref[0, qrows, col(p)] = (num / den).astype(BF16)


def _na_kernel(q_ref, k_ref, v_ref, kc_ref, vc_ref, bias_ref, o_ref, *, nw, rows_n):
    rt = NA_QROWS
    half = NA_ROWS // 2
    lane_r = lax.broadcasted_iota(jnp.int32, (GRID_W, LANES), 1)
    first_half = lane_r < GRID_W
    nblk = nw // 2
    args = (q_ref, k_ref, v_ref, kc_ref, vc_ref, bias_ref, o_ref)

    def interior_tile(r0, row0):
        plan = []
        for qi in range(rt):
            row_plan = []
            for m in range(nblk):
                ok0 = qi <= 2 * m < qi + NA_ROWS
                ok1 = qi <= 2 * m + 1 < qi + NA_ROWS
                ok = True if ok0 and ok1 else None if not (ok0 or ok1) else first_half if ok0 else ~first_half
                row_plan.append((half + 2 * m - qi, ok))
            plan.append(row_plan)
        _na_tile(*args, row0, pl.multiple_of((r0 - half) * GRID_W, GRID_W), nw, plan)

    def clipped_tile(r0, row0):
        base = jnp.clip(r0 - half, 0, rows_n - nw)
        plan = []
        for qi in range(rt):
            r = r0 + qi
            rs = jnp.clip(r - half, 0, rows_n - NA_ROWS)
            row_plan = []
            for m in range(nblk):
                j0 = base + 2 * m
                ok0 = (j0 >= rs) & (j0 < rs + NA_ROWS)
                ok1 = (j0 + 1 >= rs) & (j0 + 1 < rs + NA_ROWS)
                ok = jnp.where(first_half, ok0.astype(jnp.int32), ok1.astype(jnp.int32)) > 0
                row_plan.append((jnp.clip(j0 - r + NA_ROWS, 0, 2 * NA_ROWS - 1), ok))
            plan.append(row_plan)
        _na_tile(*args, row0, pl.multiple_of(base * GRID_W, GRID_W), nw, plan)

    for sub in range(NA_TILES_PER_STEP):
        r0 = (pl.program_id(1) * NA_TILES_PER_STEP + sub) * rt
        interior = (r0 >= half) & (r0 - half <= rows_n - nw)
        pl.when(interior)(functools.partial(interior_tile, r0, sub * rt * GRID_W))
        pl.when(~interior)(functools.partial(clipped_tile, r0, sub * rt * GRID_W))


def _neighbourhood_attention(q, k, v, kc, vc, bias, layer):
    b, t, w = q.shape
    n_ctx = kc.shape[1]
    rows_n = t // GRID_W
    nw = NA_QROWS + NA_ROWS
    nw += nw % 2
    tq = NA_TILES_PER_STEP * NA_QROWS * GRID_W
    whole = lambda n: pl.BlockSpec((1, n, w), lambda i, j: (i, 0, 0))
    return pl.pallas_call(
        functools.partial(_na_kernel, nw=nw, rows_n=rows_n),
        out_shape=jax.ShapeDtypeStruct((b, t, w), BF16),
        grid=(b, t // tq),
        in_specs=[pl.BlockSpec((1, tq, w), lambda i, j: (i, j, 0)), whole(t), whole(t), whole(n_ctx), whole(n_ctx),
                  pl.BlockSpec((NA_HEADS,) + bias.shape[1:], lambda i, j: (layer, 0, 0, 0),
                               pipeline_mode=pl.Buffered(1))],
        out_specs=pl.BlockSpec((1, tq, w), lambda i, j: (i, j, 0)),
        compiler_params=_params("parallel", "arbitrary"),
        name="neighbourhood_attention",
    )(q, k, v, kc, vc, bias)


def _ctx_attn_kernel(q_ref, k_ref, v_ref, o_ref):
    n = q_ref.shape[1]
    lane = lax.broadcasted_iota(jnp.int32, (n, LANES), 1)
    for p in range(NA_WIDTH // LANES):
        cols = slice(p * LANES, (p + 1) * LANES)
        qp = q_ref[0, :, cols]
        kp = k_ref[0, :, cols]
        vp = v_ref[0, :, cols]
        nums = []
        for hh in range(2):
            own = (lane < NA_HEAD_DIM) == (hh == 0)
            s = _dot_nt(jnp.where(own, qp, jnp.zeros_like(qp)), kp)
            pm = jnp.exp2(s - jnp.max(s, axis=-1, keepdims=True)).astype(BF16)
            nums.append(_dot(pm, jnp.where(own, vp, jnp.ones_like(vp))))
        num = jnp.where(lane < NA_HEAD_DIM, nums[0], nums[1])
        den = pltpu.roll(jnp.where(lane < NA_HEAD_DIM, nums[1], nums[0]), NA_HEAD_DIM, 1)
        o_ref[0, :, cols] = (num / den).astype(BF16)


def _context_attention(q, k, v):
    b, n, w = q.shape
    spec = pl.BlockSpec((1, n, w), lambda i: (i, 0, 0))
    return pl.pallas_call(
        _ctx_attn_kernel,
        out_shape=jax.ShapeDtypeStruct((b, n, w), BF16),
        grid=(b,),
        in_specs=[spec, spec, spec],
        out_specs=spec,
        compiler_params=_params("parallel"),
        name="context_attention",
    )(q, k, v)


def _merge_ffn_kernel(x_ref, mod_ref, gate_ref, yssd_ref, z_ref, yna_ref, ygm_ref, sn_ref, wa, wb, wc, wo,
                      n2_ref, wfi, wfo, o_ref, *, ffn_chunk):
    d = x_ref.shape[2]
    y = yssd_ref[0].astype(F32) * _silu(z_ref[0].astype(F32))
    y = (y * lax.rsqrt(jnp.mean(y * y, axis=-1, keepdims=True) + EPS) * sn_ref[...]).astype(BF16)
    mixed = (gate_ref[0, :, 0:d].astype(F32) * _dot(y, wa[...])
             + gate_ref[0, :, d:2 * d].astype(F32) * _dot(yna_ref[0], wb[...])
             + gate_ref[0, :, 2 * d:3 * d].astype(F32) * _dot(ygm_ref[0], wc[...]))
    x1 = x_ref[0] + mod_ref[0, 2:3, :] * _dot(mixed.astype(BF16), wo[...])
    xn = x1 * lax.rsqrt(jnp.mean(x1 * x1, axis=-1, keepdims=True) + EPS) * n2_ref[...]
    hb = (xn * (1.0 + mod_ref[0, 4:5, :]) + mod_ref[0, 3:4, :]).astype(BF16)
    hid = wfo.shape[0]
    acc = jnp.zeros_like(x1)
    for c0 in range(0, hid, ffn_chunk):
        a = _dot(hb, wfi[:, c0:c0 + ffn_chunk])
        g = _dot(hb, wfi[:, hid + c0:hid + c0 + ffn_chunk])
        acc = acc + _dot((_silu(a) * g).astype(BF16), wfo[c0:c0 + ffn_chunk, :])
    o_ref[0] = x1 + mod_ref[0, 5:6, :] * acc


def _merge_ffn(x, mod6, gate, y_ssd, z, y_na, y_gm, lw):
    b, t, d = x.shape
    tm = min(TOKEN_TILE, t)
    tok = lambda c: pl.BlockSpec((1, tm, c), lambda i, j: (i, j, 0))
    consts = [lw["ssd_norm"], lw["w_branch_ssd"], lw["w_branch_na"], lw["w_branch_gm"], lw["w_out"], lw["norm2"],
              lw["w_ffn_in"], lw["w_ffn_out"]]
    return pl.pallas_call(
        functools.partial(_merge_ffn_kernel, ffn_chunk=256),
        out_shape=jax.ShapeDtypeStruct((b, t, d), F32),
        grid=(b, t // tm),
        in_specs=[tok(d), pl.BlockSpec((1, 6, d), lambda i, j: (i, 0, 0)),
                  tok(3 * d), tok(SSD_INNER), tok(SSD_INNER), tok(NA_WIDTH), tok(GM_WIDTH)]
                 + [_const_spec(a.shape) for a in consts],
        out_specs=tok(d),
        compiler_params=_params("parallel", "parallel"),
        name="merge_out_ffn",
    )(x, mod6, gate, y_ssd, z, y_na, y_gm, *consts)


def _cast_kernel(w_ref, o_ref):
    o_ref[...] = w_ref[0].astype(BF16)


def _weight_bf16(w, l):
    _, rows, cols = w.shape
    rb = next(r for r in (rows, rows // 2, rows // 4, rows // 8) if r * cols * 4 <= (6 << 20) and r % 16 == 0)
    return pl.pallas_call(
        _cast_kernel,
        out_shape=jax.ShapeDtypeStruct((rows, cols), BF16),
        grid=(rows // rb,),
        in_specs=[pl.BlockSpec((1, rb, cols), lambda i: (l, i, 0))],
        out_specs=pl.BlockSpec((rb, cols), lambda i: (i, 0)),
        compiler_params=_params("parallel"),
        name="weight_to_bf16",
    )(w)


def _shared_constants(n_ctx, seq):
    pos = jnp.arange(seq)
    freqs = ROPE_BASE ** (-jnp.arange(ROPE_FREQS, dtype=F32) / ROPE_FREQS)
    ang_row = (pos // GRID_W).astype(F32)[:, None] * freqs
    ang_col = (pos % GRID_W).astype(F32)[:, None] * freqs
    cos = jnp.concatenate([jnp.cos(ang_row), jnp.cos(ang_row), jnp.cos(ang_col), jnp.cos(ang_col)], axis=1)
    sin = jnp.concatenate([-jnp.sin(ang_row), jnp.sin(ang_row), -jnp.sin(ang_col), jnp.sin(ang_col)], axis=1)

    r = jnp.arange(SSD_CHUNK)
    twice = lambda m, axis: jnp.concatenate([m, m], axis=axis).astype(BF16)
    lane_head = jnp.arange(SSD_INNER) // SSD_HEAD_DIM
    rows = jnp.arange(LANES)
    src = jnp.arange(SSD_CHUNK + 2 * HALO)
    conv_shift = jnp.concatenate([(src[None, :] == r[:, None] + HALO + k - SSD_CONV // 2) for k in range(SSD_CONV)],
                                 axis=1).astype(BF16)
    return dict(rope_cos=cos, rope_sin=sin,
                ctx_cos=jnp.ones((n_ctx, LANES), F32), ctx_sin=jnp.zeros((n_ctx, LANES), F32),
                tri_fwd=twice(r[None, :] <= r[:, None], 1), tri_bwd=twice(r[None, :] >= r[:, None], 1),
                e_fwd=twice(rows[:, None] == lane_head[None, :], 0),
                e_bwd=twice(rows[:, None] == lane_head[None, :] + SSD_HEADS, 0),
                conv_shift=conv_shift)


def _layer_weights(l, p):
    d = p["w_in"].shape[1]
    sizes = (SSD_INNER, SSD_CONV_DIM, 2 * SSD_HEADS, NA_WIDTH, NA_WIDTH, NA_WIDTH, 2 * GM_WIDTH, 3 * d)
    names = ("w_z", "w_xbc", "w_dt", "w_q", "w_k", "w_v", "w_uv", "w_gate")
    lw, start = {}, 0
    w_in = p["w_in"][l]
    for name, size in zip(names, sizes):
        lw[name] = w_in[:, start:start + size].astype(BF16)
        start += size
    pad_lanes = lambda v: jnp.pad(v, (0, LANES - v.shape[0])).reshape(1, LANES)
    lw["w_dt"] = jnp.pad(lw["w_dt"], ((0, 0), (0, LANES - 2 * SSD_HEADS)))
    lw["dt_bias"] = pad_lanes(p["dt_bias"][l].reshape(-1))
    a = -jnp.exp(p["a_log"][l].astype(F32))
    lw["a_fwd"] = pad_lanes(a[0] * LOG2E)
    lw["a_bwd"] = pad_lanes(jnp.concatenate([jnp.zeros((SSD_HEADS,), F32), a[1] * LOG2E]))
    row = lambda v: v.reshape(1, -1)
    lw["norm1"] = row(p["norm1"][l])
    lw["norm2"] = row(p["norm2"][l])
    lw["b_gate"] = row(p["b_gate"][l])
    lw["q_norm"] = row(jnp.tile(p["q_norm"][l], NA_HEADS))
    lw["k_norm"] = row(jnp.tile(p["k_norm"][l], NA_HEADS))
    head = jnp.arange(NA_WIDTH) // NA_HEAD_DIM
    lw["head_blk"] = ((head[:, None] == head[None, :]).astype(F32) / NA_HEAD_DIM).astype(BF16)
    lw["gm_norm"] = row(p["gm_norm"][l])
    w_s = p["w_spatial"][l].astype(BF16)
    lw["w_spatial"] = jnp.concatenate([w_s[0::2], w_s[1::2]], axis=2)
    lw["b_spatial"] = jnp.repeat(p["b_spatial"][l].T, GM_WIDTH // GM_GROUPS, axis=1)
    lw["conv_w"] = jnp.pad(p["conv_w"][l], ((0, 8 - SSD_CONV), (0, 0))).astype(BF16)
    lw["conv_b"] = row(p["conv_b"][l])
    lw["d_skip"] = row(jnp.repeat(p["d_skip"][l], SSD_HEAD_DIM))
    lw["ssd_norm"] = row(p["ssd_norm"][l])
    for name in ("w_branch_ssd", "w_branch_na", "w_branch_gm", "w_out", "w_ffn_in", "w_ffn_out"):
        lw[name] = _weight_bf16(p[name], l)
    return lw


def kernel(x, c, ctx, c_ctx, w_mod, b_mod, norm1, w_in, b_gate, conv_w, conv_b, a_log, dt_bias, d_skip, ssd_norm,
           q_norm, k_norm, rpb, gm_norm, w_spatial, b_spatial, w_branch_ssd, w_branch_na, w_branch_gm, w_out,
           norm2, w_ffn_in, w_ffn_out):
    p = dict(norm1=norm1, w_in=w_in, b_gate=b_gate, conv_w=conv_w, conv_b=conv_b, a_log=a_log, dt_bias=dt_bias,
             d_skip=d_skip, ssd_norm=ssd_norm, q_norm=q_norm, k_norm=k_norm, gm_norm=gm_norm, w_spatial=w_spatial,
             b_spatial=b_spatial, w_branch_ssd=w_branch_ssd, w_branch_na=w_branch_na, w_branch_gm=w_branch_gm,
             w_out=w_out, norm2=norm2, w_ffn_in=w_ffn_in, w_ffn_out=w_ffn_out)
    b, seq, d = x.shape
    n_ctx = ctx.shape[1]
    depth = w_mod.shape[0]

    c_all = jnp.zeros((8, d), F32).at[:b].set(c).at[b].set(c_ctx)
    mod = _modulation(c_all, w_mod, b_mod)
    bias = _bias_table(rpb)
    consts = _shared_constants(n_ctx, seq)
    zero_state = jnp.zeros((b, SSD_STATE, SSD_INNER), F32)

    xc = ctx
    for l in range(depth):
        lw = _layer_weights(l, p)
        mod_x = mod[l, :b].reshape(b, 6, d)
        mod_c = mod[l, b].reshape(1, 6, d)
        last = l == depth - 1

        flat = lambda t: t.reshape(1, b * n_ctx, t.shape[-1])
        per_sample = lambda t: t.reshape(b, n_ctx, t.shape[-1])
        zc, xbcc, dtc, qc, kc, vc, ygm_c, gate_c = [per_sample(t) for t in _input_projection(flat(xc), mod_c, lw)]
        z, xbc, dt, q, k, v, y_gm, gate = _input_projection(x, mod_x, lw)

        yssd_c, s_f, s_b = _ssd(xbcc, dtc, zero_state, zero_state, consts["ctx_cos"], consts["ctx_sin"], lw, consts)
        y_ssd, _, _ = _ssd(xbc, dt, s_f, s_b, consts["rope_cos"], consts["rope_sin"], lw, consts)

        y_na = _neighbourhood_attention(q, k, v, kc, vc, bias, l)
        x = _merge_ffn(x, mod_x, gate, y_ssd, z, y_na, y_gm, lw)
        if not last:
            yna_c = _context_attention(qc, kc, vc)
            xc = per_sample(_merge_ffn(flat(xc), mod_c, flat(gate_c), flat(yssd_c), flat(zc), flat(yna_c), flat(ygm_c), lw))
    return x
```

```python
import functools
import math

import jax
import jax.numpy as jnp
from jax import lax
from jax.experimental import pallas as pl
from jax.experimental.pallas import tpu as pltpu

F32 = jnp.float32
BF16 = jnp.bfloat16

EPS = 1e-6
GRID_W = 64

SSD_INNER = 1024
SSD_HEAD_DIM = 64
SSD_HEADS = 16
SSD_GROUPS = 4
SSD_STATE = 128
SSD_CONV = 5
SSD_CHUNK = 128
SSD_BC = SSD_GROUPS * SSD_STATE
SSD_CONV_DIM = SSD_INNER + 2 * SSD_BC
ROPE_FREQS = 32
ROPE_BASE = 10000.0

NA_HEAD_DIM = 64
NA_WIDTH = 512
NA_HEADS = 8
NA_ROWS = 8
NA_COLS = 16
NA_QROWS = 4
NA_TILES_PER_STEP = 4

GM_WIDTH = 512
GM_GROUPS = 8
GM_CHUNK = 128

LANES = 128
HALO = 16
NEG = -1e30
LOG2E = math.log2(math.e)
VMEM_LIMIT = 56 * 1024 * 1024
TOKEN_TILE = 512
SSD_SEQS_PER_STEP = 4


def _dot(a, b):
    return jnp.dot(a, b, preferred_element_type=F32)


def _dot_nt(a, b):
    return lax.dot_general(a, b, (((1,), (1,)), ((), ())), preferred_element_type=F32)


def _silu(x):
    return x / (1.0 + jnp.exp(-x))


def _sigmoid(x):
    return 1.0 / (1.0 + jnp.exp(-x))


def _gelu_tanh(x):
    return 0.5 * x * (1.0 + jnp.tanh(math.sqrt(2.0 / math.pi) * (x + 0.044715 * (x * x * x))))


def _softplus(x):
    return jnp.maximum(x, 0.0) + jnp.log(1.0 + jnp.exp(-jnp.abs(x)))


def _split_hi_lo(v):
    hi = v.astype(BF16)
    lo = (v - hi.astype(F32)).astype(BF16)
    return hi, lo


def _const_spec(shape):
    nd = len(shape)
    return pl.BlockSpec(shape, lambda *_: (0,) * nd, pipeline_mode=pl.Buffered(1))


def _params(*semantics):
    return pltpu.CompilerParams(dimension_semantics=semantics, vmem_limit_bytes=VMEM_LIMIT)


def _mod_kernel(c_ref, w_ref, b_ref, o_ref):
    o_ref[0] = _dot(_silu(c_ref[...]), w_ref[0]) + b_ref[0]


def _modulation(c_all, w_mod, b_mod):
    depth, d, n = w_mod.shape
    tn = 1536
    return pl.pallas_call(
        _mod_kernel,
        out_shape=jax.ShapeDtypeStruct((depth, 8, n), F32),
        grid=(depth, n // tn),
        in_specs=[pl.BlockSpec((8, d), lambda l, j: (0, 0)),
                  pl.BlockSpec((1, d, tn), lambda l, j: (l, 0, j)),
                  pl.BlockSpec((1, 1, tn), lambda l, j: (l, 0, j))],
        out_specs=pl.BlockSpec((1, 8, tn), lambda l, j: (l, 0, j)),
        compiler_params=_params("arbitrary", "arbitrary"),
        name="modulation",
    )(c_all, w_mod, b_mod.reshape(depth, 1, n))


def _inproj_kernel(x_ref, mod_ref, n1_ref, wz, wxbc, wdt, wq, wk, wv, wuv, wg, dtb, bg, qn, kn, blk,
                   gmn, ws, bsp, z_o, xbc_o, dt_o, q_o, k_o, v_o, ygm_o, gate_o):
    tm = x_ref.shape[1]
    x = x_ref[0]
    xn = x * lax.rsqrt(jnp.mean(x * x, axis=-1, keepdims=True) + EPS) * n1_ref[...]
    hb = (xn * (1.0 + mod_ref[0, 1:2, :]) + mod_ref[0, 0:1, :]).astype(BF16)

    cw = 512

    def head_norm(t, w_row):
        ms = _dot((t * t).astype(BF16), blk[...])
        return t * lax.rsqrt(ms + EPS) * w_row

    qf = _dot(hb, wq[...])
    kf = _dot(hb, wk[...])
    g = _gelu_tanh(_dot(hb, wuv[...]))
    dt_o[0] = _softplus(_dot(hb, wdt[...]) + dtb[...])
    v_o[0] = _dot(hb, wv[...]).astype(BF16)
    for n0 in range(0, z_o.shape[2], cw):
        z_o[0, :, n0:n0 + cw] = _dot(hb, wz[:, n0:n0 + cw]).astype(BF16)

    q_o[0] = (head_norm(qf, qn[...]) * (NA_HEAD_DIM ** -0.5 * LOG2E)).astype(BF16)
    k_o[0] = head_norm(kf, kn[...]).astype(BF16)
    for n0 in range(0, xbc_o.shape[2], cw):
        xbc_o[0, :, n0:n0 + cw] = _dot(hb, wxbc[:, n0:n0 + cw]).astype(BF16)

    u = g[:, :GM_WIDTH]
    v = g[:, GM_WIDTH:]
    vb = (v * lax.rsqrt(jnp.mean(v * v, axis=-1, keepdims=True) + EPS) * gmn[...]).astype(BF16)
    first = lax.broadcasted_iota(jnp.int32, (GM_CHUNK, LANES), 1) < LANES // 2
    for c0 in range(0, tm, GM_CHUNK):
        for p in range(GM_WIDTH // LANES):
            cols = slice(p * LANES, (p + 1) * LANES)
            vp = vb[c0:c0 + GM_CHUNK, cols]
            zero = jnp.zeros_like(vp)
            stacked = jnp.concatenate([jnp.where(first, vp, zero), jnp.where(first, zero, vp)], axis=0)
            mixed = _dot(ws[p], stacked) + bsp[:, cols]
            ygm_o[0, c0:c0 + GM_CHUNK, cols] = (u[c0:c0 + GM_CHUNK, cols] * mixed).astype(BF16)

    for n0 in range(0, gate_o.shape[2], cw):
        gate_o[0, :, n0:n0 + cw] = _sigmoid(_dot(hb, wg[:, n0:n0 + cw]) + bg[:, n0:n0 + cw]).astype(BF16)


def _input_projection(x, mod6, lw):
    b, t, d = x.shape
    tm = min(TOKEN_TILE, t)
    tok = lambda c: pl.BlockSpec((1, tm, c), lambda i, j: (i, j, 0))
    consts = [lw["norm1"], lw["w_z"], lw["w_xbc"], lw["w_dt"], lw["w_q"], lw["w_k"], lw["w_v"], lw["w_uv"],
              lw["w_gate"], lw["dt_bias"], lw["b_gate"], lw["q_norm"], lw["k_norm"], lw["head_blk"],
              lw["gm_norm"], lw["w_spatial"], lw["b_spatial"]]
    widths = [(SSD_INNER, BF16), (SSD_CONV_DIM, BF16), (LANES, F32), (NA_WIDTH, BF16), (NA_WIDTH, BF16),
              (NA_WIDTH, BF16), (GM_WIDTH, BF16), (3 * d, BF16)]
    return pl.pallas_call(
        _inproj_kernel,
        out_shape=[jax.ShapeDtypeStruct((b, t, c), ty) for c, ty in widths],
        grid=(b, t // tm),
        in_specs=[tok(d), pl.BlockSpec((1, 6, d), lambda i, j: (i, 0, 0))] + [_const_spec(a.shape) for a in consts],
        out_specs=[tok(c) for c, _ in widths],
        compiler_params=_params("parallel", "parallel"),
        name="input_projection",
    )(x, mod6, *consts)


def _ssd_chunks(seqs, a_row, e2_ref, tri2_ref, reverse, head_off):
    q = seqs[0][0].shape[0]
    last = 0 if reverse else q - 1
    gw = SSD_INNER // SSD_GROUPS
    hpg = gw // SSD_HEAD_DIM
    groups = range(SSD_GROUPS)
    gcols = lambda g: slice(g * gw, (g + 1) * gw)
    ncols = lambda g: slice(g * SSD_STATE, (g + 1) * SSD_STATE)
    tri2 = tri2_ref[...]

    acums = []
    for xs, bm, cm, dt, h_ref in seqs:
        a = dt * a_row
        a1 = a.astype(BF16)
        r1 = a - a1.astype(F32)
        a2 = r1.astype(BF16)
        a3 = (r1 - a2.astype(F32)).astype(BF16)
        acums.append(_dot(tri2, jnp.concatenate([a1, a2], axis=0)) + _dot(tri2[:, :q], a3))

    wides = []
    for (xs, bm, cm, dt, h_ref), acum in zip(seqs, acums):
        a_last = acum[last:last + 1, :]
        per_head = jnp.concatenate([dt, dt * jnp.exp2(a_last - acum),
                                    jnp.broadcast_to(jnp.exp2(a_last), (8, LANES))], axis=0)
        hi, lo = _split_hi_lo(per_head)
        wides.append(_dot(jnp.concatenate([hi, lo], axis=1), e2_ref[...]))

    cbs, y_inters, bts, h_prevs = [], [], [], []
    for xs, bm, cm, dt, h_ref in seqs:
        cgbs = [cm[:, ncols(g)].astype(BF16) for g in groups]
        h_prev = [h_ref[:, gcols(g)] for g in groups]
        cbs.append([_dot_nt(cgbs[g], bm[:, ncols(g)].astype(BF16)) for g in groups])
        y_inters.append([_dot(cgbs[g], h_prev[g].astype(BF16)) for g in groups])
        bts.append([bm[:, ncols(g)].astype(F32).T.astype(BF16) for g in groups])
        h_prevs.append(h_prev)

    row = lax.broadcasted_iota(jnp.int32, (q, q), 0)
    col = lax.broadcasted_iota(jnp.int32, (q, q), 1)
    in_scan = (col >= row) if reverse else (col <= row)
    lane = lax.broadcasted_iota(jnp.int32, (q, LANES), 1)
    lane_head = lax.broadcasted_iota(jnp.int32, (q, gw), 1) // SSD_HEAD_DIM

    lhs, rhs, dec_outs, xw_bs, h_decs = [], [], [], [], []
    for si, (xs, bm, cm, dt, h_ref) in enumerate(seqs):
        acum, wide = acums[si], wides[si]
        acum_t = acum.T
        xdt_b = (xs * wide[0:q]).astype(BF16)
        xw_bs.append((xs * wide[q:2 * q]).astype(BF16))
        h_decs.append(wide[2 * q:2 * q + 1])
        for g in groups:
            xg = xdt_b[:, gcols(g)]
            ms, decs = [], []
            for j in range(hpg):
                hc = head_off + g * hpg + j
                colx = jnp.broadcast_to(acum[:, hc:hc + 1], (q, q))
                decay = jnp.exp2(jnp.where(in_scan, colx - acum_t[hc:hc + 1, :], NEG))
                ms.append((cbs[si][g] * decay).astype(BF16))
                decs.append(jnp.exp2(colx))
            lhs.append(jnp.concatenate(ms, axis=1))
            rhs.append(jnp.concatenate([jnp.where(lane_head == j, xg, jnp.zeros_like(xg)) for j in range(hpg)], axis=0))
            dec_outs.append(jnp.concatenate([jnp.where(lane < SSD_HEAD_DIM, decs[2 * i], decs[2 * i + 1])
                                             for i in range(hpg // 2)], axis=1))
    y_intra = [_dot(l, r) for l, r in zip(lhs, rhs)]
    h_add = [_dot(bts[si][g], xw_bs[si][:, gcols(g)]) for si in range(len(seqs)) for g in groups]
    ys = []
    for si, (xs, bm, cm, dt, h_ref) in enumerate(seqs):
        for g in groups:
            h_ref[:, gcols(g)] = h_prevs[si][g] * h_decs[si][:, gcols(g)] + h_add[si * SSD_GROUPS + g]
        ys.append(jnp.concatenate([y_intra[si * SSD_GROUPS + g] + y_inters[si][g] * dec_outs[si * SSD_GROUPS + g]
                                   for g in groups], axis=1))
    return ys


def _ssd_fwd_kernel(cur_ref, prev_ref, next_ref, dt_ref, cos_ref, sin_ref, cw_ref, cb_ref, shift_ref, a_ref, e_ref,
                    tri_ref, dsk_ref, h0_ref, xbc_o, y_o, hn_o, h_ref):
    i = pl.program_id(1)
    nc = pl.num_programs(1)
    nb, q, _ = cur_ref.shape

    @pl.when(i == 0)
    def _():
        h_ref[...] = h0_ref[...]

    pv = jnp.where(i == 0, 0.0, 1.0).astype(BF16)
    nv = jnp.where(i == nc - 1, 0.0, 1.0).astype(BF16)
    lane = lax.broadcasted_iota(jnp.int32, (q, LANES), 1)
    cos = cos_ref[...]
    sin = sin_ref[...]

    def rope(t):
        sw = jnp.where((lane & ROPE_FREQS) == 0, pltpu.roll(t, LANES - ROPE_FREQS, 1), pltpu.roll(t, ROPE_FREQS, 1))
        return t * cos + sw * sin

    cwid = 256
    seqs = []
    for s in range(nb):
        ext = jnp.concatenate([prev_ref[s] * pv, cur_ref[s], next_ref[s] * nv], axis=0)
        parts = []
        for c0 in range(0, SSD_CONV_DIM, cwid):
            e = ext[:, c0:c0 + cwid]
            taps = jnp.concatenate([e * cw_ref[k:k + 1, c0:c0 + cwid] for k in range(SSD_CONV)], axis=0)
            parts.append(_silu(_dot(shift_ref[...], taps) + cb_ref[:, c0:c0 + cwid]))
        xs = jnp.concatenate(parts[:SSD_INNER // cwid], axis=1)
        bc = jnp.concatenate(parts[SSD_INNER // cwid:], axis=1)
        bc = jnp.concatenate([rope(bc[:, g * LANES:(g + 1) * LANES]) for g in range(2 * SSD_GROUPS)], axis=1)
        xbc_o[s, :, :SSD_INNER] = xs.astype(BF16)
        xbc_o[s, :, SSD_INNER:] = bc.astype(BF16)
        seqs.append((xs, bc[:, :SSD_BC], bc[:, SSD_BC:], dt_ref[s], h_ref.at[s]))
    ys = _ssd_chunks(seqs, a_ref[...], e_ref, tri_ref, False, 0)
    for s in range(nb):
        y_o[s] = (ys[s] + dsk_ref[...] * seqs[s][0]).astype(BF16)

    @pl.when(i == nc - 1)
    def _():
        hn_o[...] = h_ref[...]


def _ssd_bwd_kernel(xbc_ref, dt_ref, yf_ref, a_ref, e_ref, tri_ref, h0_ref, y_o, hn_o, h_ref):
    i = pl.program_id(1)

    @pl.when(i == 0)
    def _():
        h_ref[...] = h0_ref[...]

    nb = xbc_ref.shape[0]
    seqs = [(xbc_ref[s, :, :SSD_INNER].astype(F32), xbc_ref[s, :, SSD_INNER:SSD_INNER + SSD_BC],
             xbc_ref[s, :, SSD_INNER + SSD_BC:], dt_ref[s], h_ref.at[s]) for s in range(nb)]
    ys = _ssd_chunks(seqs, a_ref[...], e_ref, tri_ref, True, SSD_HEADS)
    for s in range(nb):
        y_o[s] = (ys[s] + yf_ref[s].astype(F32)).astype(BF16)

    @pl.when(i == pl.num_programs(1) - 1)
    def _():
        hn_o[...] = h_ref[...]


def _ssd(xbc, dt, h0_f, h0_b, rope_cos, rope_sin, lw, consts):
    b, t, _ = xbc.shape
    q = SSD_CHUNK
    nc = t // q
    nb = SSD_SEQS_PER_STEP if b % SSD_SEQS_PER_STEP == 0 else 1
    hb = q // HALO
    n_hb = t // HALO
    chunk = lambda c: pl.BlockSpec((nb, q, c), lambda i, j: (i, j, 0))
    state = pl.BlockSpec((nb, SSD_STATE, SSD_INNER), lambda i, j: (i, 0, 0))
    state_shape = jax.ShapeDtypeStruct((b, SSD_STATE, SSD_INNER), F32)
    scratch_h = pltpu.VMEM((nb, SSD_STATE, SSD_INNER), F32)

    xbc_c, y_f, hn_f = pl.pallas_call(
        _ssd_fwd_kernel,
        out_shape=[jax.ShapeDtypeStruct((b, t, SSD_CONV_DIM), BF16), jax.ShapeDtypeStruct((b, t, SSD_INNER), BF16),
                   state_shape],
        grid=(b // nb, nc),
        in_specs=[chunk(SSD_CONV_DIM),
                  pl.BlockSpec((nb, HALO, SSD_CONV_DIM), lambda i, j: (i, jnp.maximum(j * hb - 1, 0), 0)),
                  pl.BlockSpec((nb, HALO, SSD_CONV_DIM), lambda i, j: (i, jnp.minimum((j + 1) * hb, n_hb - 1), 0)),
                  chunk(LANES),
                  pl.BlockSpec((q, LANES), lambda i, j: (j, 0)),
                  pl.BlockSpec((q, LANES), lambda i, j: (j, 0)),
                  _const_spec(lw["conv_w"].shape), _const_spec(lw["conv_b"].shape),
                  _const_spec(consts["conv_shift"].shape),
                  _const_spec(lw["a_fwd"].shape), _const_spec(consts["e_fwd"].shape),
                  _const_spec(consts["tri_fwd"].shape), _const_spec(lw["d_skip"].shape), state],
        out_specs=[chunk(SSD_CONV_DIM), chunk(SSD_INNER), state],
        scratch_shapes=[scratch_h],
        compiler_params=_params("parallel", "arbitrary"),
        name="ssd_forward",
    )(xbc, xbc, xbc, dt, rope_cos, rope_sin, lw["conv_w"], lw["conv_b"], consts["conv_shift"],
      lw["a_fwd"], consts["e_fwd"], consts["tri_fwd"], lw["d_skip"], h0_f)

    rchunk = lambda c: pl.BlockSpec((nb, q, c), lambda i, j: (i, nc - 1 - j, 0))
    y, hn_b = pl.pallas_call(
        _ssd_bwd_kernel,
        out_shape=[jax.ShapeDtypeStruct((b, t, SSD_INNER), BF16), state_shape],
        grid=(b // nb, nc),
        in_specs=[rchunk(SSD_CONV_DIM), rchunk(LANES), rchunk(SSD_INNER),
                  _const_spec(lw["a_bwd"].shape), _const_spec(consts["e_bwd"].shape),
                  _const_spec(consts["tri_bwd"].shape), state],
        out_specs=[rchunk(SSD_INNER), state],
        scratch_shapes=[scratch_h],
        compiler_params=_params("parallel", "arbitrary"),
        name="ssd_backward",
    )(xbc_c, dt, y_f, lw["a_bwd"], consts["e_bwd"], consts["tri_bwd"], h0_b)
    return y, hn_f, hn_b


def _bias_kernel(rpb_ref, o_ref):
    lh = pl.program_id(0)
    n_ri = 2 * NA_ROWS - 1
    n_ci = 2 * NA_COLS - 1
    lane = lax.broadcasted_iota(jnp.int32, (GRID_W, LANES), 1)
    qc = lax.broadcasted_iota(jnp.int32, (GRID_W, LANES), 0)
    kc = lane % GRID_W
    cs = jnp.clip(qc - NA_COLS // 2, 0, GRID_W - NA_COLS)
    col_ok = (kc >= cs) & (kc < cs + NA_COLS)
    ci = jnp.clip(kc - qc + (NA_COLS - 1), 0, n_ci - 1)
    tiles = []
    for ri in range(n_ri):
        base = (lh * n_ri + ri) * n_ci
        acc = jnp.full((GRID_W, LANES), NEG, F32)
        for c in range(n_ci):
            acc = jnp.where(col_ok & (ci == c), rpb_ref[base + c] * LOG2E, acc)
        tiles.append(acc)
    masked = jnp.full((GRID_W, LANES), NEG, F32)
    tiles = [masked] + tiles + [masked]
    for e in range(n_ri + 1):
        o_ref[0, e] = jnp.where(lane < GRID_W, tiles[e], tiles[e + 1])


def _bias_table(rpb):
    depth, heads, n_ri, n_ci = rpb.shape
    return pl.pallas_call(
        _bias_kernel,
        out_shape=jax.ShapeDtypeStruct((depth * heads, n_ri + 1, GRID_W, LANES), F32),
        grid=(depth * heads,),
        in_specs=[pl.BlockSpec(memory_space=pltpu.SMEM)],
        out_specs=pl.BlockSpec((1, n_ri + 1, GRID_W, LANES), lambda i: (i, 0, 0, 0)),
        compiler_params=pltpu.CompilerParams(dimension_semantics=("arbitrary",)),
        name="na_bias_table",
    )(rpb.reshape(-1))


def _na_tile(q_ref, k_ref, v_ref, kc_ref, vc_ref, bias_ref, o_ref, row0, start, nw, plan):
    tq = NA_QROWS * GRID_W
    qrows = slice(row0, row0 + tq)
    n_ctx = kc_ref.shape[1]
    n_cb = n_ctx // LANES
    lane_q = lax.broadcasted_iota(jnp.int32, (tq, LANES), 1)
    lane_k = lax.broadcasted_iota(jnp.int32, (n_ctx + nw * GRID_W, LANES), 1)
    n_pairs = NA_WIDTH // LANES
    own = [lambda lane, hh=hh: (lane < NA_HEAD_DIM) == (hh == 0) for hh in range(2)]
    col = lambda p: slice(p * LANES, (p + 1) * LANES)
    scores_all = []
    for p in range(n_pairs):
        qp = q_ref[0, qrows, col(p)]
        keys = jnp.concatenate([kc_ref[0, :, col(p)], k_ref[0, pl.ds(start, nw * GRID_W), col(p)]], axis=0)
        for hh in range(2):
            scores_all.append(_dot_nt(jnp.where(own[hh](lane_q), qp, jnp.zeros_like(qp)), keys))
    p_mats = []
    for h, scores in enumerate(scores_all):
        p_rows = []
        for qi, row_plan in enumerate(plan):
            rows = slice(qi * GRID_W, (qi + 1) * GRID_W)
            blocks = [scores[rows, m * LANES:(m + 1) * LANES] for m in range(n_cb)]
            for m, (ent, ok) in enumerate(row_plan):
                if ok is None:
                    blocks.append(None)
                    continue
                sb = scores[rows, (n_cb + m) * LANES:(n_cb + m + 1) * LANES] + bias_ref[h, ent]
                blocks.append(sb if ok is True else jnp.where(ok, sb, NEG))
            live = [sb for sb in blocks if sb is not None]
            mx = live[0]
            for sb in live[1:]:
                mx = jnp.maximum(mx, sb)
            mx = jnp.max(mx, axis=-1, keepdims=True)
            p_rows.append(jnp.concatenate(
                [jnp.zeros((GRID_W, LANES), BF16) if sb is None else jnp.exp2(sb - mx).astype(BF16)
                 for sb in blocks], axis=1))
        p_mats.append(jnp.concatenate(p_rows, axis=0))
    for p in range(n_pairs):
        vals = jnp.concatenate([vc_ref[0, :, col(p)], v_ref[0, pl.ds(start, nw * GRID_W), col(p)]], axis=0)
        nums = [_dot(p_mats[2 * p + hh], jnp.where(own[hh](lane_k), vals, jnp.ones_like(vals))) for hh in range(2)]
        num = jnp.where(lane_q < NA_HEAD_DIM, nums[0], nums[1])
        den = pltpu.roll(jnp.where(lane_q < NA_HEAD_DIM, nums[1], nums[0]), NA_HEAD_DIM, 1)
        o_ref[0, qrows, col(p)] = (num / den).astype(BF16)


def _na_kernel(q_ref, k_ref, v_ref, kc_ref, vc_ref, bias_ref, o_ref, *, nw, rows_n):
    rt = NA_QROWS
    half = NA_ROWS // 2
    lane_r = lax.broadcasted_iota(jnp.int32, (GRID_W, LANES), 1)
    first_half = lane_r < GRID_W
    nblk = nw // 2
    args = (q_ref, k_ref, v_ref, kc_ref, vc_ref, bias_ref, o_ref)

    def interior_tile(r0, row0):
        plan = []
        for qi in range(rt):
            row_plan = []
            for m in range(nblk):
                ok0 = qi <= 2 * m < qi + NA_ROWS
                ok1 = qi <= 2 * m + 1 < qi + NA_ROWS
                ok = True if ok0 and ok1 else None if not (ok0 or ok1) else first_half if ok0 else ~first_half
                row_plan.append((half + 2 * m - qi, ok))
            plan.append(row_plan)
        _na_tile(*args, row0, pl.multiple_of((r0 - half) * GRID_W, GRID_W), nw, plan)

    def clipped_tile(r0, row0):
        base = jnp.clip(r0 - half, 0, rows_n - nw)
        plan = []
        for qi in range(rt):
            r = r0 + qi
            rs = jnp.clip(r - half, 0, rows_n - NA_ROWS)
            row_plan = []
            for m in range(nblk):
                j0 = base + 2 * m
                ok0 = (j0 >= rs) & (j0 < rs + NA_ROWS)
                ok1 = (j0 + 1 >= rs) & (j0 + 1 < rs + NA_ROWS)
                ok = jnp.where(first_half, ok0.astype(jnp.int32), ok1.astype(jnp.int32)) > 0
                row_plan.append((jnp.clip(j0 - r + NA_ROWS, 0, 2 * NA_ROWS - 1), ok))
            plan.append(row_plan)
        _na_tile(*args, row0, pl.multiple_of(base * GRID_W, GRID_W), nw, plan)

    for sub in range(NA_TILES_PER_STEP):
        r0 = (pl.program_id(1) * NA_TILES_PER_STEP + sub) * rt
        interior = (r0 >= half) & (r0 - half <= rows_n - nw)
        pl.when(interior)(functools.partial(interior_tile, r0, sub * rt * GRID_W))
        pl.when(~interior)(functools.partial(clipped_tile, r0, sub * rt * GRID_W))


def _neighbourhood_attention(q, k, v, kc, vc, bias, layer):
    b, t, w = q.shape
    n_ctx = kc.shape[1]
    rows_n = t // GRID_W
    nw = NA_QROWS + NA_ROWS
    nw += nw % 2
    tq = NA_TILES_PER_STEP * NA_QROWS * GRID_W
    whole = lambda n: pl.BlockSpec((1, n, w), lambda i, j: (i, 0, 0))
    return pl.pallas_call(
        functools.partial(_na_kernel, nw=nw, rows_n=rows_n),
        out_shape=jax.ShapeDtypeStruct((b, t, w), BF16),
        grid=(b, t // tq),
        in_specs=[pl.BlockSpec((1, tq, w), lambda i, j: (i, j, 0)), whole(t), whole(t), whole(n_ctx), whole(n_ctx),
                  pl.BlockSpec((NA_HEADS,) + bias.shape[1:], lambda i, j: (layer, 0, 0, 0),
                               pipeline_mode=pl.Buffered(1))],
        out_specs=pl.BlockSpec((1, tq, w), lambda i, j: (i, j, 0)),
        compiler_params=_params("parallel", "arbitrary"),
        name="neighbourhood_attention",
    )(q, k, v, kc, vc, bias)


def _ctx_attn_kernel(q_ref, k_ref, v_ref, o_ref):
    n = q_ref.shape[1]
    lane = lax.broadcasted_iota(jnp.int32, (n, LANES), 1)
    for p in range(NA_WIDTH // LANES):
        cols = slice(p * LANES, (p + 1) * LANES)
        qp = q_ref[0, :, cols]
        kp = k_ref[0, :, cols]
        vp = v_ref[0, :, cols]
        nums = []
        for hh in range(2):
            own = (lane < NA_HEAD_DIM) == (hh == 0)
            s = _dot_nt(jnp.where(own, qp, jnp.zeros_like(qp)), kp)
            pm = jnp.exp2(s - jnp.max(s, axis=-1, keepdims=True)).astype(BF16)
            nums.append(_dot(pm, jnp.where(own, vp, jnp.ones_like(vp))))
        num = jnp.where(lane < NA_HEAD_DIM, nums[0], nums[1])
        den = pltpu.roll(jnp.where(lane < NA_HEAD_DIM, nums[1], nums[0]), NA_HEAD_DIM, 1)
        o_ref[0, :, cols] = (num / den).astype(BF16)


def _context_attention(q, k, v):
    b, n, w = q.shape
    spec = pl.BlockSpec((1, n, w), lambda i: (i, 0, 0))
    return pl.pallas_call(
        _ctx_attn_kernel,
        out_shape=jax.ShapeDtypeStruct((b, n, w), BF16),
        grid=(b,),
        in_specs=[spec, spec, spec],
        out_specs=spec,
        compiler_params=_params("parallel"),
        name="context_attention",
    )(q, k, v)


def _merge_ffn_kernel(x_ref, mod_ref, gate_ref, yssd_ref, z_ref, yna_ref, ygm_ref, sn_ref, wa, wb, wc, wo,
                      n2_ref, wfi, wfo, o_ref, *, ffn_chunk):
    d = x_ref.shape[2]
    side = (gate_ref[0, :, d:2 * d].astype(F32) * _dot(yna_ref[0], wb[...])
            + gate_ref[0, :, 2 * d:3 * d].astype(F32) * _dot(ygm_ref[0], wc[...]))
    y = yssd_ref[0].astype(F32) * _silu(z_ref[0].astype(F32))
    y = (y * lax.rsqrt(jnp.mean(y * y, axis=-1, keepdims=True) + EPS) * sn_ref[...]).astype(BF16)
    mixed = side + gate_ref[0, :, 0:d].astype(F32) * _dot(y, wa[...])
    x1 = x_ref[0] + mod_ref[0, 2:3, :] * _dot(mixed.astype(BF16), wo[...])
    xn = x1 * lax.rsqrt(jnp.mean(x1 * x1, axis=-1, keepdims=True) + EPS) * n2_ref[...]
    hb = (xn * (1.0 + mod_ref[0, 4:5, :]) + mod_ref[0, 3:4, :]).astype(BF16)
    hid = wfo.shape[0]
    acc = jnp.zeros_like(x1)
    for c0 in range(0, hid, ffn_chunk):
        a = _dot(hb, wfi[:, c0:c0 + ffn_chunk])
        g = _dot(hb, wfi[:, hid + c0:hid + c0 + ffn_chunk])
        acc = acc + _dot((_silu(a) * g).astype(BF16), wfo[c0:c0 + ffn_chunk, :])
    o_ref[0] = x1 + mod_ref[0, 5:6, :] * acc


def _merge_ffn(x, mod6, gate, y_ssd, z, y_na, y_gm, lw):
    b, t, d = x.shape
    tm = min(TOKEN_TILE, t)
    tok = lambda c: pl.BlockSpec((1, tm, c), lambda i, j: (i, j, 0))
    consts = [lw["ssd_norm"], lw["w_branch_ssd"], lw["w_branch_na"], lw["w_branch_gm"], lw["w_out"], lw["norm2"],
              lw["w_ffn_in"], lw["w_ffn_out"]]
    return pl.pallas_call(
        functools.partial(_merge_ffn_kernel, ffn_chunk=256),
        out_shape=jax.ShapeDtypeStruct((b, t, d), F32),
        grid=(b, t // tm),
        in_specs=[tok(d), pl.BlockSpec((1, 6, d), lambda i, j: (i, 0, 0)),
                  tok(3 * d), tok(SSD_INNER), tok(SSD_INNER), tok(NA_WIDTH), tok(GM_WIDTH)]
                 + [_const_spec(a.shape) for a in consts],
        out_specs=tok(d),
        compiler_params=_params("parallel", "parallel"),
        name="merge_out_ffn",
    )(x, mod6, gate, y_ssd, z, y_na, y_gm, *consts)


def _cast_kernel(w_ref, o_ref):
    o_ref[...] = w_ref[0].astype(BF16)


def _weight_bf16(w, l):
    _, rows, cols = w.shape
    rb = next(r for r in (rows, rows // 2, rows // 4, rows // 8) if r * cols * 4 <= (6 << 20) and r % 16 == 0)
    return pl.pallas_call(
        _cast_kernel,
        out_shape=jax.ShapeDtypeStruct((rows, cols), BF16),
        grid=(rows // rb,),
        in_specs=[pl.BlockSpec((1, rb, cols), lambda i: (l, i, 0))],
        out_specs=pl.BlockSpec((rb, cols), lambda i: (i, 0)),
        compiler_params=_params("parallel"),
        name="weight_to_bf16",
    )(w)


def _shared_constants(n_ctx, seq):
    pos = jnp.arange(seq)
    freqs = ROPE_BASE ** (-jnp.arange(ROPE_FREQS, dtype=F32) / ROPE_FREQS)
    ang_row = (pos // GRID_W).astype(F32)[:, None] * freqs
    ang_col = (pos % GRID_W).astype(F32)[:, None] * freqs
    cos = jnp.concatenate([jnp.cos(ang_row), jnp.cos(ang_row), jnp.cos(ang_col), jnp.cos(ang_col)], axis=1)
    sin = jnp.concatenate([-jnp.sin(ang_row), jnp.sin(ang_row), -jnp.sin(ang_col), jnp.sin(ang_col)], axis=1)

    r = jnp.arange(SSD_CHUNK)
    twice = lambda m, axis: jnp.concatenate([m, m], axis=axis).astype(BF16)
    lane_head = jnp.arange(SSD_INNER) // SSD_HEAD_DIM
    rows = jnp.arange(LANES)
    src = jnp.arange(SSD_CHUNK + 2 * HALO)
    conv_shift = jnp.concatenate([(src[None, :] == r[:, None] + HALO + k - SSD_CONV // 2) for k in range(SSD_CONV)],
                                 axis=1).astype(BF16)
    return dict(rope_cos=cos, rope_sin=sin,
                ctx_cos=jnp.ones((n_ctx, LANES), F32), ctx_sin=jnp.zeros((n_ctx, LANES), F32),
                tri_fwd=twice(r[None, :] <= r[:, None], 1), tri_bwd=twice(r[None, :] >= r[:, None], 1),
                e_fwd=twice(rows[:, None] == lane_head[None, :], 0),
                e_bwd=twice(rows[:, None] == lane_head[None, :] + SSD_HEADS, 0),
                conv_shift=conv_shift)


def _layer_weights(l, p):
    d = p["w_in"].shape[1]
    sizes = (SSD_INNER, SSD_CONV_DIM, 2 * SSD_HEADS, NA_WIDTH, NA_WIDTH, NA_WIDTH, 2 * GM_WIDTH, 3 * d)
    names = ("w_z", "w_xbc", "w_dt", "w_q", "w_k", "w_v", "w_uv", "w_gate")
    lw, start = {}, 0
    w_in = p["w_in"][l]
    for name, size in zip(names, sizes):
        lw[name] = w_in[:, start:start + size].astype(BF16)
        start += size
    pad_lanes = lambda v: jnp.pad(v, (0, LANES - v.shape[0])).reshape(1, LANES)
    lw["w_dt"] = jnp.pad(lw["w_dt"], ((0, 0), (0, LANES - 2 * SSD_HEADS)))
    lw["dt_bias"] = pad_lanes(p["dt_bias"][l].reshape(-1))
    a = -jnp.exp(p["a_log"][l].astype(F32))
    lw["a_fwd"] = pad_lanes(a[0] * LOG2E)
    lw["a_bwd"] = pad_lanes(jnp.concatenate([jnp.zeros((SSD_HEADS,), F32), a[1] * LOG2E]))
    row = lambda v: v.reshape(1, -1)
    lw["norm1"] = row(p["norm1"][l])
    lw["norm2"] = row(p["norm2"][l])
    lw["b_gate"] = row(p["b_gate"][l])
    lw["q_norm"] = row(jnp.tile(p["q_norm"][l], NA_HEADS))
    lw["k_norm"] = row(jnp.tile(p["k_norm"][l], NA_HEADS))
    head = jnp.arange(NA_WIDTH) // NA_HEAD_DIM
    lw["head_blk"] = ((head[:, None] == head[None, :]).astype(F32) / NA_HEAD_DIM).astype(BF16)
    lw["gm_norm"] = row(p["gm_norm"][l])
    w_s = p["w_spatial"][l].astype(BF16)
    lw["w_spatial"] = jnp.concatenate([w_s[0::2], w_s[1::2]], axis=2)
    lw["b_spatial"] = jnp.repeat(p["b_spatial"][l].T, GM_WIDTH // GM_GROUPS, axis=1)
    lw["conv_w"] = jnp.pad(p["conv_w"][l], ((0, 8 - SSD_CONV), (0, 0))).astype(BF16)
    lw["conv_b"] = row(p["conv_b"][l])
    lw["d_skip"] = row(jnp.repeat(p["d_skip"][l], SSD_HEAD_DIM))
    lw["ssd_norm"] = row(p["ssd_norm"][l])
    for name in ("w_branch_ssd", "w_branch_na", "w_branch_gm", "w_out", "w_ffn_in", "w_ffn_out"):
        lw[name] = _weight_bf16(p[name], l)
    return lw


def kernel(x, c, ctx, c_ctx, w_mod, b_mod, norm1, w_in, b_gate, conv_w, conv_b, a_log, dt_bias, d_skip, ssd_norm,
           q_norm, k_norm, rpb, gm_norm, w_spatial, b_spatial, w_branch_ssd, w_branch_na, w_branch_gm, w_out,
           norm2, w_ffn_in, w_ffn_out):
    p = dict(norm1=norm1, w_in=w_in, b_gate=b_gate, conv_w=conv_w, conv_b=conv_b, a_log=a_log, dt_bias=dt_bias,
             d_skip=d_skip, ssd_norm=ssd_norm, q_norm=q_norm, k_norm=k_norm, gm_norm=gm_norm, w_spatial=w_spatial,
             b_spatial=b_spatial, w_branch_ssd=w_branch_ssd, w_branch_na=w_branch_na, w_branch_gm=w_branch_gm,
             w_out=w_out, norm2=norm2, w_ffn_in=w_ffn_in, w_ffn_out=w_ffn_out)
    b, seq, d = x.shape
    n_ctx = ctx.shape[1]
    depth = w_mod.shape[0]

    c_all = jnp.zeros((8, d), F32).at[:b].set(c).at[b].set(c_ctx)
    mod = _modulation(c_all, w_mod, b_mod)
    bias = _bias_table(rpb)
    consts = _shared_constants(n_ctx, seq)
    zero_state = jnp.zeros((b, SSD_STATE, SSD_INNER), F32)

    xc = ctx
    for l in range(depth):
        lw = _layer_weights(l, p)
        mod_x = mod[l, :b].reshape(b, 6, d)
        mod_c = mod[l, b].reshape(1, 6, d)
        last = l == depth - 1

        flat = lambda t: t.reshape(1, b * n_ctx, t.shape[-1])
        per_sample = lambda t: t.reshape(b, n_ctx, t.shape[-1])
        zc, xbcc, dtc, qc, kc, vc, ygm_c, gate_c = [per_sample(t) for t in _input_projection(flat(xc), mod_c, lw)]
        z, xbc, dt, q, k, v, y_gm, gate = _input_projection(x, mod_x, lw)

        yssd_c, s_f, s_b = _ssd(xbcc, dtc, zero_state, zero_state, consts["ctx_cos"], consts["ctx_sin"], lw, consts)
        y_ssd, _, _ = _ssd(xbc, dt, s_f, s_b, consts["rope_cos"], consts["rope_sin"], lw, consts)

        y_na = _neighbourhood_attention(q, k, v, kc, vc, bias, l)
        x = _merge_ffn(x, mod_x, gate, y_ssd, z, y_na, y_gm, lw)
        if not last:
            yna_c = _context_attention(qc, kc, vc)
            xc = per_sample(_merge_ffn(flat(xc), mod_c, flat(gate_c), flat(yssd_c), flat(zc), flat(yna_c), flat(ygm_c), lw))
    return x
```

```python
import functools
import math

import jax
import jax.numpy as jnp
from jax import lax
from jax.experimental import pallas as pl
from jax.experimental.pallas import tpu as pltpu

F32 = jnp.float32
BF16 = jnp.bfloat16

EPS = 1e-6
GRID_W = 64

SSD_INNER = 1024
SSD_HEAD_DIM = 64
SSD_HEADS = 16
SSD_GROUPS = 4
SSD_STATE = 128
SSD_CONV = 5
SSD_CHUNK = 128
SSD_BC = SSD_GROUPS * SSD_STATE
SSD_CONV_DIM = SSD_INNER + 2 * SSD_BC
ROPE_FREQS = 32
ROPE_BASE = 10000.0

NA_HEAD_DIM = 64
NA_WIDTH = 512
NA_HEADS = 8
NA_ROWS = 8
NA_COLS = 16
NA_QROWS = 4
NA_TILES_PER_STEP = 4

GM_WIDTH = 512
GM_GROUPS = 8
GM_CHUNK = 128

LANES = 128
HALO = 16
NEG = -1e30
LOG2E = math.log2(math.e)
VMEM_LIMIT = 56 * 1024 * 1024
TOKEN_TILE = 512
SSD_SEQS_PER_STEP = 4


def _dot(a, b):
    return jnp.dot(a, b, preferred_element_type=F32)


def _dot_nt(a, b):
    return lax.dot_general(a, b, (((1,), (1,)), ((), ())), preferred_element_type=F32)


def _silu(x):
    return x / (1.0 + jnp.exp(-x))


def _sigmoid(x):
    return 1.0 / (1.0 + jnp.exp(-x))


def _gelu_tanh(x):
    return 0.5 * x * (1.0 + jnp.tanh(math.sqrt(2.0 / math.pi) * (x + 0.044715 * (x * x * x))))


def _softplus(x):
    return jnp.maximum(x, 0.0) + jnp.log(1.0 + jnp.exp(-jnp.abs(x)))


def _split_hi_lo(v):
    hi = v.astype(BF16)
    lo = (v - hi.astype(F32)).astype(BF16)
    return hi, lo


def _const_spec(shape):
    nd = len(shape)
    return pl.BlockSpec(shape, lambda *_: (0,) * nd, pipeline_mode=pl.Buffered(1))


def _params(*semantics):
    return pltpu.CompilerParams(dimension_semantics=semantics, vmem_limit_bytes=VMEM_LIMIT)


def _mod_kernel(c_ref, w_ref, b_ref, o_ref):
    o_ref[0] = _dot(_silu(c_ref[...]), w_ref[0]) + b_ref[0]


def _modulation(c_all, w_mod, b_mod):
    depth, d, n = w_mod.shape
    tn = 1536
    return pl.pallas_call(
        _mod_kernel,
        out_shape=jax.ShapeDtypeStruct((depth, 8, n), F32),
        grid=(depth, n // tn),
        in_specs=[pl.BlockSpec((8, d), lambda l, j: (0, 0)),
                  pl.BlockSpec((1, d, tn), lambda l, j: (l, 0, j)),
                  pl.BlockSpec((1, 1, tn), lambda l, j: (l, 0, j))],
        out_specs=pl.BlockSpec((1, 8, tn), lambda l, j: (l, 0, j)),
        compiler_params=_params("arbitrary", "arbitrary"),
        name="modulation",
    )(c_all, w_mod, b_mod.reshape(depth, 1, n))


def _inproj_kernel(x_ref, mod_ref, n1_ref, wz, wxbc, wdt, wq, wk, wv, wuv, wg, dtb, bg, qn, kn, blk,
                   gmn, ws, bsp, z_o, xbc_o, dt_o, q_o, k_o, v_o, ygm_o, gate_o):
    tm = x_ref.shape[1]
    x = x_ref[0]
    xn = x * lax.rsqrt(jnp.mean(x * x, axis=-1, keepdims=True) + EPS) * n1_ref[...]
    hb = (xn * (1.0 + mod_ref[0, 1:2, :]) + mod_ref[0, 0:1, :]).astype(BF16)

    cw = 512

    def head_norm(t, w_row):
        ms = _dot((t * t).astype(BF16), blk[...])
        return t * lax.rsqrt(ms + EPS) * w_row

    qf = _dot(hb, wq[...])
    kf = _dot(hb, wk[...])
    g = _gelu_tanh(_dot(hb, wuv[...]))
    dt_o[0] = _softplus(_dot(hb, wdt[...]) + dtb[...])
    v_o[0] = _dot(hb, wv[...]).astype(BF16)
    for n0 in range(0, z_o.shape[2], cw):
        z_o[0, :, n0:n0 + cw] = _dot(hb, wz[:, n0:n0 + cw]).astype(BF16)

    q_o[0] = (head_norm(qf, qn[...]) * (NA_HEAD_DIM ** -0.5 * LOG2E)).astype(BF16)
    k_o[0] = head_norm(kf, kn[...]).astype(BF16)
    for n0 in range(0, xbc_o.shape[2], cw):
        xbc_o[0, :, n0:n0 + cw] = _dot(hb, wxbc[:, n0:n0 + cw]).astype(BF16)

    u = g[:, :GM_WIDTH]
    v = g[:, GM_WIDTH:]
    vb = (v * lax.rsqrt(jnp.mean(v * v, axis=-1, keepdims=True) + EPS) * gmn[...]).astype(BF16)
    first = lax.broadcasted_iota(jnp.int32, (GM_CHUNK, LANES), 1) < LANES // 2
    for c0 in range(0, tm, GM_CHUNK):
        for p in range(GM_WIDTH // LANES):
            cols = slice(p * LANES, (p + 1) * LANES)
            vp = vb[c0:c0 + GM_CHUNK, cols]
            zero = jnp.zeros_like(vp)
            stacked = jnp.concatenate([jnp.where(first, vp, zero), jnp.where(first, zero, vp)], axis=0)
            mixed = _dot(ws[p], stacked) + bsp[:, cols]
            ygm_o[0, c0:c0 + GM_CHUNK, cols] = (u[c0:c0 + GM_CHUNK, cols] * mixed).astype(BF16)

    for n0 in range(0, gate_o.shape[2], cw):
        gate_o[0, :, n0:n0 + cw] = _sigmoid(_dot(hb, wg[:, n0:n0 + cw]) + bg[:, n0:n0 + cw]).astype(BF16)


def _input_projection(x, mod6, lw):
    b, t, d = x.shape
    tm = min(TOKEN_TILE, t)
    tok = lambda c: pl.BlockSpec((1, tm, c), lambda i, j: (i, j, 0))
    consts = [lw["norm1"], lw["w_z"], lw["w_xbc"], lw["w_dt"], lw["w_q"], lw["w_k"], lw["w_v"], lw["w_uv"],
              lw["w_gate"], lw["dt_bias"], lw["b_gate"], lw["q_norm"], lw["k_norm"], lw["head_blk"],
              lw["gm_norm"], lw["w_spatial"], lw["b_spatial"]]
    widths = [(SSD_INNER, BF16), (SSD_CONV_DIM, BF16), (LANES, F32), (NA_WIDTH, BF16), (NA_WIDTH, BF16),
              (NA_WIDTH, BF16), (GM_WIDTH, BF16), (3 * d, BF16)]
    return pl.pallas_call(
        _inproj_kernel,
        out_shape=[jax.ShapeDtypeStruct((b, t, c), ty) for c, ty in widths],
        grid=(b, t // tm),
        in_specs=[tok(d), pl.BlockSpec((1, 6, d), lambda i, j: (i, 0, 0))] + [_const_spec(a.shape) for a in consts],
        out_specs=[tok(c) for c, _ in widths],
        compiler_params=_params("parallel", "parallel"),
        name="input_projection",
    )(x, mod6, *consts)


def _ssd_chunks(seqs, a_row, e2_ref, tri2_ref, reverse, head_off):
    q = seqs[0][0].shape[0]
    last = 0 if reverse else q - 1
    gw = SSD_INNER // SSD_GROUPS
    hpg = gw // SSD_HEAD_DIM
    groups = range(SSD_GROUPS)
    gcols = lambda g: slice(g * gw, (g + 1) * gw)
    ncols = lambda g: slice(g * SSD_STATE, (g + 1) * SSD_STATE)
    tri2 = tri2_ref[...]

    acums = []
    for xs, bm, cm, dt, h_ref in seqs:
        a = dt * a_row
        a1 = a.astype(BF16)
        r1 = a - a1.astype(F32)
        a2 = r1.astype(BF16)
        a3 = (r1 - a2.astype(F32)).astype(BF16)
        acums.append(_dot(tri2, jnp.concatenate([a1, a2], axis=0)) + _dot(tri2[:, :q], a3))

    wides = []
    for (xs, bm, cm, dt, h_ref), acum in zip(seqs, acums):
        a_last = acum[last:last + 1, :]
        per_head = jnp.concatenate([dt, dt * jnp.exp2(a_last - acum),
                                    jnp.broadcast_to(jnp.exp2(a_last), (8, LANES))], axis=0)
        hi, lo = _split_hi_lo(per_head)
        wides.append(_dot(jnp.concatenate([hi, lo], axis=1), e2_ref[...]))

    cbs, y_inters, bts, h_prevs = [], [], [], []
    for xs, bm, cm, dt, h_ref in seqs:
        cgbs = [cm[:, ncols(g)].astype(BF16) for g in groups]
        h_prev = [h_ref[:, gcols(g)] for g in groups]
        cbs.append([_dot_nt(cgbs[g], bm[:, ncols(g)].astype(BF16)) for g in groups])
        y_inters.append([_dot(cgbs[g], h_prev[g].astype(BF16)) for g in groups])
        bts.append([bm[:, ncols(g)].astype(F32).T.astype(BF16) for g in groups])
        h_prevs.append(h_prev)

    row = lax.broadcasted_iota(jnp.int32, (q, q), 0)
    col = lax.broadcasted_iota(jnp.int32, (q, q), 1)
    in_scan = (col >= row) if reverse else (col <= row)
    lane = lax.broadcasted_iota(jnp.int32, (q, LANES), 1)
    lane_head = lax.broadcasted_iota(jnp.int32, (q, gw), 1) // SSD_HEAD_DIM

    lhs, rhs, dec_outs, xw_bs, h_decs = [], [], [], [], []
    for si, (xs, bm, cm, dt, h_ref) in enumerate(seqs):
        acum, wide = acums[si], wides[si]
        acum_t = acum.T
        xdt_b = (xs * wide[0:q]).astype(BF16)
        xw_bs.append((xs * wide[q:2 * q]).astype(BF16))
        h_decs.append(wide[2 * q:2 * q + 1])
        for g in groups:
            xg = xdt_b[:, gcols(g)]
            ms, decs = [], []
            for j in range(hpg):
                hc = head_off + g * hpg + j
                colx = jnp.broadcast_to(acum[:, hc:hc + 1], (q, q))
                decay = jnp.exp2(jnp.where(in_scan, colx - acum_t[hc:hc + 1, :], NEG))
                ms.append((cbs[si][g] * decay).astype(BF16))
                decs.append(jnp.exp2(colx))
            lhs.append(jnp.concatenate(ms, axis=1))
            rhs.append(jnp.concatenate([jnp.where(lane_head == j, xg, jnp.zeros_like(xg)) for j in range(hpg)], axis=0))
            dec_outs.append(jnp.concatenate([jnp.where(lane < SSD_HEAD_DIM, decs[2 * i], decs[2 * i + 1])
                                             for i in range(hpg // 2)], axis=1))
    y_intra = [_dot(l, r) for l, r in zip(lhs, rhs)]
    h_add = [_dot(bts[si][g], xw_bs[si][:, gcols(g)]) for si in range(len(seqs)) for g in groups]
    ys = []
    for si, (xs, bm, cm, dt, h_ref) in enumerate(seqs):
        for g in groups:
            h_ref[:, gcols(g)] = h_prevs[si][g] * h_decs[si][:, gcols(g)] + h_add[si * SSD_GROUPS + g]
        ys.append(jnp.concatenate([y_intra[si * SSD_GROUPS + g] + y_inters[si][g] * dec_outs[si * SSD_GROUPS + g]
                                   for g in groups], axis=1))
    return ys


def _ssd_fwd_kernel(cur_ref, prev_ref, next_ref, dt_ref, cos_ref, sin_ref, cw_ref, cb_ref, shift_ref, a_ref, e_ref,
                    tri_ref, dsk_ref, h0_ref, xbc_o, y_o, hn_o, h_ref):
    i = pl.program_id(1)
    nc = pl.num_programs(1)
    nb, q, _ = cur_ref.shape

    @pl.when(i == 0)
    def _():
        h_ref[...] = h0_ref[...]

    pv = jnp.where(i == 0, 0.0, 1.0).astype(BF16)
    nv = jnp.where(i == nc - 1, 0.0, 1.0).astype(BF16)
    lane = lax.broadcasted_iota(jnp.int32, (q, LANES), 1)
    cos = cos_ref[...]
    sin = sin_ref[...]

    def rope(t):
        sw = jnp.where((lane & ROPE_FREQS) == 0, pltpu.roll(t, LANES - ROPE_FREQS, 1), pltpu.roll(t, ROPE_FREQS, 1))
        return t * cos + sw * sin

    cwid = 256
    seqs = []
    for s in range(nb):
        ext = jnp.concatenate([prev_ref[s] * pv, cur_ref[s], next_ref[s] * nv], axis=0)
        parts = []
        for c0 in range(0, SSD_CONV_DIM, cwid):
            e = ext[:, c0:c0 + cwid]
            taps = jnp.concatenate([e * cw_ref[k:k + 1, c0:c0 + cwid] for k in range(SSD_CONV)], axis=0)
            parts.append(_silu(_dot(shift_ref[...], taps) + cb_ref[:, c0:c0 + cwid]))
        xs = jnp.concatenate(parts[:SSD_INNER // cwid], axis=1)
        bc = jnp.concatenate(parts[SSD_INNER // cwid:], axis=1)
        bc = jnp.concatenate([rope(bc[:, g * LANES:(g + 1) * LANES]) for g in range(2 * SSD_GROUPS)], axis=1)
        xbc_o[s, :, :SSD_INNER] = xs.astype(BF16)
        xbc_o[s, :, SSD_INNER:] = bc.astype(BF16)
        seqs.append((xs, bc[:, :SSD_BC], bc[:, SSD_BC:], dt_ref[s], h_ref.at[s]))
    ys = _ssd_chunks(seqs, a_ref[...], e_ref, tri_ref, False, 0)
    for s in range(nb):
        y_o[s] = (ys[s] + dsk_ref[...] * seqs[s][0]).astype(BF16)

    @pl.when(i == nc - 1)
    def _():
        hn_o[...] = h_ref[...]


def _ssd_bwd_kernel(xbc_ref, dt_ref, yf_ref, a_ref, e_ref, tri_ref, h0_ref, y_o, hn_o, h_ref):
    i = pl.program_id(1)

    @pl.when(i == 0)
    def _():
        h_ref[...] = h0_ref[...]

    nb = xbc_ref.shape[0]
    seqs = [(xbc_ref[s, :, :SSD_INNER].astype(F32), xbc_ref[s, :, SSD_INNER:SSD_INNER + SSD_BC],
             xbc_ref[s, :, SSD_INNER + SSD_BC:], dt_ref[s], h_ref.at[s]) for s in range(nb)]
    ys = _ssd_chunks(seqs, a_ref[...], e_ref, tri_ref, True, SSD_HEADS)
    for s in range(nb):
        y_o[s] = (ys[s] + yf_ref[s].astype(F32)).astype(BF16)

    @pl.when(i == pl.num_programs(1) - 1)
    def _():
        hn_o[...] = h_ref[...]


def _ssd(xbc, dt, h0_f, h0_b, rope_cos, rope_sin, lw, consts):
    b, t, _ = xbc.shape
    q = SSD_CHUNK
    nc = t // q
    nb = SSD_SEQS_PER_STEP if b % SSD_SEQS_PER_STEP == 0 else 1
    hb = q // HALO
    n_hb = t // HALO
    chunk = lambda c: pl.BlockSpec((nb, q, c), lambda i, j: (i, j, 0))
    state = pl.BlockSpec((nb, SSD_STATE, SSD_INNER), lambda i, j: (i, 0, 0))
    state_shape = jax.ShapeDtypeStruct((b, SSD_STATE, SSD_INNER), F32)
    scratch_h = pltpu.VMEM((nb, SSD_STATE, SSD_INNER), F32)

    xbc_c, y_f, hn_f = pl.pallas_call(
        _ssd_fwd_kernel,
        out_shape=[jax.ShapeDtypeStruct((b, t, SSD_CONV_DIM), BF16), jax.ShapeDtypeStruct((b, t, SSD_INNER), BF16),
                   state_shape],
        grid=(b // nb, nc),
        in_specs=[chunk(SSD_CONV_DIM),
                  pl.BlockSpec((nb, HALO, SSD_CONV_DIM), lambda i, j: (i, jnp.maximum(j * hb - 1, 0), 0)),
                  pl.BlockSpec((nb, HALO, SSD_CONV_DIM), lambda i, j: (i, jnp.minimum((j + 1) * hb, n_hb - 1), 0)),
                  chunk(LANES),
                  pl.BlockSpec((q, LANES), lambda i, j: (j, 0)),
                  pl.BlockSpec((q, LANES), lambda i, j: (j, 0)),
                  _const_spec(lw["conv_w"].shape), _const_spec(lw["conv_b"].shape),
                  _const_spec(consts["conv_shift"].shape),
                  _const_spec(lw["a_fwd"].shape), _const_spec(consts["e_fwd"].shape),
                  _const_spec(consts["tri_fwd"].shape), _const_spec(lw["d_skip"].shape), state],
        out_specs=[chunk(SSD_CONV_DIM), chunk(SSD_INNER), state],
        scratch_shapes=[scratch_h],
        compiler_params=_params("parallel", "arbitrary"),
        name="ssd_forward",
    )(xbc, xbc, xbc, dt, rope_cos, rope_sin, lw["conv_w"], lw["conv_b"], consts["conv_shift"],
      lw["a_fwd"], consts["e_fwd"], consts["tri_fwd"], lw["d_skip"], h0_f)

    rchunk = lambda c: pl.BlockSpec((nb, q, c), lambda i, j: (i, nc - 1 - j, 0))
    y, hn_b = pl.pallas_call(
        _ssd_bwd_kernel,
        out_shape=[jax.ShapeDtypeStruct((b, t, SSD_INNER), BF16), state_shape],
        grid=(b // nb, nc),
        in_specs=[rchunk(SSD_CONV_DIM), rchunk(LANES), rchunk(SSD_INNER),
                  _const_spec(lw["a_bwd"].shape), _const_spec(consts["e_bwd"].shape),
                  _const_spec(consts["tri_bwd"].shape), state],
        out_specs=[rchunk(SSD_INNER), state],
        scratch_shapes=[scratch_h],
        compiler_params=_params("parallel", "arbitrary"),
        name="ssd_backward",
    )(xbc_c, dt, y_f, lw["a_bwd"], consts["e_bwd"], consts["tri_bwd"], h0_b)
    return y, hn_f, hn_b


def _bias_kernel(rpb_ref, o_ref):
    lh = pl.program_id(0)
    n_ri = 2 * NA_ROWS - 1
    n_ci = 2 * NA_COLS - 1
    lane = lax.broadcasted_iota(jnp.int32, (GRID_W, LANES), 1)
    qc = lax.broadcasted_iota(jnp.int32, (GRID_W, LANES), 0)
    kc = lane % GRID_W
    cs = jnp.clip(qc - NA_COLS // 2, 0, GRID_W - NA_COLS)
    col_ok = (kc >= cs) & (kc < cs + NA_COLS)
    ci = jnp.clip(kc - qc + (NA_COLS - 1), 0, n_ci - 1)
    tiles = []
    for ri in range(n_ri):
        base = (lh * n_ri + ri) * n_ci
        acc = jnp.full((GRID_W, LANES), NEG, F32)
        for c in range(n_ci):
            acc = jnp.where(col_ok & (ci == c), rpb_ref[base + c] * LOG2E, acc)
        tiles.append(acc)
    masked = jnp.full((GRID_W, LANES), NEG, F32)
    tiles = [masked] + tiles + [masked]
    for e in range(n_ri + 1):
        o_ref[0, e] = jnp.where(lane < GRID_W, tiles[e], tiles[e + 1])


def _bias_table(rpb):
    depth, heads, n_ri, n_ci = rpb.shape
    return pl.pallas_call(
        _bias_kernel,
        out_shape=jax.ShapeDtypeStruct((depth * heads, n_ri + 1, GRID_W, LANES), F32),
        grid=(depth * heads,),
        in_specs=[pl.BlockSpec(memory_space=pltpu.SMEM)],
        out_specs=pl.BlockSpec((1, n_ri + 1, GRID_W, LANES), lambda i: (i, 0, 0, 0)),
        compiler_params=pltpu.CompilerParams(dimension_semantics=("arbitrary",)),
        name="na_bias_table",
    )(rpb.reshape(-1))


def _na_tile(q_ref, k_ref, v_ref, kc_ref, vc_ref, bias_ref, o_ref, row0, start, nw, plan):
    tq = NA_QROWS * GRID_W
    qrows = slice(row0, row0 + tq)
    n_ctx = kc_ref.shape[1]
    n_cb = n_ctx // LANES
    lane_q = lax.broadcasted_iota(jnp.int32, (tq, LANES), 1)
    lane_k = lax.broadcasted_iota(jnp.int32, (n_ctx + nw * GRID_W, LANES), 1)
    n_pairs = NA_WIDTH // LANES
    own = [lambda lane, hh=hh: (lane < NA_HEAD_DIM) == (hh == 0) for hh in range(2)]
    col = lambda p: slice(p * LANES, (p + 1) * LANES)
    scores_all = []
    for p in range(n_pairs):
        qp = q_ref[0, qrows, col(p)]
        keys = jnp.concatenate([kc_ref[0, :, col(p)], k_ref[0, pl.ds(start, nw * GRID_W), col(p)]], axis=0)
        for hh in range(2):
            scores_all.append(_dot_nt(jnp.where(own[hh](lane_q), qp, jnp.zeros_like(qp)), keys))
    p_mats = []
    for h, scores in enumerate(scores_all):
        p_rows = []
        for qi, row_plan in enumerate(plan):
            rows = slice(qi * GRID_W, (qi + 1) * GRID_W)
            blocks = [scores[rows, m * LANES:(m + 1) * LANES] for m in range(n_cb)]
            for m, (ent, ok) in enumerate(row_plan):
                if ok is None:
                    blocks.append(None)
                    continue
                sb = scores[rows, (n_cb + m) * LANES:(n_cb + m + 1) * LANES] + bias_ref[h, ent]
                blocks.append(sb if ok is True else jnp.where(ok, sb, NEG))
            live = [sb for sb in blocks if sb is not None]
            mx = live[0]
            for sb in live[1:]:
                mx = jnp.maximum(mx, sb)
            mx = jnp.max(mx, axis=-1, keepdims=True)
            p_rows.append(jnp.concatenate(
                [jnp.zeros((GRID_W, LANES), BF16) if sb is None else jnp.exp2(sb - mx).astype(BF16)
                 for sb in blocks], axis=1))
        p_mats.append(jnp.concatenate(p_rows, axis=0))
    for p in range(n_pairs):
        vals = jnp.concatenate([vc_ref[0, :, col(p)], v_ref[0, pl.ds(start, nw * GRID_W), col(p)]], axis=0)
        nums = [_dot(p_mats[2 * p + hh], jnp.where(own[hh](lane_k), vals, jnp.ones_like(vals))) for hh in range(2)]
        num = jnp.where(lane_q < NA_HEAD_DIM, nums[0], nums[1])
        den = pltpu.roll(jnp.where(lane_q < NA_HEAD_DIM, nums[1], nums[0]), NA_HEAD_DIM, 1)
        o_ref[0, qrows, col(p)] = (num / den).astype(BF16)


def _na_kernel(q_ref, k_ref, v_ref, kc_ref, vc_ref, bias_ref, o_ref, *, nw, rows_n):
    rt = NA_QROWS
    half = NA_ROWS // 2
    lane_r = lax.broadcasted_iota(jnp.int32, (GRID_W, LANES), 1)
    first_half = lane_r < GRID_W
    nblk = nw // 2
    args = (q_ref, k_ref, v_ref, kc_ref, vc_ref, bias_ref, o_ref)

    def interior_tile(r0, row0):
        plan = []
        for qi in range(rt):
            row_plan = []
            for m in range(nblk):
                ok0 = qi <= 2 * m < qi + NA_ROWS
                ok1 = qi <= 2 * m + 1 < qi + NA_ROWS
                ok = True if ok0 and ok1 else None if not (ok0 or ok1) else first_half if ok0 else ~first_half
                row_plan.append((half + 2 * m - qi, ok))
            plan.append(row_plan)
        _na_tile(*args, row0, pl.multiple_of((r0 - half) * GRID_W, GRID_W), nw, plan)

    def clipped_tile(r0, row0):
        base = jnp.clip(r0 - half, 0, rows_n - nw)
        plan = []
        for qi in range(rt):
            r = r0 + qi
            rs = jnp.clip(r - half, 0, rows_n - NA_ROWS)
            row_plan = []
            for m in range(nblk):
                j0 = base + 2 * m
                ok0 = (j0 >= rs) & (j0 < rs + NA_ROWS)
                ok1 = (j0 + 1 >= rs) & (j0 + 1 < rs + NA_ROWS)
                ok = jnp.where(first_half, ok0.astype(jnp.int32), ok1.astype(jnp.int32)) > 0
                row_plan.append((jnp.clip(j0 - r + NA_ROWS, 0, 2 * NA_ROWS - 1), ok))
            plan.append(row_plan)
        _na_tile(*args, row0, pl.multiple_of(base * GRID_W, GRID_W), nw, plan)

    for sub in range(NA_TILES_PER_STEP):
        r0 = (pl.program_id(1) * NA_TILES_PER_STEP + sub) * rt
        interior = (r0 >= half) & (r0 - half <= rows_n - nw)
        pl.when(interior)(functools.partial(interior_tile, r0, sub * rt * GRID_W))
        pl.when(~interior)(functools.partial(clipped_tile, r0, sub * rt * GRID_W))


def _neighbourhood_attention(q, k, v, kc, vc, bias, layer):
    b, t, w = q.shape
    n_ctx = kc.shape[1]
    rows_n = t // GRID_W
    nw = NA_QROWS + NA_ROWS
    nw += nw % 2
    tq = NA_TILES_PER_STEP * NA_QROWS * GRID_W
    whole = lambda n: pl.BlockSpec((1, n, w), lambda i, j: (i, 0, 0))
    return pl.pallas_call(
        functools.partial(_na_kernel, nw=nw, rows_n=rows_n),
        out_shape=jax.ShapeDtypeStruct((b, t, w), BF16),
        grid=(b, t // tq),
        in_specs=[pl.BlockSpec((1, tq, w), lambda i, j: (i, j, 0)), whole(t), whole(t), whole(n_ctx), whole(n_ctx),
                  pl.BlockSpec((NA_HEADS,) + bias.shape[1:], lambda i, j: (layer, 0, 0, 0),
                               pipeline_mode=pl.Buffered(1))],
        out_specs=pl.BlockSpec((1, tq, w), lambda i, j: (i, j, 0)),
        compiler_params=_params("parallel", "arbitrary"),
        name="neighbourhood_attention",
    )(q, k, v, kc, vc, bias)


def _ctx_attn_kernel(q_ref, k_ref, v_ref, o_ref):
    n = q_ref.shape[1]
    lane = lax.broadcasted_iota(jnp.int32, (n, LANES), 1)
    for p in range(NA_WIDTH // LANES):
        cols = slice(p * LANES, (p + 1) * LANES)
        qp = q_ref[0, :, cols]
        kp = k_ref[0, :, cols]
        vp = v_ref[0, :, cols]
        nums = []
        for hh in range(2):
            own = (lane < NA_HEAD_DIM) == (hh == 0)
            s = _dot_nt(jnp.where(own, qp, jnp.zeros_like(qp)), kp)
            pm = jnp.exp2(s - jnp.max(s, axis=-1, keepdims=True)).astype(BF16)
            nums.append(_dot(pm, jnp.where(own, vp, jnp.ones_like(vp))))
        num = jnp.where(lane < NA_HEAD_DIM, nums[0], nums[1])
        den = pltpu.roll(jnp.where(lane < NA_HEAD_DIM, nums[1], nums[0]), NA_HEAD_DIM, 1)
        o_ref[0, :, cols] = (num / den).astype(BF16)


def _context_attention(q, k, v):
    b, n, w = q.shape
    spec = pl.BlockSpec((1, n, w), lambda i: (i, 0, 0))
    return pl.pallas_call(
        _ctx_attn_kernel,
        out_shape=jax.ShapeDtypeStruct((b, n, w), BF16),
        grid=(b,),
        in_specs=[spec, spec, spec],
        out_specs=spec,
        compiler_params=_params("parallel"),
        name="context_attention",
    )(q, k, v)


def _merge_ffn_kernel(x_ref, mod_ref, gate_ref, yssd_ref, z_ref, yna_ref, ygm_ref, sn_ref, wa, wb, wc, wo,
                      n2_ref, wfi, wfo, o_ref, *, ffn_chunk):
    d = x_ref.shape[2]
    y = yssd_ref[0].astype(F32) * _silu(z_ref[0].astype(F32))
    y = (y * lax.rsqrt(jnp.mean(y * y, axis=-1, keepdims=True) + EPS) * sn_ref[...]).astype(BF16)
    mixed = (gate_ref[0, :, 0:d].astype(F32) * _dot(y, wa[...])
             + gate_ref[0, :, d:2 * d].astype(F32) * _dot(yna_ref[0], wb[...])
             + gate_ref[0, :, 2 * d:3 * d].astype(F32) * _dot(ygm_ref[0], wc[...]))
    x1 = x_ref[0] + mod_ref[0, 2:3, :] * _dot(mixed.astype(BF16), wo[...])
    xn = x1 * lax.rsqrt(jnp.mean(x1 * x1, axis=-1, keepdims=True) + EPS) * n2_ref[...]
    hb = (xn * (1.0 + mod_ref[0, 4:5, :]) + mod_ref[0, 3:4, :]).astype(BF16)
    hid = wfo.shape[0]
    acc = jnp.zeros_like(x1)
    for c0 in range(0, hid, ffn_chunk):
        a = _dot(hb, wfi[:, c0:c0 + ffn_chunk])
        g = _dot(hb, wfi[:, hid + c0:hid + c0 + ffn_chunk])
        acc = acc + _dot((_silu(a) * g).astype(BF16), wfo[c0:c0 + ffn_chunk, :])
    o_ref[0] = x1 + mod_ref[0, 5:6, :] * acc


def _merge_ffn(x, mod6, gate, y_ssd, z, y_na, y_gm, lw):
    b, t, d = x.shape
    tm = min(TOKEN_TILE, t)
    tok = lambda c: pl.BlockSpec((1, tm, c), lambda i, j: (i, j, 0))
    consts = [lw["ssd_norm"], lw["w_branch_ssd"], lw["w_branch_na"], lw["w_branch_gm"], lw["w_out"], lw["norm2"],
              lw["w_ffn_in"], lw["w_ffn_out"]]
    return pl.pallas_call(
        functools.partial(_merge_ffn_kernel, ffn_chunk=256),
        out_shape=jax.ShapeDtypeStruct((b, t, d), F32),
        grid=(b, t // tm),
        in_specs=[tok(d), pl.BlockSpec((1, 6, d), lambda i, j: (i, 0, 0)),
                  tok(3 * d), tok(SSD_INNER), tok(SSD_INNER), tok(NA_WIDTH), tok(GM_WIDTH)]
                 + [_const_spec(a.shape) for a in consts],
        out_specs=tok(d),
        compiler_params=_params("parallel", "parallel"),
        name="merge_out_ffn",
    )(x, mod6, gate, y_ssd, z, y_na, y_gm, *consts)


def _cast_kernel(w_ref, o_ref):
    o_ref[...] = w_ref[0].astype(BF16)


def _weight_bf16(w, l):
    _, rows, cols = w.shape
    rb = next(r for r in (rows, rows // 2, rows // 4, rows // 8) if r * cols * 4 <= (6 << 20) and r % 16 == 0)
    return pl.pallas_call(
        _cast_kernel,
        out_shape=jax.ShapeDtypeStruct((rows, cols), BF16),
        grid=(rows // rb, 1),
        in_specs=[pl.BlockSpec((1, rb, cols), lambda i, j: (l, i, j))],
        out_specs=pl.BlockSpec((rb, cols), lambda i, j: (i, j)),
        compiler_params=_params("parallel", "parallel"),
        name="weight_to_bf16",
    )(w)


def _shared_constants(n_ctx, seq):
    pos = jnp.arange(seq)
    freqs = ROPE_BASE ** (-jnp.arange(ROPE_FREQS, dtype=F32) / ROPE_FREQS)
    ang_row = (pos // GRID_W).astype(F32)[:, None] * freqs
    ang_col = (pos % GRID_W).astype(F32)[:, None] * freqs
    cos = jnp.concatenate([jnp.cos(ang_row), jnp.cos(ang_row), jnp.cos(ang_col), jnp.cos(ang_col)], axis=1)
    sin = jnp.concatenate([-jnp.sin(ang_row), jnp.sin(ang_row), -jnp.sin(ang_col), jnp.sin(ang_col)], axis=1)

    r = jnp.arange(SSD_CHUNK)
    twice = lambda m, axis: jnp.concatenate([m, m], axis=axis).astype(BF16)
    lane_head = jnp.arange(SSD_INNER) // SSD_HEAD_DIM
    rows = jnp.arange(LANES)
    src = jnp.arange(SSD_CHUNK + 2 * HALO)
    conv_shift = jnp.concatenate([(src[None, :] == r[:, None] + HALO + k - SSD_CONV // 2) for k in range(SSD_CONV)],
                                 axis=1).astype(BF16)
    return dict(rope_cos=cos, rope_sin=sin,
                ctx_cos=jnp.ones((n_ctx, LANES), F32), ctx_sin=jnp.zeros((n_ctx, LANES), F32),
                tri_fwd=twice(r[None, :] <= r[:, None], 1), tri_bwd=twice(r[None, :] >= r[:, None], 1),
                e_fwd=twice(rows[:, None] == lane_head[None, :], 0),
                e_bwd=twice(rows[:, None] == lane_head[None, :] + SSD_HEADS, 0),
                conv_shift=conv_shift)


def _layer_weights(l, p):
    d = p["w_in"].shape[1]
    sizes = (SSD_INNER, SSD_CONV_DIM, 2 * SSD_HEADS, NA_WIDTH, NA_WIDTH, NA_WIDTH, 2 * GM_WIDTH, 3 * d)
    names = ("w_z", "w_xbc", "w_dt", "w_q", "w_k", "w_v", "w_uv", "w_gate")
    lw, start = {}, 0
    w_in = p["w_in"][l]
    for name, size in zip(names, sizes):
        lw[name] = w_in[:, start:start + size].astype(BF16)
        start += size
    pad_lanes = lambda v: jnp.pad(v, (0, LANES - v.shape[0])).reshape(1, LANES)
    lw["w_dt"] = jnp.pad(lw["w_dt"], ((0, 0), (0, LANES - 2 * SSD_HEADS)))
    lw["dt_bias"] = pad_lanes(p["dt_bias"][l].reshape(-1))
    a = -jnp.exp(p["a_log"][l].astype(F32))
    lw["a_fwd"] = pad_lanes(a[0] * LOG2E)
    lw["a_bwd"] = pad_lanes(jnp.concatenate([jnp.zeros((SSD_HEADS,), F32), a[1] * LOG2E]))
    row = lambda v: v.reshape(1, -1)
    lw["norm1"] = row(p["norm1"][l])
    lw["norm2"] = row(p["norm2"][l])
    lw["b_gate"] = row(p["b_gate"][l])
    lw["q_norm"] = row(jnp.tile(p["q_norm"][l], NA_HEADS))
    lw["k_norm"] = row(jnp.tile(p["k_norm"][l], NA_HEADS))
    head = jnp.arange(NA_WIDTH) // NA_HEAD_DIM
    lw["head_blk"] = ((head[:, None] == head[None, :]).astype(F32) / NA_HEAD_DIM).astype(BF16)
    lw["gm_norm"] = row(p["gm_norm"][l])
    w_s = p["w_spatial"][l].astype(BF16)
    lw["w_spatial"] = jnp.concatenate([w_s[0::2], w_s[1::2]], axis=2)
    lw["b_spatial"] = jnp.repeat(p["b_spatial"][l].T, GM_WIDTH // GM_GROUPS, axis=1)
    lw["conv_w"] = jnp.pad(p["conv_w"][l], ((0, 8 - SSD_CONV), (0, 0))).astype(BF16)
    lw["conv_b"] = row(p["conv_b"][l])
    lw["d_skip"] = row(jnp.repeat(p["d_skip"][l], SSD_HEAD_DIM))
    lw["ssd_norm"] = row(p["ssd_norm"][l])
    for name in ("w_branch_ssd", "w_branch_na", "w_branch_gm", "w_out", "w_ffn_in", "w_ffn_out"):
        lw[name] = _weight_bf16(p[name], l)
    return lw


def kernel(x, c, ctx, c_ctx, w_mod, b_mod, norm1, w_in, b_gate, conv_w, conv_b, a_log, dt_bias, d_skip, ssd_norm,
           q_norm, k_norm, rpb, gm_norm, w_spatial, b_spatial, w_branch_ssd, w_branch_na, w_branch_gm, w_out,
           norm2, w_ffn_in, w_ffn_out):
    p = dict(norm1=norm1, w_in=w_in, b_gate=b_gate, conv_w=conv_w, conv_b=conv_b, a_log=a_log, dt_bias=dt_bias,
             d_skip=d_skip, ssd_norm=ssd_norm, q_norm=q_norm, k_norm=k_norm, gm_norm=gm_norm, w_spatial=w_spatial,
             b_spatial=b_spatial, w_branch_ssd=w_branch_ssd, w_branch_na=w_branch_na, w_branch_gm=w_branch_gm,
             w_out=w_out, norm2=norm2, w_ffn_in=w_ffn_in, w_ffn_out=w_ffn_out)
    b, seq, d = x.shape
    n_ctx = ctx.shape[1]
    depth = w_mod.shape[0]

    c_all = jnp.zeros((8, d), F32).at[:b].set(c).at[b].set(c_ctx)
    mod = _modulation(c_all, w_mod, b_mod)
    bias = _bias_table(rpb)
    consts = _shared_constants(n_ctx, seq)
    zero_state = jnp.zeros((b, SSD_STATE, SSD_INNER), F32)

    xc = ctx
    for l in range(depth):
        lw = _layer_weights(l, p)
        mod_x = mod[l, :b].reshape(b, 6, d)
        mod_c = mod[l, b].reshape(1, 6, d)
        last = l == depth - 1

        flat = lambda t: t.reshape(1, b * n_ctx, t.shape[-1])
        per_sample = lambda t: t.reshape(b, n_ctx, t.shape[-1])
        zc, xbcc, dtc, qc, kc, vc, ygm_c, gate_c = [per_sample(t) for t in _input_projection(flat(xc), mod_c, lw)]
        z, xbc, dt, q, k, v, y_gm, gate = _input_projection(x, mod_x, lw)

        yssd_c, s_f, s_b = _ssd(xbcc, dtc, zero_state, zero_state, consts["ctx_cos"], consts["ctx_sin"], lw, consts)
        y_ssd, _, _ = _ssd(xbc, dt, s_f, s_b, consts["rope_cos"], consts["rope_sin"], lw, consts)

        y_na = _neighbourhood_attention(q, k, v, kc, vc, bias, l)
        x = _merge_ffn(x, mod_x, gate, y_ssd, z, y_na, y_gm, lw)
        if not last:
            yna_c = _context_attention(qc, kc, vc)
            xc = per_sample(_merge_ffn(flat(xc), mod_c, flat(gate_c), flat(yssd_c), flat(zc), flat(yna_c), flat(ygm_c), lw))
    return x
```

```python
import functools
import math

import jax
import jax.numpy as jnp
from jax import lax
from jax.experimental import pallas as pl
from jax.experimental.pallas import tpu as pltpu

F32 = jnp.float32
BF16 = jnp.bfloat16

EPS = 1e-6
GRID_W = 64

SSD_INNER = 1024
SSD_HEAD_DIM = 64
SSD_HEADS = 16
SSD_GROUPS = 4
SSD_STATE = 128
SSD_CONV = 5
SSD_CHUNK = 128
SSD_BC = SSD_GROUPS * SSD_STATE
SSD_CONV_DIM = SSD_INNER + 2 * SSD_BC
ROPE_FREQS = 32
ROPE_BASE = 10000.0

NA_HEAD_DIM = 64
NA_WIDTH = 512
NA_HEADS = 8
NA_ROWS = 8
NA_COLS = 16
NA_QROWS = 4
NA_TILES_PER_STEP = 4

GM_WIDTH = 512
GM_GROUPS = 8
GM_CHUNK = 128

LANES = 128
HALO = 16
NEG = -1e30
LOG2E = math.log2(math.e)
VMEM_LIMIT = 56 * 1024 * 1024
TOKEN_TILE = 512
SSD_SEQS_PER_STEP = 4
CAST_BLOCK_BYTES = 6 << 20


def _dot(a, b):
    return jnp.dot(a, b, preferred_element_type=F32)


def _dot_nt(a, b):
    return lax.dot_general(a, b, (((1,), (1,)), ((), ())), preferred_element_type=F32)


def _silu(x):
    return x / (1.0 + jnp.exp(-x))


def _sigmoid(x):
    return 1.0 / (1.0 + jnp.exp(-x))


def _gelu_tanh(x):
    return 0.5 * x * (1.0 + jnp.tanh(math.sqrt(2.0 / math.pi) * (x + 0.044715 * (x * x * x))))


def _softplus(x):
    return jnp.maximum(x, 0.0) + jnp.log(1.0 + jnp.exp(-jnp.abs(x)))


def _split_hi_lo(v):
    hi = v.astype(BF16)
    lo = (v - hi.astype(F32)).astype(BF16)
    return hi, lo


def _const_spec(shape):
    nd = len(shape)
    return pl.BlockSpec(shape, lambda *_: (0,) * nd, pipeline_mode=pl.Buffered(1))


def _params(*semantics):
    return pltpu.CompilerParams(dimension_semantics=semantics, vmem_limit_bytes=VMEM_LIMIT)


def _mod_kernel(c_ref, w_ref, b_ref, o_ref):
    o_ref[0] = _dot(_silu(c_ref[...]), w_ref[0]) + b_ref[0]


def _modulation(c_all, w_mod, b_mod):
    depth, d, n = w_mod.shape
    tn = 1536
    return pl.pallas_call(
        _mod_kernel,
        out_shape=jax.ShapeDtypeStruct((depth, 8, n), F32),
        grid=(depth, n // tn),
        in_specs=[pl.BlockSpec((8, d), lambda l, j: (0, 0)),
                  pl.BlockSpec((1, d, tn), lambda l, j: (l, 0, j)),
                  pl.BlockSpec((1, 1, tn), lambda l, j: (l, 0, j))],
        out_specs=pl.BlockSpec((1, 8, tn), lambda l, j: (l, 0, j)),
        compiler_params=_params("arbitrary", "arbitrary"),
        name="modulation",
    )(c_all, w_mod, b_mod.reshape(depth, 1, n))


def _inproj_kernel(x_ref, mod_ref, n1_ref, wz, wxbc, wdt, wq, wk, wv, wuv, wg, dtb, bg, qn, kn, blk,
                   gmn, ws, bsp, z_o, xbc_o, dt_o, q_o, k_o, v_o, ygm_o, gate_o):
    tm = x_ref.shape[1]
    x = x_ref[0]
    xn = x * lax.rsqrt(jnp.mean(x * x, axis=-1, keepdims=True) + EPS) * n1_ref[...]
    hb = (xn * (1.0 + mod_ref[0, 1:2, :]) + mod_ref[0, 0:1, :]).astype(BF16)

    cw = 512

    def head_norm(t, w_row):
        ms = _dot((t * t).astype(BF16), blk[...])
        return t * lax.rsqrt(ms + EPS) * w_row

    qf = _dot(hb, wq[...])
    kf = _dot(hb, wk[...])
    g = _gelu_tanh(_dot(hb, wuv[...]))
    dt_o[0] = _softplus(_dot(hb, wdt[...]) + dtb[...])
    v_o[0] = _dot(hb, wv[...]).astype(BF16)
    for n0 in range(0, z_o.shape[2], cw):
        z_o[0, :, n0:n0 + cw] = _dot(hb, wz[:, n0:n0 + cw]).astype(BF16)

    q_o[0] = (head_norm(qf, qn[...]) * (NA_HEAD_DIM ** -0.5 * LOG2E)).astype(BF16)
    k_o[0] = head_norm(kf, kn[...]).astype(BF16)
    for n0 in range(0, xbc_o.shape[2], cw):
        xbc_o[0, :, n0:n0 + cw] = _dot(hb, wxbc[:, n0:n0 + cw]).astype(BF16)

    u = g[:, :GM_WIDTH]
    v = g[:, GM_WIDTH:]
    vb = (v * lax.rsqrt(jnp.mean(v * v, axis=-1, keepdims=True) + EPS) * gmn[...]).astype(BF16)
    first = lax.broadcasted_iota(jnp.int32, (GM_CHUNK, LANES), 1) < LANES // 2
    for c0 in range(0, tm, GM_CHUNK):
        for p in range(GM_WIDTH // LANES):
            cols = slice(p * LANES, (p + 1) * LANES)
            vp = vb[c0:c0 + GM_CHUNK, cols]
            zero = jnp.zeros_like(vp)
            stacked = jnp.concatenate([jnp.where(first, vp, zero), jnp.where(first, zero, vp)], axis=0)
            mixed = _dot(ws[p], stacked) + bsp[:, cols]
            ygm_o[0, c0:c0 + GM_CHUNK, cols] = (u[c0:c0 + GM_CHUNK, cols] * mixed).astype(BF16)

    for n0 in range(0, gate_o.shape[2], cw):
        gate_o[0, :, n0:n0 + cw] = _sigmoid(_dot(hb, wg[:, n0:n0 + cw]) + bg[:, n0:n0 + cw]).astype(BF16)


def _input_projection(x, mod6, lw):
    b, t, d = x.shape
    tm = min(TOKEN_TILE, t)
    tok = lambda c: pl.BlockSpec((1, tm, c), lambda i, j: (i, j, 0))
    consts = [lw["norm1"], lw["w_z"], lw["w_xbc"], lw["w_dt"], lw["w_q"], lw["w_k"], lw["w_v"], lw["w_uv"],
              lw["w_gate"], lw["dt_bias"], lw["b_gate"], lw["q_norm"], lw["k_norm"], lw["head_blk"],
              lw["gm_norm"], lw["w_spatial"], lw["b_spatial"]]
    widths = [(SSD_INNER, BF16), (SSD_CONV_DIM, BF16), (LANES, F32), (NA_WIDTH, BF16), (NA_WIDTH, BF16),
              (NA_WIDTH, BF16), (GM_WIDTH, BF16), (3 * d, BF16)]
    return pl.pallas_call(
        _inproj_kernel,
        out_shape=[jax.ShapeDtypeStruct((b, t, c), ty) for c, ty in widths],
        grid=(b, t // tm),
        in_specs=[tok(d), pl.BlockSpec((1, 6, d), lambda i, j: (i, 0, 0))] + [_const_spec(a.shape) for a in consts],
        out_specs=[tok(c) for c, _ in widths],
        compiler_params=_params("parallel", "parallel"),
        name="input_projection",
    )(x, mod6, *consts)


def _ssd_chunks(seqs, a_row, e2_ref, tri2_ref, reverse, head_off):
    q = seqs[0][0].shape[0]
    last = 0 if reverse else q - 1
    gw = SSD_INNER // SSD_GROUPS
    hpg = gw // SSD_HEAD_DIM
    groups = range(SSD_GROUPS)
    gcols = lambda g: slice(g * gw, (g + 1) * gw)
    ncols = lambda g: slice(g * SSD_STATE, (g + 1) * SSD_STATE)
    tri2 = tri2_ref[...]

    acums = []
    for xs, bm, cm, dt, h_ref in seqs:
        a = dt * a_row
        a1 = a.astype(BF16)
        r1 = a - a1.astype(F32)
        a2 = r1.astype(BF16)
        a3 = (r1 - a2.astype(F32)).astype(BF16)
        acums.append(_dot(tri2, jnp.concatenate([a1, a2], axis=0)) + _dot(tri2[:, :q], a3))

    wides = []
    for (xs, bm, cm, dt, h_ref), acum in zip(seqs, acums):
        a_last = acum[last:last + 1, :]
        per_head = jnp.concatenate([dt, dt * jnp.exp2(a_last - acum),
                                    jnp.broadcast_to(jnp.exp2(a_last), (8, LANES))], axis=0)
        hi, lo = _split_hi_lo(per_head)
        wides.append(_dot(jnp.concatenate([hi, lo], axis=1), e2_ref[...]))

    cbs, y_inters, bts, h_prevs = [], [], [], []
    for xs, bm, cm, dt, h_ref in seqs:
        cgbs = [cm[:, ncols(g)].astype(BF16) for g in groups]
        h_prev = [h_ref[:, gcols(g)] for g in groups]
        cbs.append([_dot_nt(cgbs[g], bm[:, ncols(g)].astype(BF16)) for g in groups])
        y_inters.append([_dot(cgbs[g], h_prev[g].astype(BF16)) for g in groups])
        bts.append([bm[:, ncols(g)].astype(F32).T.astype(BF16) for g in groups])
        h_prevs.append(h_prev)

    row = lax.broadcasted_iota(jnp.int32, (q, q), 0)
    col = lax.broadcasted_iota(jnp.int32, (q, q), 1)
    in_scan = (col >= row) if reverse else (col <= row)
    lane = lax.broadcasted_iota(jnp.int32, (q, LANES), 1)
    lane_head = lax.broadcasted_iota(jnp.int32, (q, gw), 1) // SSD_HEAD_DIM

    lhs, rhs, dec_outs, xw_bs, h_decs = [], [], [], [], []
    for si, (xs, bm, cm, dt, h_ref) in enumerate(seqs):
        acum, wide = acums[si], wides[si]
        acum_t = acum.T
        xdt_b = (xs * wide[0:q]).astype(BF16)
        xw_bs.append((xs * wide[q:2 * q]).astype(BF16))
        h_decs.append(wide[2 * q:2 * q + 1])
        for g in groups:
            xg = xdt_b[:, gcols(g)]
            ms, decs = [], []
            for j in range(hpg):
                hc = head_off + g * hpg + j
                colx = jnp.broadcast_to(acum[:, hc:hc + 1], (q, q))
                decay = jnp.exp2(jnp.where(in_scan, colx - acum_t[hc:hc + 1, :], NEG))
                ms.append((cbs[si][g] * decay).astype(BF16))
                decs.append(jnp.exp2(colx))
            lhs.append(jnp.concatenate(ms, axis=1))
            rhs.append(jnp.concatenate([jnp.where(lane_head == j, xg, jnp.zeros_like(xg)) for j in range(hpg)], axis=0))
            dec_outs.append(jnp.concatenate([jnp.where(lane < SSD_HEAD_DIM, decs[2 * i], decs[2 * i + 1])
                                             for i in range(hpg // 2)], axis=1))
    y_intra = [_dot(l, r) for l, r in zip(lhs, rhs)]
    h_add = [_dot(bts[si][g], xw_bs[si][:, gcols(g)]) for si in range(len(seqs)) for g in groups]
    ys = []
    for si, (xs, bm, cm, dt, h_ref) in enumerate(seqs):
        for g in groups:
            h_ref[:, gcols(g)] = h_prevs[si][g] * h_decs[si][:, gcols(g)] + h_add[si * SSD_GROUPS + g]
        ys.append(jnp.concatenate([y_intra[si * SSD_GROUPS + g] + y_inters[si][g] * dec_outs[si * SSD_GROUPS + g]
                                   for g in groups], axis=1))
    return ys


def _ssd_fwd_kernel(cur_ref, prev_ref, next_ref, dt_ref, cos_ref, sin_ref, cw_ref, cb_ref, shift_ref, a_ref, e_ref,
                    tri_ref, dsk_ref, h0_ref, xbc_o, y_o, hn_o, h_ref):
    i = pl.program_id(1)
    nc = pl.num_programs(1)
    nb, q, _ = cur_ref.shape

    @pl.when(i == 0)
    def _():
        h_ref[...] = h0_ref[...]

    pv = jnp.where(i == 0, 0.0, 1.0).astype(BF16)
    nv = jnp.where(i == nc - 1, 0.0, 1.0).astype(BF16)
    lane = lax.broadcasted_iota(jnp.int32, (q, LANES), 1)
    cos = cos_ref[...]
    sin = sin_ref[...]

    def rope(t):
        sw = jnp.where((lane & ROPE_FREQS) == 0, pltpu.roll(t, LANES - ROPE_FREQS, 1), pltpu.roll(t, ROPE_FREQS, 1))
        return t * cos + sw * sin

    cwid = 256
    seqs = []
    for s in range(nb):
        ext = jnp.concatenate([prev_ref[s] * pv, cur_ref[s], next_ref[s] * nv], axis=0)
        parts = []
        for c0 in range(0, SSD_CONV_DIM, cwid):
            e = ext[:, c0:c0 + cwid]
            taps = jnp.concatenate([e * cw_ref[k:k + 1, c0:c0 + cwid] for k in range(SSD_CONV)], axis=0)
            parts.append(_silu(_dot(shift_ref[...], taps) + cb_ref[:, c0:c0 + cwid]))
        xs = jnp.concatenate(parts[:SSD_INNER // cwid], axis=1)
        bc = jnp.concatenate(parts[SSD_INNER // cwid:], axis=1)
        bc = jnp.concatenate([rope(bc[:, g * LANES:(g + 1) * LANES]) for g in range(2 * SSD_GROUPS)], axis=1)
        xbc_o[s, :, :SSD_INNER] = xs.astype(BF16)
        xbc_o[s, :, SSD_INNER:] = bc.astype(BF16)
        seqs.append((xs, bc[:, :SSD_BC], bc[:, SSD_BC:], dt_ref[s], h_ref.at[s]))
    ys = _ssd_chunks(seqs, a_ref[...], e_ref, tri_ref, False, 0)
    for s in range(nb):
        y_o[s] = (ys[s] + dsk_ref[...] * seqs[s][0]).astype(BF16)

    @pl.when(i == nc - 1)
    def _():
        hn_o[...] = h_ref[...]


def _ssd_bwd_kernel(xbc_ref, dt_ref, yf_ref, a_ref, e_ref, tri_ref, h0_ref, y_o, hn_o, h_ref):
    i = pl.program_id(1)

    @pl.when(i == 0)
    def _():
        h_ref[...] = h0_ref[...]

    nb = xbc_ref.shape[0]
    seqs = [(xbc_ref[s, :, :SSD_INNER].astype(F32), xbc_ref[s, :, SSD_INNER:SSD_INNER + SSD_BC],
             xbc_ref[s, :, SSD_INNER + SSD_BC:], dt_ref[s], h_ref.at[s]) for s in range(nb)]
    ys = _ssd_chunks(seqs, a_ref[...], e_ref, tri_ref, True, SSD_HEADS)
    for s in range(nb):
        y_o[s] = (ys[s] + yf_ref[s].astype(F32)).astype(BF16)

    @pl.when(i == pl.num_programs(1) - 1)
    def _():
        hn_o[...] = h_ref[...]


def _ssd(xbc, dt, h0_f, h0_b, rope_cos, rope_sin, lw, consts):
    b, t, _ = xbc.shape
    q = SSD_CHUNK
    nc = t // q
    nb = SSD_SEQS_PER_STEP if b % SSD_SEQS_PER_STEP == 0 else 1
    hb = q // HALO
    n_hb = t // HALO
    chunk = lambda c: pl.BlockSpec((nb, q, c), lambda i, j: (i, j, 0))
    state = pl.BlockSpec((nb, SSD_STATE, SSD_INNER), lambda i, j: (i, 0, 0))
    state_shape = jax.ShapeDtypeStruct((b, SSD_STATE, SSD_INNER), F32)
    scratch_h = pltpu.VMEM((nb, SSD_STATE, SSD_INNER), F32)

    xbc_c, y_f, hn_f = pl.pallas_call(
        _ssd_fwd_kernel,
        out_shape=[jax.ShapeDtypeStruct((b, t, SSD_CONV_DIM), BF16), jax.ShapeDtypeStruct((b, t, SSD_INNER), BF16),
                   state_shape],
        grid=(b // nb, nc),
        in_specs=[chunk(SSD_CONV_DIM),
                  pl.BlockSpec((nb, HALO, SSD_CONV_DIM), lambda i, j: (i, jnp.maximum(j * hb - 1, 0), 0)),
                  pl.BlockSpec((nb, HALO, SSD_CONV_DIM), lambda i, j: (i, jnp.minimum((j + 1) * hb, n_hb - 1), 0)),
                  chunk(LANES),
                  pl.BlockSpec((q, LANES), lambda i, j: (j, 0)),
                  pl.BlockSpec((q, LANES), lambda i, j: (j, 0)),
                  _const_spec(lw["conv_w"].shape), _const_spec(lw["conv_b"].shape),
                  _const_spec(consts["conv_shift"].shape),
                  _const_spec(lw["a_fwd"].shape), _const_spec(consts["e_fwd"].shape),
                  _const_spec(consts["tri_fwd"].shape), _const_spec(lw["d_skip"].shape), state],
        out_specs=[chunk(SSD_CONV_DIM), chunk(SSD_INNER), state],
        scratch_shapes=[scratch_h],
        compiler_params=_params("parallel", "arbitrary"),
        name="ssd_forward",
    )(xbc, xbc, xbc, dt, rope_cos, rope_sin, lw["conv_w"], lw["conv_b"], consts["conv_shift"],
      lw["a_fwd"], consts["e_fwd"], consts["tri_fwd"], lw["d_skip"], h0_f)

    rchunk = lambda c: pl.BlockSpec((nb, q, c), lambda i, j: (i, nc - 1 - j, 0))
    y, hn_b = pl.pallas_call(
        _ssd_bwd_kernel,
        out_shape=[jax.ShapeDtypeStruct((b, t, SSD_INNER), BF16), state_shape],
        grid=(b // nb, nc),
        in_specs=[rchunk(SSD_CONV_DIM), rchunk(LANES), rchunk(SSD_INNER),
                  _const_spec(lw["a_bwd"].shape), _const_spec(consts["e_bwd"].shape),
                  _const_spec(consts["tri_bwd"].shape), state],
        out_specs=[rchunk(SSD_INNER), state],
        scratch_shapes=[scratch_h],
        compiler_params=_params("parallel", "arbitrary"),
        name="ssd_backward",
    )(xbc_c, dt, y_f, lw["a_bwd"], consts["e_bwd"], consts["tri_bwd"], h0_b)
    return y, hn_f, hn_b


def _bias_kernel(rpb_ref, o_ref):
    lh = pl.program_id(0)
    n_ri = 2 * NA_ROWS - 1
    n_ci = 2 * NA_COLS - 1
    lane = lax.broadcasted_iota(jnp.int32, (GRID_W, LANES), 1)
    qc = lax.broadcasted_iota(jnp.int32, (GRID_W, LANES), 0)
    kc = lane % GRID_W
    cs = jnp.clip(qc - NA_COLS // 2, 0, GRID_W - NA_COLS)
    col_ok = (kc >= cs) & (kc < cs + NA_COLS)
    ci = jnp.clip(kc - qc + (NA_COLS - 1), 0, n_ci - 1)
    tiles = []
    for ri in range(n_ri):
        base = (lh * n_ri + ri) * n_ci
        acc = jnp.full((GRID_W, LANES), NEG, F32)
        for c in range(n_ci):
            acc = jnp.where(col_ok & (ci == c), rpb_ref[base + c] * LOG2E, acc)
        tiles.append(acc)
    masked = jnp.full((GRID_W, LANES), NEG, F32)
    tiles = [masked] + tiles + [masked]
    for e in range(n_ri + 1):
        o_ref[0, e] = jnp.where(lane < GRID_W, tiles[e], tiles[e + 1])


def _bias_table(rpb):
    depth, heads, n_ri, n_ci = rpb.shape
    return pl.pallas_call(
        _bias_kernel,
        out_shape=jax.ShapeDtypeStruct((depth * heads, n_ri + 1, GRID_W, LANES), F32),
        grid=(depth * heads,),
        in_specs=[pl.BlockSpec(memory_space=pltpu.SMEM)],
        out_specs=pl.BlockSpec((1, n_ri + 1, GRID_W, LANES), lambda i: (i, 0, 0, 0)),
        compiler_params=pltpu.CompilerParams(dimension_semantics=("arbitrary",)),
        name="na_bias_table",
    )(rpb.reshape(-1))


def _na_tile(q_ref, k_ref, v_ref, kc_ref, vc_ref, bias_ref, o_ref, row0, start, nw, plan):
    tq = NA_QROWS * GRID_W
    qrows = slice(row0, row0 + tq)
    n_ctx = kc_ref.shape[1]
    n_cb = n_ctx // LANES
    lane_q = lax.broadcasted_iota(jnp.int32, (tq, LANES), 1)
    lane_k = lax.broadcasted_iota(jnp.int32, (n_ctx + nw * GRID_W, LANES), 1)
    n_pairs = NA_WIDTH // LANES
    own = [lambda lane, hh=hh: (lane < NA_HEAD_DIM) == (hh == 0) for hh in range(2)]
    col = lambda p: slice(p * LANES, (p + 1) * LANES)
    scores_all = []
    for p in range(n_pairs):
        qp = q_ref[0, qrows, col(p)]
        keys = jnp.concatenate([kc_ref[0, :, col(p)], k_ref[0, pl.ds(start, nw * GRID_W), col(p)]], axis=0)
        for hh in range(2):
            scores_all.append(_dot_nt(jnp.where(own[hh](lane_q), qp, jnp.zeros_like(qp)), keys))
    p_mats = []
    for h, scores in enumerate(scores_all):
        p_rows = []
        for qi, row_plan in enumerate(plan):
            rows = slice(qi * GRID_W, (qi + 1) * GRID_W)
            blocks = [scores[rows, m * LANES:(m + 1) * LANES] for m in range(n_cb)]
            for m, (ent, ok) in enumerate(row_plan):
                if ok is None:
                    blocks.append(None)
                    continue
                sb = scores[rows, (n_cb + m) * LANES:(n_cb + m + 1) * LANES] + bias_ref[h, ent]
                blocks.append(sb if ok is True else jnp.where(ok, sb, NEG))
            live = [sb for sb in blocks if sb is not None]
            mx = live[0]
            for sb in live[1:]:
                mx = jnp.maximum(mx, sb)
            mx = jnp.max(mx, axis=-1, keepdims=True)
            p_rows.append(jnp.concatenate(
                [jnp.zeros((GRID_W, LANES), BF16) if sb is None else jnp.exp2(sb - mx).astype(BF16)
                 for sb in blocks], axis=1))
        p_mats.append(jnp.concatenate(p_rows, axis=0))
    for p in range(n_pairs):
        vals = jnp.concatenate([vc_ref[0, :, col(p)], v_ref[0, pl.ds(start, nw * GRID_W), col(p)]], axis=0)
        nums = [_dot(p_mats[2 * p + hh], jnp.where(own[hh](lane_k), vals, jnp.ones_like(vals))) for hh in range(2)]
        num = jnp.where(lane_q < NA_HEAD_DIM, nums[0], nums[1])
        den = pltpu.roll(jnp.where(lane_q < NA_HEAD_DIM, nums[1], nums[0]), NA_HEAD_DIM, 1)
        o_ref[0, qrows, col(p)] = (num / den).astype(BF16)


def _na_kernel(q_ref, k_ref, v_ref, kc_ref, vc_ref, bias_ref, o_ref, *, nw, rows_n):
    rt = NA_QROWS
    half = NA_ROWS // 2
    lane_r = lax.broadcasted_iota(jnp.int32, (GRID_W, LANES), 1)
    first_half = lane_r < GRID_W
    nblk = nw // 2
    args = (q_ref, k_ref, v_ref, kc_ref, vc_ref, bias_ref, o_ref)

    def interior_tile(r0, row0):
        plan = []
        for qi in range(rt):
            row_plan = []
            for m in range(nblk):
                ok0 = qi <= 2 * m < qi + NA_ROWS
                ok1 = qi <= 2 * m + 1 < qi + NA_ROWS
                ok = True if ok0 and ok1 else None if not (ok0 or ok1) else first_half if ok0 else ~first_half
                row_plan.append((half + 2 * m - qi, ok))
            plan.append(row_plan)
        _na_tile(*args, row0, pl.multiple_of((r0 - half) * GRID_W, GRID_W), nw, plan)

    def clipped_tile(r0, row0):
        base = jnp.clip(r0 - half, 0, rows_n - nw)
        plan = []
        for qi in range(rt):
            r = r0 + qi
            rs = jnp.clip(r - half, 0, rows_n - NA_ROWS)
            row_plan = []
            for m in range(nblk):
                j0 = base + 2 * m
                ok0 = (j0 >= rs) & (j0 < rs + NA_ROWS)
                ok1 = (j0 + 1 >= rs) & (j0 + 1 < rs + NA_ROWS)
                ok = jnp.where(first_half, ok0.astype(jnp.int32), ok1.astype(jnp.int32)) > 0
                row_plan.append((jnp.clip(j0 - r + NA_ROWS, 0, 2 * NA_ROWS - 1), ok))
            plan.append(row_plan)
        _na_tile(*args, row0, pl.multiple_of(base * GRID_W, GRID_W), nw, plan)

    for sub in range(NA_TILES_PER_STEP):
        r0 = (pl.program_id(1) * NA_TILES_PER_STEP + sub) * rt
        interior = (r0 >= half) & (r0 - half <= rows_n - nw)
        pl.when(interior)(functools.partial(interior_tile, r0, sub * rt * GRID_W))
        pl.when(~interior)(functools.partial(clipped_tile, r0, sub * rt * GRID_W))


def _neighbourhood_attention(q, k, v, kc, vc, bias, layer):
    b, t, w = q.shape
    n_ctx = kc.shape[1]
    rows_n = t // GRID_W
    nw = NA_QROWS + NA_ROWS
    nw += nw % 2
    tq = NA_TILES_PER_STEP * NA_QROWS * GRID_W
    whole = lambda n: pl.BlockSpec((1, n, w), lambda i, j: (i, 0, 0))
    return pl.pallas_call(
        functools.partial(_na_kernel, nw=nw, rows_n=rows_n),
        out_shape=jax.ShapeDtypeStruct((b, t, w), BF16),
        grid=(b, t // tq),
        in_specs=[pl.BlockSpec((1, tq, w), lambda i, j: (i, j, 0)), whole(t), whole(t), whole(n_ctx), whole(n_ctx),
                  pl.BlockSpec((NA_HEADS,) + bias.shape[1:], lambda i, j: (layer, 0, 0, 0),
                               pipeline_mode=pl.Buffered(1))],
        out_specs=pl.BlockSpec((1, tq, w), lambda i, j: (i, j, 0)),
        compiler_params=_params("parallel", "arbitrary"),
        name="neighbourhood_attention",
    )(q, k, v, kc, vc, bias)


def _ctx_attn_kernel(q_ref, k_ref, v_ref, o_ref):
    n = q_ref.shape[1]
    lane = lax.broadcasted_iota(jnp.int32, (n, LANES), 1)
    for p in range(NA_WIDTH // LANES):
        cols = slice(p * LANES, (p + 1) * LANES)
        qp = q_ref[0, :, cols]
        kp = k_ref[0, :, cols]
        vp = v_ref[0, :, cols]
        nums = []
        for hh in range(2):
            own = (lane < NA_HEAD_DIM) == (hh == 0)
            s = _dot_nt(jnp.where(own, qp, jnp.zeros_like(qp)), kp)
            pm = jnp.exp2(s - jnp.max(s, axis=-1, keepdims=True)).astype(BF16)
            nums.append(_dot(pm, jnp.where(own, vp, jnp.ones_like(vp))))
        num = jnp.where(lane < NA_HEAD_DIM, nums[0], nums[1])
        den = pltpu.roll(jnp.where(lane < NA_HEAD_DIM, nums[1], nums[0]), NA_HEAD_DIM, 1)
        o_ref[0, :, cols] = (num / den).astype(BF16)


def _context_attention(q, k, v):
    b, n, w = q.shape
    spec = pl.BlockSpec((1, n, w), lambda i: (i, 0, 0))
    return pl.pallas_call(
        _ctx_attn_kernel,
        out_shape=jax.ShapeDtypeStruct((b, n, w), BF16),
        grid=(b,),
        in_specs=[spec, spec, spec],
        out_specs=spec,
        compiler_params=_params("parallel"),
        name="context_attention",
    )(q, k, v)


def _merge_ffn_kernel(x_ref, mod_ref, gate_ref, yssd_ref, z_ref, yna_ref, ygm_ref, sn_ref, wa, wb, wc, wo,
                      n2_ref, wfi, wfo, o_ref, *, ffn_chunk):
    d = x_ref.shape[2]
    y = yssd_ref[0].astype(F32) * _silu(z_ref[0].astype(F32))
    y = (y * lax.rsqrt(jnp.mean(y * y, axis=-1, keepdims=True) + EPS) * sn_ref[...]).astype(BF16)
    mixed = (gate_ref[0, :, 0:d].astype(F32) * _dot(y, wa[...])
             + gate_ref[0, :, d:2 * d].astype(F32) * _dot(yna_ref[0], wb[...])
             + gate_ref[0, :, 2 * d:3 * d].astype(F32) * _dot(ygm_ref[0], wc[...]))
    x1 = x_ref[0] + mod_ref[0, 2:3, :] * _dot(mixed.astype(BF16), wo[...])
    xn = x1 * lax.rsqrt(jnp.mean(x1 * x1, axis=-1, keepdims=True) + EPS) * n2_ref[...]
    hb = (xn * (1.0 + mod_ref[0, 4:5, :]) + mod_ref[0, 3:4, :]).astype(BF16)
    hid = wfo.shape[0]
    acc = jnp.zeros_like(x1)
    for c0 in range(0, hid, ffn_chunk):
        a = _dot(hb, wfi[:, c0:c0 + ffn_chunk])
        g = _dot(hb, wfi[:, hid + c0:hid + c0 + ffn_chunk])
        acc = acc + _dot((_silu(a) * g).astype(BF16), wfo[c0:c0 + ffn_chunk, :])
    o_ref[0] = x1 + mod_ref[0, 5:6, :] * acc


def _merge_ffn(x, mod6, gate, y_ssd, z, y_na, y_gm, lw):
    b, t, d = x.shape
    tm = min(TOKEN_TILE, t)
    tok = lambda c: pl.BlockSpec((1, tm, c), lambda i, j: (i, j, 0))
    consts = [lw["ssd_norm"], lw["w_branch_ssd"], lw["w_branch_na"], lw["w_branch_gm"], lw["w_out"], lw["norm2"],
              lw["w_ffn_in"], lw["w_ffn_out"]]
    return pl.pallas_call(
        functools.partial(_merge_ffn_kernel, ffn_chunk=256),
        out_shape=jax.ShapeDtypeStruct((b, t, d), F32),
        grid=(b, t // tm),
        in_specs=[tok(d), pl.BlockSpec((1, 6, d), lambda i, j: (i, 0, 0)),
                  tok(3 * d), tok(SSD_INNER), tok(SSD_INNER), tok(NA_WIDTH), tok(GM_WIDTH)]
                 + [_const_spec(a.shape) for a in consts],
        out_specs=tok(d),
        compiler_params=_params("parallel", "parallel"),
        name="merge_out_ffn",
    )(x, mod6, gate, y_ssd, z, y_na, y_gm, *consts)


def _cast_kernel(w_ref, o_ref):
    o_ref[...] = w_ref[0].astype(BF16)


def _weight_bf16(w, l):
    _, rows, cols = w.shape
    rb = next(r for r in (rows, rows // 2, rows // 4, rows // 8)
              if r * cols * 4 <= CAST_BLOCK_BYTES and r % 16 == 0)
    return pl.pallas_call(
        _cast_kernel,
        out_shape=jax.ShapeDtypeStruct((rows, cols), BF16),
        grid=(rows // rb, 1),
        in_specs=[pl.BlockSpec((1, rb, cols), lambda i, j: (l, i, j))],
        out_specs=pl.BlockSpec((rb, cols), lambda i, j: (i, j)),
        compiler_params=_params("parallel", "parallel"),
        name="weight_to_bf16",
    )(w)


def _shared_constants(n_ctx, seq):
    pos = jnp.arange(seq)
    freqs = ROPE_BASE ** (-jnp.arange(ROPE_FREQS, dtype=F32) / ROPE_FREQS)
    ang_row = (pos // GRID_W).astype(F32)[:, None] * freqs
    ang_col = (pos % GRID_W).astype(F32)[:, None] * freqs
    cos = jnp.concatenate([jnp.cos(ang_row), jnp.cos(ang_row), jnp.cos(ang_col), jnp.cos(ang_col)], axis=1)
    sin = jnp.concatenate([-jnp.sin(ang_row), jnp.sin(ang_row), -jnp.sin(ang_col), jnp.sin(ang_col)], axis=1)

    r = jnp.arange(SSD_CHUNK)
    twice = lambda m, axis: jnp.concatenate([m, m], axis=axis).astype(BF16)
    lane_head = jnp.arange(SSD_INNER) // SSD_HEAD_DIM
    rows = jnp.arange(LANES)
    src = jnp.arange(SSD_CHUNK + 2 * HALO)
    conv_shift = jnp.concatenate([(src[None, :] == r[:, None] + HALO + k - SSD_CONV // 2) for k in range(SSD_CONV)],
                                 axis=1).astype(BF16)
    return dict(rope_cos=cos, rope_sin=sin,
                ctx_cos=jnp.ones((n_ctx, LANES), F32), ctx_sin=jnp.zeros((n_ctx, LANES), F32),
                tri_fwd=twice(r[None, :] <= r[:, None], 1), tri_bwd=twice(r[None, :] >= r[:, None], 1),
                e_fwd=twice(rows[:, None] == lane_head[None, :], 0),
                e_bwd=twice(rows[:, None] == lane_head[None, :] + SSD_HEADS, 0),
                conv_shift=conv_shift)


def _layer_weights(l, p):
    d = p["w_in"].shape[1]
    sizes = (SSD_INNER, SSD_CONV_DIM, 2 * SSD_HEADS, NA_WIDTH, NA_WIDTH, NA_WIDTH, 2 * GM_WIDTH, 3 * d)
    names = ("w_z", "w_xbc", "w_dt", "w_q", "w_k", "w_v", "w_uv", "w_gate")
    lw, start = {}, 0
    w_in = p["w_in"][l]
    for name, size in zip(names, sizes):
        lw[name] = w_in[:, start:start + size].astype(BF16)
        start += size
    pad_lanes = lambda v: jnp.pad(v, (0, LANES - v.shape[0])).reshape(1, LANES)
    lw["w_dt"] = jnp.pad(lw["w_dt"], ((0, 0), (0, LANES - 2 * SSD_HEADS)))
    lw["dt_bias"] = pad_lanes(p["dt_bias"][l].reshape(-1))
    a = -jnp.exp(p["a_log"][l].astype(F32))
    lw["a_fwd"] = pad_lanes(a[0] * LOG2E)
    lw["a_bwd"] = pad_lanes(jnp.concatenate([jnp.zeros((SSD_HEADS,), F32), a[1] * LOG2E]))
    row = lambda v: v.reshape(1, -1)
    lw["norm1"] = row(p["norm1"][l])
    lw["norm2"] = row(p["norm2"][l])
    lw["b_gate"] = row(p["b_gate"][l])
    lw["q_norm"] = row(jnp.tile(p["q_norm"][l], NA_HEADS))
    lw["k_norm"] = row(jnp.tile(p["k_norm"][l], NA_HEADS))
    head = jnp.arange(NA_WIDTH) // NA_HEAD_DIM
    lw["head_blk"] = ((head[:, None] == head[None, :]).astype(F32) / NA_HEAD_DIM).astype(BF16)
    lw["gm_norm"] = row(p["gm_norm"][l])
    w_s = p["w_spatial"][l].astype(BF16)
    lw["w_spatial"] = jnp.concatenate([w_s[0::2], w_s[1::2]], axis=2)
    lw["b_spatial"] = jnp.repeat(p["b_spatial"][l].T, GM_WIDTH // GM_GROUPS, axis=1)
    lw["conv_w"] = jnp.pad(p["conv_w"][l], ((0, 8 - SSD_CONV), (0, 0))).astype(BF16)
    lw["conv_b"] = row(p["conv_b"][l])
    lw["d_skip"] = row(jnp.repeat(p["d_skip"][l], SSD_HEAD_DIM))
    lw["ssd_norm"] = row(p["ssd_norm"][l])
    for name in ("w_branch_ssd", "w_branch_na", "w_branch_gm", "w_out", "w_ffn_in", "w_ffn_out"):
        lw[name] = _weight_bf16(p[name], l)
    return lw


def kernel(x, c, ctx, c_ctx, w_mod, b_mod, norm1, w_in, b_gate, conv_w, conv_b, a_log, dt_bias, d_skip, ssd_norm,
           q_norm, k_norm, rpb, gm_norm, w_spatial, b_spatial, w_branch_ssd, w_branch_na, w_branch_gm, w_out,
           norm2, w_ffn_in, w_ffn_out):
    p = dict(norm1=norm1, w_in=w_in, b_gate=b_gate, conv_w=conv_w, conv_b=conv_b, a_log=a_log, dt_bias=dt_bias,
             d_skip=d_skip, ssd_norm=ssd_norm, q_norm=q_norm, k_norm=k_norm, gm_norm=gm_norm, w_spatial=w_spatial,
             b_spatial=b_spatial, w_branch_ssd=w_branch_ssd, w_branch_na=w_branch_na, w_branch_gm=w_branch_gm,
             w_out=w_out, norm2=norm2, w_ffn_in=w_ffn_in, w_ffn_out=w_ffn_out)
    b, seq, d = x.shape
    n_ctx = ctx.shape[1]
    depth = w_mod.shape[0]

    c_all = jnp.zeros((8, d), F32).at[:b].set(c).at[b].set(c_ctx)
    mod = _modulation(c_all, w_mod, b_mod)
    bias = _bias_table(rpb)
    consts = _shared_constants(n_ctx, seq)
    zero_state = jnp.zeros((b, SSD_STATE, SSD_INNER), F32)

    xc = ctx
    for l in range(depth):
        lw = _layer_weights(l, p)
        mod_x = mod[l, :b].reshape(b, 6, d)
        mod_c = mod[l, b].reshape(1, 6, d)
        last = l == depth - 1

        flat = lambda t: t.reshape(1, b * n_ctx, t.shape[-1])
        per_sample = lambda t: t.reshape(b, n_ctx, t.shape[-1])
        zc, xbcc, dtc, qc, kc, vc, ygm_c, gate_c = [per_sample(t) for t in _input_projection(flat(xc), mod_c, lw)]
        z, xbc, dt, q, k, v, y_gm, gate = _input_projection(x, mod_x, lw)

        yssd_c, s_f, s_b = _ssd(xbcc, dtc, zero_state, zero_state, consts["ctx_cos"], consts["ctx_sin"], lw, consts)
        y_ssd, _, _ = _ssd(xbc, dt, s_f, s_b, consts["rope_cos"], consts["rope_sin"], lw, consts)

        y_na = _neighbourhood_attention(q, k, v, kc, vc, bias, l)
        x = _merge_ffn(x, mod_x, gate, y_ssd, z, y_na, y_gm, lw)
        if not last:
            yna_c = _context_attention(qc, kc, vc)
            xc = per_sample(_merge_ffn(flat(xc), mod_c, flat(gate_c), flat(yssd_c), flat(zc), flat(yna_c), flat(ygm_c), lw))
    return x
```

```python
import functools
import math

import jax
import jax.numpy as jnp
from jax import lax
from jax.experimental import pallas as pl
from jax.experimental.pallas import tpu as pltpu

F32 = jnp.float32
BF16 = jnp.bfloat16

EPS = 1e-6
GRID_W = 64

SSD_INNER = 1024
SSD_HEAD_DIM = 64
SSD_HEADS = 16
SSD_GROUPS = 4
SSD_STATE = 128
SSD_CONV = 5
SSD_CHUNK = 128
SSD_BC = SSD_GROUPS * SSD_STATE
SSD_CONV_DIM = SSD_INNER + 2 * SSD_BC
ROPE_FREQS = 32
ROPE_BASE = 10000.0

NA_HEAD_DIM = 64
NA_WIDTH = 512
NA_HEADS = 8
NA_ROWS = 8
NA_COLS = 16
NA_QROWS = 4
NA_TILES_PER_STEP = 4

GM_WIDTH = 512
GM_GROUPS = 8
GM_CHUNK = 128

LANES = 128
HALO = 16
NEG = -1e30
LOG2E = math.log2(math.e)
VMEM_LIMIT = 56 * 1024 * 1024
TOKEN_TILE = 512
SSD_SEQS_PER_STEP = 4
CAST_BLOCK_BYTES = 3 << 20


def _dot(a, b):
    return jnp.dot(a, b, preferred_element_type=F32)


def _dot_nt(a, b):
    return lax.dot_general(a, b, (((1,), (1,)), ((), ())), preferred_element_type=F32)


def _silu(x):
    return x / (1.0 + jnp.exp(-x))


def _sigmoid(x):
    return 1.0 / (1.0 + jnp.exp(-x))


def _gelu_tanh(x):
    return 0.5 * x * (1.0 + jnp.tanh(math.sqrt(2.0 / math.pi) * (x + 0.044715 * (x * x * x))))


def _softplus(x):
    return jnp.maximum(x, 0.0) + jnp.log(1.0 + jnp.exp(-jnp.abs(x)))


def _split_hi_lo(v):
    hi = v.astype(BF16)
    lo = (v - hi.astype(F32)).astype(BF16)
    return hi, lo


def _const_spec(shape):
    nd = len(shape)
    return pl.BlockSpec(shape, lambda *_: (0,) * nd, pipeline_mode=pl.Buffered(1))


def _params(*semantics):
    return pltpu.CompilerParams(dimension_semantics=semantics, vmem_limit_bytes=VMEM_LIMIT)


def _mod_kernel(c_ref, w_ref, b_ref, o_ref):
    o_ref[0] = _dot(_silu(c_ref[...]), w_ref[0]) + b_ref[0]


def _modulation(c_all, w_mod, b_mod):
    depth, d, n = w_mod.shape
    tn = 1536
    return pl.pallas_call(
        _mod_kernel,
        out_shape=jax.ShapeDtypeStruct((depth, 8, n), F32),
        grid=(depth, n // tn),
        in_specs=[pl.BlockSpec((8, d), lambda l, j: (0, 0)),
                  pl.BlockSpec((1, d, tn), lambda l, j: (l, 0, j)),
                  pl.BlockSpec((1, 1, tn), lambda l, j: (l, 0, j))],
        out_specs=pl.BlockSpec((1, 8, tn), lambda l, j: (l, 0, j)),
        compiler_params=_params("arbitrary", "arbitrary"),
        name="modulation",
    )(c_all, w_mod, b_mod.reshape(depth, 1, n))


def _inproj_kernel(x_ref, mod_ref, n1_ref, wz, wxbc, wdt, wq, wk, wv, wuv, wg, dtb, bg, qn, kn, blk,
                   gmn, ws, bsp, z_o, xbc_o, dt_o, q_o, k_o, v_o, ygm_o, gate_o):
    tm = x_ref.shape[1]
    x = x_ref[0]
    xn = x * lax.rsqrt(jnp.mean(x * x, axis=-1, keepdims=True) + EPS) * n1_ref[...]
    hb = (xn * (1.0 + mod_ref[0, 1:2, :]) + mod_ref[0, 0:1, :]).astype(BF16)

    cw = 512

    def head_norm(t, w_row):
        ms = _dot((t * t).astype(BF16), blk[...])
        return t * lax.rsqrt(ms + EPS) * w_row

    qf = _dot(hb, wq[...])
    kf = _dot(hb, wk[...])
    g = _gelu_tanh(_dot(hb, wuv[...]))
    dt_o[0] = _softplus(_dot(hb, wdt[...]) + dtb[...])
    v_o[0] = _dot(hb, wv[...]).astype(BF16)
    for n0 in range(0, z_o.shape[2], cw):
        z_o[0, :, n0:n0 + cw] = _dot(hb, wz[:, n0:n0 + cw]).astype(BF16)

    q_o[0] = (head_norm(qf, qn[...]) * (NA_HEAD_DIM ** -0.5 * LOG2E)).astype(BF16)
    k_o[0] = head_norm(kf, kn[...]).astype(BF16)
    for n0 in range(0, xbc_o.shape[2], cw):
        xbc_o[0, :, n0:n0 + cw] = _dot(hb, wxbc[:, n0:n0 + cw]).astype(BF16)

    u = g[:, :GM_WIDTH]
    v = g[:, GM_WIDTH:]
    vb = (v * lax.rsqrt(jnp.mean(v * v, axis=-1, keepdims=True) + EPS) * gmn[...]).astype(BF16)
    first = lax.broadcasted_iota(jnp.int32, (GM_CHUNK, LANES), 1) < LANES // 2
    for c0 in range(0, tm, GM_CHUNK):
        for p in range(GM_WIDTH // LANES):
            cols = slice(p * LANES, (p + 1) * LANES)
            vp = vb[c0:c0 + GM_CHUNK, cols]
            zero = jnp.zeros_like(vp)
            stacked = jnp.concatenate([jnp.where(first, vp, zero), jnp.where(first, zero, vp)], axis=0)
            mixed = _dot(ws[p], stacked) + bsp[:, cols]
            ygm_o[0, c0:c0 + GM_CHUNK, cols] = (u[c0:c0 + GM_CHUNK, cols] * mixed).astype(BF16)

    for n0 in range(0, gate_o.shape[2], cw):
        gate_o[0, :, n0:n0 + cw] = _sigmoid(_dot(hb, wg[:, n0:n0 + cw]) + bg[:, n0:n0 + cw]).astype(BF16)


def _input_projection(x, mod6, lw):
    b, t, d = x.shape
    tm = min(TOKEN_TILE, t)
    tok = lambda c: pl.BlockSpec((1, tm, c), lambda i, j: (i, j, 0))
    consts = [lw["norm1"], lw["w_z"], lw["w_xbc"], lw["w_dt"], lw["w_q"], lw["w_k"], lw["w_v"], lw["w_uv"],
              lw["w_gate"], lw["dt_bias"], lw["b_gate"], lw["q_norm"], lw["k_norm"], lw["head_blk"],
              lw["gm_norm"], lw["w_spatial"], lw["b_spatial"]]
    widths = [(SSD_INNER, BF16), (SSD_CONV_DIM, BF16), (LANES, F32), (NA_WIDTH, BF16), (NA_WIDTH, BF16),
              (NA_WIDTH, BF16), (GM_WIDTH, BF16), (3 * d, BF16)]
    return pl.pallas_call(
        _inproj_kernel,
        out_shape=[jax.ShapeDtypeStruct((b, t, c), ty) for c, ty in widths],
        grid=(b, t // tm),
        in_specs=[tok(d), pl.BlockSpec((1, 6, d), lambda i, j: (i, 0, 0))] + [_const_spec(a.shape) for a in consts],
        out_specs=[tok(c) for c, _ in widths],
        compiler_params=_params("parallel", "parallel"),
        name="input_projection",
    )(x, mod6, *consts)


def _ssd_chunks(seqs, a_row, e2_ref, tri2_ref, reverse, head_off):
    q = seqs[0][0].shape[0]
    last = 0 if reverse else q - 1
    gw = SSD_INNER // SSD_GROUPS
    hpg = gw // SSD_HEAD_DIM
    groups = range(SSD_GROUPS)
    gcols = lambda g: slice(g * gw, (g + 1) * gw)
    ncols = lambda g: slice(g * SSD_STATE, (g + 1) * SSD_STATE)
    tri2 = tri2_ref[...]

    acums = []
    for xs, bm, cm, dt, h_ref in seqs:
        a = dt * a_row
        a1 = a.astype(BF16)
        r1 = a - a1.astype(F32)
        a2 = r1.astype(BF16)
        a3 = (r1 - a2.astype(F32)).astype(BF16)
        acums.append(_dot(tri2, jnp.concatenate([a1, a2], axis=0)) + _dot(tri2[:, :q], a3))

    wides = []
    for (xs, bm, cm, dt, h_ref), acum in zip(seqs, acums):
        a_last = acum[last:last + 1, :]
        per_head = jnp.concatenate([dt, dt * jnp.exp2(a_last - acum),
                                    jnp.broadcast_to(jnp.exp2(a_last), (8, LANES))], axis=0)
        hi, lo = _split_hi_lo(per_head)
        wides.append(_dot(jnp.concatenate([hi, lo], axis=1), e2_ref[...]))

    cbs, y_inters, bts, h_prevs = [], [], [], []
    for xs, bm, cm, dt, h_ref in seqs:
        cgbs = [cm[:, ncols(g)].astype(BF16) for g in groups]
        h_prev = [h_ref[:, gcols(g)] for g in groups]
        cbs.append([_dot_nt(cgbs[g], bm[:, ncols(g)].astype(BF16)) for g in groups])
        y_inters.append([_dot(cgbs[g], h_prev[g].astype(BF16)) for g in groups])
        bts.append([bm[:, ncols(g)].astype(F32).T.astype(BF16) for g in groups])
        h_prevs.append(h_prev)

    row = lax.broadcasted_iota(jnp.int32, (q, q), 0)
    col = lax.broadcasted_iota(jnp.int32, (q, q), 1)
    in_scan = (col >= row) if reverse else (col <= row)
    lane = lax.broadcasted_iota(jnp.int32, (q, LANES), 1)
    lane_head = lax.broadcasted_iota(jnp.int32, (q, gw), 1) // SSD_HEAD_DIM

    lhs, rhs, dec_outs, xw_bs, h_decs = [], [], [], [], []
    for si, (xs, bm, cm, dt, h_ref) in enumerate(seqs):
        acum, wide = acums[si], wides[si]
        acum_t = acum.T
        xdt_b = (xs * wide[0:q]).astype(BF16)
        xw_bs.append((xs * wide[q:2 * q]).astype(BF16))
        h_decs.append(wide[2 * q:2 * q + 1])
        for g in groups:
            xg = xdt_b[:, gcols(g)]
            ms, decs = [], []
            for j in range(hpg):
                hc = head_off + g * hpg + j
                colx = jnp.broadcast_to(acum[:, hc:hc + 1], (q, q))
                decay = jnp.exp2(jnp.where(in_scan, colx - acum_t[hc:hc + 1, :], NEG))
                ms.append((cbs[si][g] * decay).astype(BF16))
                decs.append(jnp.exp2(colx))
            lhs.append(jnp.concatenate(ms, axis=1))
            rhs.append(jnp.concatenate([jnp.where(lane_head == j, xg, jnp.zeros_like(xg)) for j in range(hpg)], axis=0))
            dec_outs.append(jnp.concatenate([jnp.where(lane < SSD_HEAD_DIM, decs[2 * i], decs[2 * i + 1])
                                             for i in range(hpg // 2)], axis=1))
    y_intra = [_dot(l, r) for l, r in zip(lhs, rhs)]
    h_add = [_dot(bts[si][g], xw_bs[si][:, gcols(g)]) for si in range(len(seqs)) for g in groups]
    ys = []
    for si, (xs, bm, cm, dt, h_ref) in enumerate(seqs):
        for g in groups:
            h_ref[:, gcols(g)] = h_prevs[si][g] * h_decs[si][:, gcols(g)] + h_add[si * SSD_GROUPS + g]
        ys.append(jnp.concatenate([y_intra[si * SSD_GROUPS + g] + y_inters[si][g] * dec_outs[si * SSD_GROUPS + g]
                                   for g in groups], axis=1))
    return ys


def _ssd_fwd_kernel(cur_ref, prev_ref, next_ref, dt_ref, cos_ref, sin_ref, cw_ref, cb_ref, shift_ref, a_ref, e_ref,
                    tri_ref, dsk_ref, h0_ref, xbc_o, y_o, hn_o, h_ref):
    i = pl.program_id(1)
    nc = pl.num_programs(1)
    nb, q, _ = cur_ref.shape

    @pl.when(i == 0)
    def _():
        h_ref[...] = h0_ref[...]

    pv = jnp.where(i == 0, 0.0, 1.0).astype(BF16)
    nv = jnp.where(i == nc - 1, 0.0, 1.0).astype(BF16)
    lane = lax.broadcasted_iota(jnp.int32, (q, LANES), 1)
    cos = cos_ref[...]
    sin = sin_ref[...]

    def rope(t):
        sw = jnp.where((lane & ROPE_FREQS) == 0, pltpu.roll(t, LANES - ROPE_FREQS, 1), pltpu.roll(t, ROPE_FREQS, 1))
        return t * cos + sw * sin

    cwid = 256
    seqs = []
    for s in range(nb):
        ext = jnp.concatenate([prev_ref[s] * pv, cur_ref[s], next_ref[s] * nv], axis=0)
        parts = []
        for c0 in range(0, SSD_CONV_DIM, cwid):
            e = ext[:, c0:c0 + cwid]
            taps = jnp.concatenate([e * cw_ref[k:k + 1, c0:c0 + cwid] for k in range(SSD_CONV)], axis=0)
            parts.append(_silu(_dot(shift_ref[...], taps) + cb_ref[:, c0:c0 + cwid]))
        xs = jnp.concatenate(parts[:SSD_INNER // cwid], axis=1)
        bc = jnp.concatenate(parts[SSD_INNER // cwid:], axis=1)
        bc = jnp.concatenate([rope(bc[:, g * LANES:(g + 1) * LANES]) for g in range(2 * SSD_GROUPS)], axis=1)
        xbc_o[s, :, :SSD_INNER] = xs.astype(BF16)
        xbc_o[s, :, SSD_INNER:] = bc.astype(BF16)
        seqs.append((xs, bc[:, :SSD_BC], bc[:, SSD_BC:], dt_ref[s], h_ref.at[s]))
    ys = _ssd_chunks(seqs, a_ref[...], e_ref, tri_ref, False, 0)
    for s in range(nb):
        y_o[s] = (ys[s] + dsk_ref[...] * seqs[s][0]).astype(BF16)

    @pl.when(i == nc - 1)
    def _():
        hn_o[...] = h_ref[...]


def _ssd_bwd_kernel(xbc_ref, dt_ref, yf_ref, a_ref, e_ref, tri_ref, h0_ref, y_o, hn_o, h_ref):
    i = pl.program_id(1)

    @pl.when(i == 0)
    def _():
        h_ref[...] = h0_ref[...]

    nb = xbc_ref.shape[0]
    seqs = [(xbc_ref[s, :, :SSD_INNER].astype(F32), xbc_ref[s, :, SSD_INNER:SSD_INNER + SSD_BC],
             xbc_ref[s, :, SSD_INNER + SSD_BC:], dt_ref[s], h_ref.at[s]) for s in range(nb)]
    ys = _ssd_chunks(seqs, a_ref[...], e_ref, tri_ref, True, SSD_HEADS)
    for s in range(nb):
        y_o[s] = (ys[s] + yf_ref[s].astype(F32)).astype(BF16)

    @pl.when(i == pl.num_programs(1) - 1)
    def _():
        hn_o[...] = h_ref[...]


def _ssd(xbc, dt, h0_f, h0_b, rope_cos, rope_sin, lw, consts):
    b, t, _ = xbc.shape
    q = SSD_CHUNK
    nc = t // q
    nb = SSD_SEQS_PER_STEP if b % SSD_SEQS_PER_STEP == 0 else 1
    hb = q // HALO
    n_hb = t // HALO
    chunk = lambda c: pl.BlockSpec((nb, q, c), lambda i, j: (i, j, 0))
    state = pl.BlockSpec((nb, SSD_STATE, SSD_INNER), lambda i, j: (i, 0, 0))
    state_shape = jax.ShapeDtypeStruct((b, SSD_STATE, SSD_INNER), F32)
    scratch_h = pltpu.VMEM((nb, SSD_STATE, SSD_INNER), F32)

    xbc_c, y_f, hn_f = pl.pallas_call(
        _ssd_fwd_kernel,
        out_shape=[jax.ShapeDtypeStruct((b, t, SSD_CONV_DIM), BF16), jax.ShapeDtypeStruct((b, t, SSD_INNER), BF16),
                   state_shape],
        grid=(b // nb, nc),
        in_specs=[chunk(SSD_CONV_DIM),
                  pl.BlockSpec((nb, HALO, SSD_CONV_DIM), lambda i, j: (i, jnp.maximum(j * hb - 1, 0), 0)),
                  pl.BlockSpec((nb, HALO, SSD_CONV_DIM), lambda i, j: (i, jnp.minimum((j + 1) * hb, n_hb - 1), 0)),
                  chunk(LANES),
                  pl.BlockSpec((q, LANES), lambda i, j: (j, 0)),
                  pl.BlockSpec((q, LANES), lambda i, j: (j, 0)),
                  _const_spec(lw["conv_w"].shape), _const_spec(lw["conv_b"].shape),
                  _const_spec(consts["conv_shift"].shape),
                  _const_spec(lw["a_fwd"].shape), _const_spec(consts["e_fwd"].shape),
                  _const_spec(consts["tri_fwd"].shape), _const_spec(lw["d_skip"].shape), state],
        out_specs=[chunk(SSD_CONV_DIM), chunk(SSD_INNER), state],
        scratch_shapes=[scratch_h],
        compiler_params=_params("parallel", "arbitrary"),
        name="ssd_forward",
    )(xbc, xbc, xbc, dt, rope_cos, rope_sin, lw["conv_w"], lw["conv_b"], consts["conv_shift"],
      lw["a_fwd"], consts["e_fwd"], consts["tri_fwd"], lw["d_skip"], h0_f)

    rchunk = lambda c: pl.BlockSpec((nb, q, c), lambda i, j: (i, nc - 1 - j, 0))
    y, hn_b = pl.pallas_call(
        _ssd_bwd_kernel,
        out_shape=[jax.ShapeDtypeStruct((b, t, SSD_INNER), BF16), state_shape],
        grid=(b // nb, nc),
        in_specs=[rchunk(SSD_CONV_DIM), rchunk(LANES), rchunk(SSD_INNER),
                  _const_spec(lw["a_bwd"].shape), _const_spec(consts["e_bwd"].shape),
                  _const_spec(consts["tri_bwd"].shape), state],
        out_specs=[rchunk(SSD_INNER), state],
        scratch_shapes=[scratch_h],
        compiler_params=_params("parallel", "arbitrary"),
        name="ssd_backward",
    )(xbc_c, dt, y_f, lw["a_bwd"], consts["e_bwd"], consts["tri_bwd"], h0_b)
    return y, hn_f, hn_b


def _bias_kernel(rpb_ref, o_ref):
    lh = pl.program_id(0)
    n_ri = 2 * NA_ROWS - 1
    n_ci = 2 * NA_COLS - 1
    lane = lax.broadcasted_iota(jnp.int32, (GRID_W, LANES), 1)
    qc = lax.broadcasted_iota(jnp.int32, (GRID_W, LANES), 0)
    kc = lane % GRID_W
    cs = jnp.clip(qc - NA_COLS // 2, 0, GRID_W - NA_COLS)
    col_ok = (kc >= cs) & (kc < cs + NA_COLS)
    ci = jnp.clip(kc - qc + (NA_COLS - 1), 0, n_ci - 1)
    tiles = []
    for ri in range(n_ri):
        base = (lh * n_ri + ri) * n_ci
        acc = jnp.full((GRID_W, LANES), NEG, F32)
        for c in range(n_ci):
            acc = jnp.where(col_ok & (ci == c), rpb_ref[base + c] * LOG2E, acc)
        tiles.append(acc)
    masked = jnp.full((GRID_W, LANES), NEG, F32)
    tiles = [masked] + tiles + [masked]
    for e in range(n_ri + 1):
        o_ref[0, e] = jnp.where(lane < GRID_W, tiles[e], tiles[e + 1])


def _bias_table(rpb):
    depth, heads, n_ri, n_ci = rpb.shape
    return pl.pallas_call(
        _bias_kernel,
        out_shape=jax.ShapeDtypeStruct((depth * heads, n_ri + 1, GRID_W, LANES), F32),
        grid=(depth * heads,),
        in_specs=[pl.BlockSpec(memory_space=pltpu.SMEM)],
        out_specs=pl.BlockSpec((1, n_ri + 1, GRID_W, LANES), lambda i: (i, 0, 0, 0)),
        compiler_params=pltpu.CompilerParams(dimension_semantics=("arbitrary",)),
        name="na_bias_table",
    )(rpb.reshape(-1))


def _na_tile(q_ref, k_ref, v_ref, kc_ref, vc_ref, bias_ref, o_ref, row0, start, nw, plan):
    tq = NA_QROWS * GRID_W
    qrows = slice(row0, row0 + tq)
    n_ctx = kc_ref.shape[1]
    n_cb = n_ctx // LANES
    lane_q = lax.broadcasted_iota(jnp.int32, (tq, LANES), 1)
    lane_k = lax.broadcasted_iota(jnp.int32, (n_ctx + nw * GRID_W, LANES), 1)
    n_pairs = NA_WIDTH // LANES
    own = [lambda lane, hh=hh: (lane < NA_HEAD_DIM) == (hh == 0) for hh in range(2)]
    col = lambda p: slice(p * LANES, (p + 1) * LANES)
    scores_all = []
    for p in range(n_pairs):
        qp = q_ref[0, qrows, col(p)]
        keys = jnp.concatenate([kc_ref[0, :, col(p)], k_ref[0, pl.ds(start, nw * GRID_W), col(p)]], axis=0)
        for hh in range(2):
            scores_all.append(_dot_nt(jnp.where(own[hh](lane_q), qp, jnp.zeros_like(qp)), keys))
    p_mats = []
    for h, scores in enumerate(scores_all):
        p_rows = []
        for qi, row_plan in enumerate(plan):
            rows = slice(qi * GRID_W, (qi + 1) * GRID_W)
            blocks = [scores[rows, m * LANES:(m + 1) * LANES] for m in range(n_cb)]
            for m, (ent, ok) in enumerate(row_plan):
                if ok is None:
                    blocks.append(None)
                    continue
                sb = scores[rows, (n_cb + m) * LANES:(n_cb + m + 1) * LANES] + bias_ref[h, ent]
                blocks.append(sb if ok is True else jnp.where(ok, sb, NEG))
            live = [sb for sb in blocks if sb is not None]
            mx = live[0]
            for sb in live[1:]:
                mx = jnp.maximum(mx, sb)
            mx = jnp.max(mx, axis=-1, keepdims=True)
            p_rows.append(jnp.concatenate(
                [jnp.zeros((GRID_W, LANES), BF16) if sb is None else jnp.exp2(sb - mx).astype(BF16)
                 for sb in blocks], axis=1))
        p_mats.append(jnp.concatenate(p_rows, axis=0))
    for p in range(n_pairs):
        vals = jnp.concatenate([vc_ref[0, :, col(p)], v_ref[0, pl.ds(start, nw * GRID_W), col(p)]], axis=0)
        nums = [_dot(p_mats[2 * p + hh], jnp.where(own[hh](lane_k), vals, jnp.ones_like(vals))) for hh in range(2)]
        num = jnp.where(lane_q < NA_HEAD_DIM, nums[0], nums[1])
        den = pltpu.roll(jnp.where(lane_q < NA_HEAD_DIM, nums[1], nums[0]), NA_HEAD_DIM, 1)
        o_ref[0, qrows, col(p)] = (num / den).astype(BF16)


def _na_kernel(q_ref, k_ref, v_ref, kc_ref, vc_ref, bias_ref, o_ref, *, nw, rows_n):
    rt = NA_QROWS
    half = NA_ROWS // 2
    lane_r = lax.broadcasted_iota(jnp.int32, (GRID_W, LANES), 1)
    first_half = lane_r < GRID_W
    nblk = nw // 2
    args = (q_ref, k_ref, v_ref, kc_ref, vc_ref, bias_ref, o_ref)

    def interior_tile(r0, row0):
        plan = []
        for qi in range(rt):
            row_plan = []
            for m in range(nblk):
                ok0 = qi <= 2 * m < qi + NA_ROWS
                ok1 = qi <= 2 * m + 1 < qi + NA_ROWS
                ok = True if ok0 and ok1 else None if not (ok0 or ok1) else first_half if ok0 else ~first_half
                row_plan.append((half + 2 * m - qi, ok))
            plan.append(row_plan)
        _na_tile(*args, row0, pl.multiple_of((r0 - half) * GRID_W, GRID_W), nw, plan)

    def clipped_tile(r0, row0):
        base = jnp.clip(r0 - half, 0, rows_n - nw)
        plan = []
        for qi in range(rt):
            r = r0 + qi
            rs = jnp.clip(r - half, 0, rows_n - NA_ROWS)
            row_plan = []
            for m in range(nblk):
                j0 = base + 2 * m
                ok0 = (j0 >= rs) & (j0 < rs + NA_ROWS)
                ok1 = (j0 + 1 >= rs) & (j0 + 1 < rs + NA_ROWS)
                ok = jnp.where(first_half, ok0.astype(jnp.int32), ok1.astype(jnp.int32)) > 0
                row_plan.append((jnp.clip(j0 - r + NA_ROWS, 0, 2 * NA_ROWS - 1), ok))
            plan.append(row_plan)
        _na_tile(*args, row0, pl.multiple_of(base * GRID_W, GRID_W), nw, plan)

    for sub in range(NA_TILES_PER_STEP):
        r0 = (pl.program_id(1) * NA_TILES_PER_STEP + sub) * rt
        interior = (r0 >= half) & (r0 - half <= rows_n - nw)
        pl.when(interior)(functools.partial(interior_tile, r0, sub * rt * GRID_W))
        pl.when(~interior)(functools.partial(clipped_tile, r0, sub * rt * GRID_W))


def _neighbourhood_attention(q, k, v, kc, vc, bias, layer):
    b, t, w = q.shape
    n_ctx = kc.shape[1]
    rows_n = t // GRID_W
    nw = NA_QROWS + NA_ROWS
    nw += nw % 2
    tq = NA_TILES_PER_STEP * NA_QROWS * GRID_W
    whole = lambda n: pl.BlockSpec((1, n, w), lambda i, j: (i, 0, 0))
    return pl.pallas_call(
        functools.partial(_na_kernel, nw=nw, rows_n=rows_n),
        out_shape=jax.ShapeDtypeStruct((b, t, w), BF16),
        grid=(b, t // tq),
        in_specs=[pl.BlockSpec((1, tq, w), lambda i, j: (i, j, 0)), whole(t), whole(t), whole(n_ctx), whole(n_ctx),
                  pl.BlockSpec((NA_HEADS,) + bias.shape[1:], lambda i, j: (layer, 0, 0, 0),
                               pipeline_mode=pl.Buffered(1))],
        out_specs=pl.BlockSpec((1, tq, w), lambda i, j: (i, j, 0)),
        compiler_params=_params("parallel", "arbitrary"),
        name="neighbourhood_attention",
    )(q, k, v, kc, vc, bias)


def _ctx_attn_kernel(q_ref, k_ref, v_ref, o_ref):
    n = q_ref.shape[1]
    lane = lax.broadcasted_iota(jnp.int32, (n, LANES), 1)
    for p in range(NA_WIDTH // LANES):
        cols = slice(p * LANES, (p + 1) * LANES)
        qp = q_ref[0, :, cols]
        kp = k_ref[0, :, cols]
        vp = v_ref[0, :, cols]
        nums = []
        for hh in range(2):
            own = (lane < NA_HEAD_DIM) == (hh == 0)
            s = _dot_nt(jnp.where(own, qp, jnp.zeros_like(qp)), kp)
            pm = jnp.exp2(s - jnp.max(s, axis=-1, keepdims=True)).astype(BF16)
            nums.append(_dot(pm, jnp.where(own, vp, jnp.ones_like(vp))))
        num = jnp.where(lane < NA_HEAD_DIM, nums[0], nums[1])
        den = pltpu.roll(jnp.where(lane < NA_HEAD_DIM, nums[1], nums[0]), NA_HEAD_DIM, 1)
        o_ref[0, :, cols] = (num / den).astype(BF16)


def _context_attention(q, k, v):
    b, n, w = q.shape
    spec = pl.BlockSpec((1, n, w), lambda i: (i, 0, 0))
    return pl.pallas_call(
        _ctx_attn_kernel,
        out_shape=jax.ShapeDtypeStruct((b, n, w), BF16),
        grid=(b,),
        in_specs=[spec, spec, spec],
        out_specs=spec,
        compiler_params=_params("parallel"),
        name="context_attention",
    )(q, k, v)


def _merge_ffn_kernel(x_ref, mod_ref, gate_ref, yssd_ref, z_ref, yna_ref, ygm_ref, sn_ref, wa, wb, wc, wo,
                      n2_ref, wfi, wfo, o_ref, *, ffn_chunk):
    d = x_ref.shape[2]
    y = yssd_ref[0].astype(F32) * _silu(z_ref[0].astype(F32))
    y = (y * lax.rsqrt(jnp.mean(y * y, axis=-1, keepdims=True) + EPS) * sn_ref[...]).astype(BF16)
    mixed = (gate_ref[0, :, 0:d].astype(F32) * _dot(y, wa[...])
             + gate_ref[0, :, d:2 * d].astype(F32) * _dot(yna_ref[0], wb[...])
             + gate_ref[0, :, 2 * d:3 * d].astype(F32) * _dot(ygm_ref[0], wc[...]))
    x1 = x_ref[0] + mod_ref[0, 2:3, :] * _dot(mixed.astype(BF16), wo[...])
    xn = x1 * lax.rsqrt(jnp.mean(x1 * x1, axis=-1, keepdims=True) + EPS) * n2_ref[...]
    hb = (xn * (1.0 + mod_ref[0, 4:5, :]) + mod_ref[0, 3:4, :]).astype(BF16)
    hid = wfo.shape[0]
    acc = jnp.zeros_like(x1)
    for c0 in range(0, hid, ffn_chunk):
        a = _dot(hb, wfi[:, c0:c0 + ffn_chunk])
        g = _dot(hb, wfi[:, hid + c0:hid + c0 + ffn_chunk])
        acc = acc + _dot((_silu(a) * g).astype(BF16), wfo[c0:c0 + ffn_chunk, :])
    o_ref[0] = x1 + mod_ref[0, 5:6, :] * acc


def _merge_ffn(x, mod6, gate, y_ssd, z, y_na, y_gm, lw):
    b, t, d = x.shape
    tm = min(TOKEN_TILE, t)
    tok = lambda c: pl.BlockSpec((1, tm, c), lambda i, j: (i, j, 0))
    consts = [lw["ssd_norm"], lw["w_branch_ssd"], lw["w_branch_na"], lw["w_branch_gm"], lw["w_out"], lw["norm2"],
              lw["w_ffn_in"], lw["w_ffn_out"]]
    return pl.pallas_call(
        functools.partial(_merge_ffn_kernel, ffn_chunk=256),
        out_shape=jax.ShapeDtypeStruct((b, t, d), F32),
        grid=(b, t // tm),
        in_specs=[tok(d), pl.BlockSpec((1, 6, d), lambda i, j: (i, 0, 0)),
                  tok(3 * d), tok(SSD_INNER), tok(SSD_INNER), tok(NA_WIDTH), tok(GM_WIDTH)]
                 + [_const_spec(a.shape) for a in consts],
        out_specs=tok(d),
        compiler_params=_params("parallel", "parallel"),
        name="merge_out_ffn",
    )(x, mod6, gate, y_ssd, z, y_na, y_gm, *consts)


def _cast_kernel(w_ref, o_ref):
    o_ref[...] = w_ref[0].astype(BF16)


def _weight_bf16(w, l):
    _, rows, cols = w.shape
    rb = next(r for r in (rows, rows // 2, rows // 4, rows // 8)
              if r * cols * 4 <= CAST_BLOCK_BYTES and r % 16 == 0)
    return pl.pallas_call(
        _cast_kernel,
        out_shape=jax.ShapeDtypeStruct((rows, cols), BF16),
        grid=(rows // rb, 1),
        in_specs=[pl.BlockSpec((1, rb, cols), lambda i, j: (l, i, j))],
        out_specs=pl.BlockSpec((rb, cols), lambda i, j: (i, j)),
        compiler_params=_params("parallel", "parallel"),
        name="weight_to_bf16",
    )(w)


def _shared_constants(n_ctx, seq):
    pos = jnp.arange(seq)
    freqs = ROPE_BASE ** (-jnp.arange(ROPE_FREQS, dtype=F32) / ROPE_FREQS)
    ang_row = (pos // GRID_W).astype(F32)[:, None] * freqs
    ang_col = (pos % GRID_W).astype(F32)[:, None] * freqs
    cos = jnp.concatenate([jnp.cos(ang_row), jnp.cos(ang_row), jnp.cos(ang_col), jnp.cos(ang_col)], axis=1)
    sin = jnp.concatenate([-jnp.sin(ang_row), jnp.sin(ang_row), -jnp.sin(ang_col), jnp.sin(ang_col)], axis=1)

    r = jnp.arange(SSD_CHUNK)
    twice = lambda m, axis: jnp.concatenate([m, m], axis=axis).astype(BF16)
    lane_head = jnp.arange(SSD_INNER) // SSD_HEAD_DIM
    rows = jnp.arange(LANES)
    src = jnp.arange(SSD_CHUNK + 2 * HALO)
    conv_shift = jnp.concatenate([(src[None, :] == r[:, None] + HALO + k - SSD_CONV // 2) for k in range(SSD_CONV)],
                                 axis=1).astype(BF16)
    return dict(rope_cos=cos, rope_sin=sin,
                ctx_cos=jnp.ones((n_ctx, LANES), F32), ctx_sin=jnp.zeros((n_ctx, LANES), F32),
                tri_fwd=twice(r[None, :] <= r[:, None], 1), tri_bwd=twice(r[None, :] >= r[:, None], 1),
                e_fwd=twice(rows[:, None] == lane_head[None, :], 0),
                e_bwd=twice(rows[:, None] == lane_head[None, :] + SSD_HEADS, 0),
                conv_shift=conv_shift)


def _layer_weights(l, p):
    d = p["w_in"].shape[1]
    sizes = (SSD_INNER, SSD_CONV_DIM, 2 * SSD_HEADS, NA_WIDTH, NA_WIDTH, NA_WIDTH, 2 * GM_WIDTH, 3 * d)
    names = ("w_z", "w_xbc", "w_dt", "w_q", "w_k", "w_v", "w_uv", "w_gate")
    lw, start = {}, 0
    w_in = p["w_in"][l]
    for name, size in zip(names, sizes):
        lw[name] = w_in[:, start:start + size].astype(BF16)
        start += size
    pad_lanes = lambda v: jnp.pad(v, (0, LANES - v.shape[0])).reshape(1, LANES)
    lw["w_dt"] = jnp.pad(lw["w_dt"], ((0, 0), (0, LANES - 2 * SSD_HEADS)))
    lw["dt_bias"] = pad_lanes(p["dt_bias"][l].reshape(-1))
    a = -jnp.exp(p["a_log"][l].astype(F32))
    lw["a_fwd"] = pad_lanes(a[0] * LOG2E)
    lw["a_bwd"] = pad_lanes(jnp.concatenate([jnp.zeros((SSD_HEADS,), F32), a[1] * LOG2E]))
    row = lambda v: v.reshape(1, -1)
    lw["norm1"] = row(p["norm1"][l])
    lw["norm2"] = row(p["norm2"][l])
    lw["b_gate"] = row(p["b_gate"][l])
    lw["q_norm"] = row(jnp.tile(p["q_norm"][l], NA_HEADS))
    lw["k_norm"] = row(jnp.tile(p["k_norm"][l], NA_HEADS))
    head = jnp.arange(NA_WIDTH) // NA_HEAD_DIM
    lw["head_blk"] = ((head[:, None] == head[None, :]).astype(F32) / NA_HEAD_DIM).astype(BF16)
    lw["gm_norm"] = row(p["gm_norm"][l])
    w_s = p["w_spatial"][l].astype(BF16)
    lw["w_spatial"] = jnp.concatenate([w_s[0::2], w_s[1::2]], axis=2)
    lw["b_spatial"] = jnp.repeat(p["b_spatial"][l].T, GM_WIDTH // GM_GROUPS, axis=1)
    lw["conv_w"] = jnp.pad(p["conv_w"][l], ((0, 8 - SSD_CONV), (0, 0))).astype(BF16)
    lw["conv_b"] = row(p["conv_b"][l])
    lw["d_skip"] = row(jnp.repeat(p["d_skip"][l], SSD_HEAD_DIM))
    lw["ssd_norm"] = row(p["ssd_norm"][l])
    for name in ("w_branch_ssd", "w_branch_na", "w_branch_gm", "w_out", "w_ffn_in", "w_ffn_out"):
        lw[name] = _weight_bf16(p[name], l)
    return lw


def kernel(x, c, ctx, c_ctx, w_mod, b_mod, norm1, w_in, b_gate, conv_w, conv_b, a_log, dt_bias, d_skip, ssd_norm,
           q_norm, k_norm, rpb, gm_norm, w_spatial, b_spatial, w_branch_ssd, w_branch_na, w_branch_gm, w_out,
           norm2, w_ffn_in, w_ffn_out):
    p = dict(norm1=norm1, w_in=w_in, b_gate=b_gate, conv_w=conv_w, conv_b=conv_b, a_log=a_log, dt_bias=dt_bias,
             d_skip=d_skip, ssd_norm=ssd_norm, q_norm=q_norm, k_norm=k_norm, gm_norm=gm_norm, w_spatial=w_spatial,
             b_spatial=b_spatial, w_branch_ssd=w_branch_ssd, w_branch_na=w_branch_na, w_branch_gm=w_branch_gm,
             w_out=w_out, norm2=norm2, w_ffn_in=w_ffn_in, w_ffn_out=w_ffn_out)
    b, seq, d = x.shape
    n_ctx = ctx.shape[1]
    depth = w_mod.shape[0]

    c_all = jnp.zeros((8, d), F32).at[:b].set(c).at[b].set(c_ctx)
    mod = _modulation(c_all, w_mod, b_mod)
    bias = _bias_table(rpb)
    consts = _shared_constants(n_ctx, seq)
    zero_state = jnp.zeros((b, SSD_STATE, SSD_INNER), F32)

    xc = ctx
    for l in range(depth):
        lw = _layer_weights(l, p)
        mod_x = mod[l, :b].reshape(b, 6, d)
        mod_c = mod[l, b].reshape(1, 6, d)
        last = l == depth - 1

        flat = lambda t: t.reshape(1, b * n_ctx, t.shape[-1])
        per_sample = lambda t: t.reshape(b, n_ctx, t.shape[-1])
        zc, xbcc, dtc, qc, kc, vc, ygm_c, gate_c = [per_sample(t) for t in _input_projection(flat(xc), mod_c, lw)]
        z, xbc, dt, q, k, v, y_gm, gate = _input_projection(x, mod_x, lw)

        yssd_c, s_f, s_b = _ssd(xbcc, dtc, zero_state, zero_state, consts["ctx_cos"], consts["ctx_sin"], lw, consts)
        y_ssd, _, _ = _ssd(xbc, dt, s_f, s_b, consts["rope_cos"], consts["rope_sin"], lw, consts)

        y_na = _neighbourhood_attention(q, k, v, kc, vc, bias, l)
        x = _merge_ffn(x, mod_x, gate, y_ssd, z, y_na, y_gm, lw)
        if not last:
            yna_c = _context_attention(qc, kc, vc)
            xc = per_sample(_merge_ffn(flat(xc), mod_c, flat(gate_c), flat(yssd_c), flat(zc), flat(yna_c), flat(ygm_c), lw))
    return x
```
